```python
import math
import jax
import jax.numpy as jnp
from jax import lax
import numpy as np

D_MODEL = 1024
BATCH = 4
SEQ = 8192
DEPTH = 1

GRID_W = 64
CTX_LEN = 256
NH_A = 8
DK_A = 64
DV_A = 128
D_A = NH_A * DV_A
CHUNK = 128
GATE_CAP = 15.0
NH_B = 16
DH_B = 64
D_B = NH_B * DH_B
WIN_R = 8
WIN_C = 16
QB_R_MAX = 8
QB_C = 16
D_FF = -(-8 * D_MODEL // (3 * 256)) * 256
ROPE_THETA = 10000.0
EPS = 1e-6
N_MOD = 6
_PART_NAMES = ('q_a', 'k_a', 'v_a', 'o_a', 'gates_a', 'q_b', 'k_b', 'v_b', 'merge')
_PART_SIZES = (NH_A * DK_A, NH_A * DK_A, D_A, D_A, 4 * NH_A, D_B, D_B, D_B, 2 * D_MODEL)
D_IN = sum(_PART_SIZES)

kernel_name = 'hybrid_mlstm_natten_dit_layer'


def _rmsnorm(x, g):
    xf = x.astype(jnp.float32)
    y = xf * lax.rsqrt(jnp.mean(xf * xf, axis=-1, keepdims=True) + EPS)
    return (y * g.astype(jnp.float32)).astype(x.dtype)


def _proj(h, w_in, names):
    offs = np.cumsum((0,) + _PART_SIZES)
    sel = [i for i, n in enumerate(_PART_NAMES) if n in names]
    if len(sel) == len(_PART_NAMES):
        w = w_in
    else:
        w = jnp.concatenate([w_in[:, int(offs[i]):int(offs[i + 1])] for i in sel], axis=1)
    sizes = [_PART_SIZES[i] for i in sel]
    parts = jnp.split(h @ w, [int(s) for s in np.cumsum(sizes)[:-1]], axis=-1)
    return {_PART_NAMES[i]: p for i, p in zip(sel, parts)}


def _rope_axis(x, pos):
    half = x.shape[-1] // 2
    inv = ROPE_THETA ** (-jnp.arange(half, dtype=jnp.float32) / half)
    ang = pos.astype(jnp.float32)[:, None] * inv
    cos, sin = jnp.cos(ang)[:, None, :], jnp.sin(ang)[:, None, :]
    x1 = x[..., :half].astype(jnp.float32)
    x2 = x[..., half:].astype(jnp.float32)
    return jnp.concatenate([x1 * cos - x2 * sin, x2 * cos + x1 * sin], axis=-1).astype(x.dtype)


def _rope2d(x, row, col):
    d = x.shape[-1] // 2
    return jnp.concatenate([_rope_axis(x[..., :d], row), _rope_axis(x[..., d:], col)], axis=-1)


def _zero_state(b):
    return (jnp.zeros((b, NH_A, DK_A, DV_A), jnp.float32),
            jnp.zeros((b, NH_A, DK_A), jnp.float32),
            jnp.zeros((b, NH_A), jnp.float32))


def _mlstm_inputs(p, b_g, pos):
    bn, L, _ = p['q_a'].shape
    q = p['q_a'].reshape(bn, L, NH_A, DK_A)
    k = p['k_a'].reshape(bn, L, NH_A, DK_A)
    if pos is not None:
        q = _rope2d(q, pos[0], pos[1])
        k = _rope2d(k, pos[0], pos[1])
    v = p['v_a'].reshape(bn, L, NH_A, DV_A)
    bhl = lambda t: t.astype(jnp.float32).transpose(0, 2, 1, 3)
    g = p['gates_a'].reshape(bn, L, 4, NH_A).astype(jnp.float32) + b_g.astype(jnp.float32)
    g = GATE_CAP * jnp.tanh(g / GATE_CAP)
    g = g.transpose(2, 0, 3, 1)
    return (bhl(q) * DK_A ** -0.5, bhl(k), bhl(v),
            g[0], jax.nn.log_sigmoid(g[1]), g[2], jax.nn.log_sigmoid(g[3]))


def _mlstm_scan(q, k, v, ig, lf, state, with_output):
    bn, nh, L, _ = q.shape
    lc = min(CHUNK, L)
    nc = L // lc

    def to_chunks(t):
        t = t.reshape(t.shape[:2] + (nc, lc) + t.shape[3:])
        return jnp.moveaxis(t, 2, 0)

    causal = jnp.tril(jnp.ones((lc, lc), dtype=bool))

    def body(carry, xs):
        C, n, m = carry
        qc, kc, vc, igc, lfc = xs
        b = jnp.cumsum(lfc, axis=-1)
        b_last = b[..., -1]
        g_end = b_last[..., None] - b + igc
        m_new = jnp.maximum(b_last + m, jnp.max(g_end, axis=-1))
        w_end = jnp.exp(g_end - m_new[..., None])
        decay = jnp.exp(b_last + m - m_new)
        kw = kc * w_end[..., None]
        C_new = decay[..., None, None] * C + jnp.einsum('bhsd,bhse->bhde', kw, vc)
        n_new = decay[..., None] * n + jnp.sum(kw, axis=2)
        if not with_output:
            return (C_new, n_new, m_new), None
        log_d = jnp.where(causal, b[..., :, None] - b[..., None, :] + igc[..., None, :], -jnp.inf)
        log_inter = b + m[..., None]
        m_t = jnp.maximum(log_inter, jnp.max(log_d, axis=-1))
        a = jnp.exp(log_inter - m_t)
        s = jnp.einsum('bhtd,bhsd->bhts', qc, kc) * jnp.exp(log_d - m_t[..., None])
        num = a[..., None] * jnp.einsum('bhtd,bhde->bhte', qc, C) + jnp.einsum('bhts,bhse->bhte', s, vc)
        den = a * jnp.einsum('bhtd,bhd->bht', qc, n) + jnp.sum(s, axis=-1)
        h = num / jnp.maximum(jnp.abs(den), jnp.exp(-m_t))[..., None]
        return (C_new, n_new, m_new), h

    state, h = lax.scan(body, state, tuple(to_chunks(t) for t in (q, k, v, ig, lf)))
    if with_output:
        h = jnp.moveaxis(h, 0, 2).reshape(bn, nh, L, DV_A)
    return h, state


def _mlstm_bidir(inp, states0, with_output):
    q, k, v, ig_f, lf_f, ig_b, lf_b = inp
    h_f, st_f = _mlstm_scan(q, k, v, ig_f, lf_f, states0[0], with_output)
    fl = lambda t: jnp.flip(t, axis=2)
    h_b, st_b = _mlstm_scan(fl(q), fl(k), fl(v), fl(ig_b), fl(lf_b), states0[1], with_output)
    h = h_f + fl(h_b) if with_output else None
    return h, st_f, st_b


def _qk_norm(t, g):
    bn, L, _ = t.shape
    return _rmsnorm(t.reshape(bn, L, NH_B, DH_B), g)


def _axis_blocks(n, k, qb):
    kb = min(k + qb - 1, n)
    q0 = np.arange(0, n, qb)
    kstart = np.clip(q0 - k // 2, 0, n - kb)
    qpos = q0[:, None] + np.arange(qb)
    kpos = kstart[:, None] + np.arange(kb)
    wstart = np.clip(qpos - k // 2, 0, n - k)
    inwin = (kpos[:, None, :] >= wstart[:, :, None]) & (kpos[:, None, :] < wstart[:, :, None] + k)
    off = kpos[:, None, :] - qpos[:, :, None]
    return kpos, inwin, off


def _neighbourhood_attention(q, k, v, kc, vc, rpb, rows):
    bn, S, nh, d = q.shape
    kh = min(WIN_R, rows)
    qr = math.gcd(rows, QB_R_MAX)
    r_k, r_in, r_off = _axis_blocks(rows, kh, qr)
    c_k, c_in, c_off = _axis_blocks(GRID_W, WIN_C, QB_C)
    nbr, nbc = rows // qr, GRID_W // QB_C
    kr, kcol = r_k.shape[1], c_k.shape[1]
    nq, nk = qr * QB_C, kr * kcol
    ktok = (r_k[:, None, :, None] * GRID_W + c_k[None, :, None, :]).reshape(nbr, nbc, nk)
    mask = (r_in[:, None, :, None, :, None] & c_in[None, :, None, :, None, :]).reshape(nbr, nbc, nq, nk)
    dr = np.clip(r_off + WIN_R - 1, 0, 2 * WIN_R - 2)
    dc = np.clip(c_off + WIN_C - 1, 0, 2 * WIN_C - 2)
    qb = q.reshape(bn, nbr, qr, nbc, QB_C, nh, d).transpose(1, 0, 3, 2, 4, 5, 6).reshape(nbr, bn, nbc, nq, nh, d)

    def row_block(args):
        q_x, ktok_x, mask_x, dr_x = args
        kb = k[:, ktok_x]
        vb = v[:, ktok_x]
        bias = rpb[:, dr_x[None, :, None, :, None], dc[:, None, :, None, :]]
        bias = jnp.where(mask_x, bias.reshape(nh, nbc, nq, nk).astype(jnp.float32), -jnp.inf)
        s_win = jnp.einsum('byqhd,bykhd->bhyqk', q_x, kb).astype(jnp.float32) + bias[None]
        s_ctx = jnp.einsum('byqhd,bchd->bhyqc', q_x, kc).astype(jnp.float32)
        p = jax.nn.softmax(jnp.concatenate([s_win, s_ctx], axis=-1), axis=-1)
        return (jnp.einsum('bhyqk,bykhd->byqhd', p[..., :nk].astype(v.dtype), vb)
                + jnp.einsum('bhyqc,bchd->byqhd', p[..., nk:].astype(vc.dtype), vc))

    o = lax.map(row_block, (qb, jnp.asarray(ktok, jnp.int32), jnp.asarray(mask), jnp.asarray(dr, jnp.int32)))
    o = o.reshape(nbr, bn, nbc, qr, QB_C, nh, d).transpose(1, 0, 3, 2, 4, 5, 6)
    return o.reshape(bn, S, nh * d)


def _ctx_attention(q, k, v):
    bn, L, nh, d = q.shape
    s = jnp.einsum('bqhd,bkhd->bhqk', q, k).astype(jnp.float32)
    p = jax.nn.softmax(s, axis=-1).astype(v.dtype)
    return jnp.einsum('bhqk,bkhd->bqhd', p, v).reshape(bn, L, nh * d)


def _merge(h_a, o_gate, na_out, merge, g_norm, w_a, w_b, w_o):
    bn, nh, L, dv = h_a.shape
    ha = _rmsnorm(h_a.transpose(0, 2, 1, 3), g_norm.reshape(NH_A, DV_A)).reshape(bn, L, D_A)
    ha = ha.astype(o_gate.dtype) * jax.nn.sigmoid(o_gate)
    g_a, g_b = jnp.split(jax.nn.sigmoid(merge), 2, axis=-1)
    return (g_a * (ha @ w_a) + g_b * (na_out @ w_b)) @ w_o


def _swiglu(h, wg, wu, wd):
    return (jax.nn.silu(h @ wg) * (h @ wu)) @ wd


def setup_inputs(seed: int = 0) -> dict:
    key = jax.random.key(seed)
    ks = jax.random.split(key, 20)
    D = D_MODEL
    nrm = lambda kk, shape, scale: jax.random.normal(kk, shape, jnp.float32) * scale
    forget_bias = jnp.linspace(3.0, 6.0, NH_A, dtype=jnp.float32)
    gate_rows = jnp.array([0.0, 1.0, 0.0, 1.0], jnp.float32)[:, None]
    return {
        'x': nrm(ks[0], (BATCH, SEQ, D), 1.0),
        'c': nrm(ks[1], (BATCH, D), 1.0),
        'ctx': nrm(ks[2], (BATCH, CTX_LEN, D), 1.0),
        'c_ctx': nrm(ks[3], (D,), 1.0),
        'w_mod': nrm(ks[4], (DEPTH, D, N_MOD * D), 0.5 * D ** -0.5),
        'b_mod': nrm(ks[5], (DEPTH, N_MOD * D), 0.02),
        'norm1_g': 1.0 + nrm(ks[6], (DEPTH, D), 0.02),
        'w_in': nrm(ks[7], (DEPTH, D, D_IN), D ** -0.5),
        'b_gates': nrm(ks[8], (DEPTH, 4, NH_A), 0.1) + gate_rows * forget_bias,
        'mlstm_norm_g': 1.0 + nrm(ks[9], (DEPTH, D_A), 0.02),
        'qn_g': 1.0 + nrm(ks[10], (DEPTH, DH_B), 0.02),
        'kn_g': 1.0 + nrm(ks[11], (DEPTH, DH_B), 0.02),
        'rpb': nrm(ks[12], (DEPTH, NH_B, 2 * WIN_R - 1, 2 * WIN_C - 1), 0.1),
        'w_branch_a': nrm(ks[13], (DEPTH, D_A, D), D_A ** -0.5),
        'w_branch_b': nrm(ks[14], (DEPTH, D_B, D), D_B ** -0.5),
        'w_out': nrm(ks[15], (DEPTH, D, D), D ** -0.5),
        'norm2_g': 1.0 + nrm(ks[16], (DEPTH, D), 0.02),
        'w_ffn_gate': nrm(ks[17], (DEPTH, D, D_FF), D ** -0.5),
        'w_ffn_up': nrm(ks[18], (DEPTH, D, D_FF), D ** -0.5),
        'w_ffn_down': nrm(ks[19], (DEPTH, D_FF, D), D_FF ** -0.5),
    }


def reference(x, c, ctx, c_ctx, w_mod, b_mod, norm1_g, w_in, b_gates, mlstm_norm_g, qn_g, kn_g,
              rpb, w_branch_a, w_branch_b, w_out, norm2_g, w_ffn_gate, w_ffn_up, w_ffn_down):
    bn, S, _ = x.shape
    L_ctx = ctx.shape[1]
    rows = S // GRID_W
    t = jnp.arange(S)
    pos = (t // GRID_W, t % GRID_W)
    for l in range(DEPTH):
        last = l == DEPTH - 1
        sh1, sc1, g1, sh2, sc2, g2 = jnp.split(jax.nn.silu(c) @ w_mod[l] + b_mod[l], N_MOD, axis=-1)
        sh1c, sc1c, g1c, sh2c, sc2c, g2c = jnp.split(jax.nn.silu(c_ctx) @ w_mod[l] + b_mod[l], N_MOD, axis=-1)
        hx = _rmsnorm(x, norm1_g[l]) * (1.0 + sc1[:, None]) + sh1[:, None]
        hc = _rmsnorm(ctx, norm1_g[l]) * (1.0 + sc1c) + sh1c
        px = _proj(hx, w_in[l], _PART_NAMES)
        ctx_parts = _PART_NAMES if not last else ('q_a', 'k_a', 'v_a', 'gates_a', 'k_b', 'v_b')
        pc = _proj(hc, w_in[l], ctx_parts)
        hc_a, st_f, st_b = _mlstm_bidir(_mlstm_inputs(pc, b_gates[l], None),
                                        (_zero_state(bn), _zero_state(bn)), not last)
        hx_a, _, _ = _mlstm_bidir(_mlstm_inputs(px, b_gates[l], pos), (st_f, st_b), True)
        kc_b = _qk_norm(pc['k_b'], kn_g[l])
        vc_b = pc['v_b'].reshape(bn, L_ctx, NH_B, DH_B)
        qx_b = _qk_norm(px['q_b'], qn_g[l]) * DH_B ** -0.5
        kx_b = _qk_norm(px['k_b'], kn_g[l])
        vx_b = px['v_b'].reshape(bn, S, NH_B, DH_B)
        na_x = _neighbourhood_attention(qx_b, kx_b, vx_b, kc_b, vc_b, rpb[l], rows)
        mix_x = _merge(hx_a, px['o_a'], na_x, px['merge'], mlstm_norm_g[l],
                       w_branch_a[l], w_branch_b[l], w_out[l])
        x_new = x + g1[:, None] * mix_x
        hx2 = _rmsnorm(x_new, norm2_g[l]) * (1.0 + sc2[:, None]) + sh2[:, None]
        x_new = x_new + g2[:, None] * _swiglu(hx2, w_ffn_gate[l], w_ffn_up[l], w_ffn_down[l])
        if not last:
            qc_b = _qk_norm(pc['q_b'], qn_g[l]) * DH_B ** -0.5
            na_c = _ctx_attention(qc_b, kc_b, vc_b)
            mix_c = _merge(hc_a, pc['o_a'], na_c, pc['merge'], mlstm_norm_g[l],
                           w_branch_a[l], w_branch_b[l], w_out[l])
            ctx = ctx + g1c * mix_c
            hc2 = _rmsnorm(ctx, norm2_g[l]) * (1.0 + sc2c) + sh2c
            ctx = ctx + g2c * _swiglu(hc2, w_ffn_gate[l], w_ffn_up[l], w_ffn_down[l])
        x = x_new
    return x
```

```python
import functools
import math

import numpy as np
import jax
import jax.numpy as jnp
from jax import lax
from jax.experimental import pallas as pl
from jax.experimental.pallas import tpu as pltpu

F32 = jnp.float32
BF16 = jnp.bfloat16

GRID_W = 64
NH_A, DK_A, DV_A = 8, 64, 128
NH_B, DH_B = 16, 64
CHUNK = 128
GATE_CAP = 15.0
WIN_R, WIN_C = 8, 16
ROPE_THETA = 10000.0
EPS = 1e-6
N_MOD = 6
LANES = 128
VMEM_LIMIT = 56 * 1024 * 1024

QROWS = 4
KROWS = QROWS + WIN_R - 1


def _cparams(sem):
    return pltpu.CompilerParams(dimension_semantics=sem, vmem_limit_bytes=VMEM_LIMIT)


def _mod_kernel(c_ref, w_ref, b_ref, o_ref):
    c = c_ref[...]
    h = c * jax.nn.sigmoid(c)
    o_ref[...] = jnp.dot(h, w_ref[...], preferred_element_type=F32,
                         precision=lax.Precision.HIGHEST) + b_ref[...]


def _modulation(cc, w_mod, b_mod):
    rows, d = cc.shape
    n = w_mod.shape[1]
    tn = 1024
    return pl.pallas_call(
        _mod_kernel,
        grid=(n // tn,),
        in_specs=[pl.BlockSpec((rows, d), lambda j: (0, 0)),
                  pl.BlockSpec((d, tn), lambda j: (0, j)),
                  pl.BlockSpec((1, tn), lambda j: (0, j))],
        out_specs=pl.BlockSpec((rows, tn), lambda j: (0, j)),
        out_shape=jax.ShapeDtypeStruct((rows, n), F32),
        compiler_params=_cparams(("arbitrary",)),
        name="modulation",
    )(cc, w_mod, b_mod.reshape(1, n))


def _prenorm_kernel(x_ref, mod_ref, g_ref, o_ref, *, sh_idx, sc_idx):
    x = x_ref[...]
    ms = jnp.mean(x * x, axis=-1, keepdims=True)
    y = x * lax.rsqrt(ms + EPS) * g_ref[...]
    sc = mod_ref[sc_idx:sc_idx + 1, :]
    sh = mod_ref[sh_idx:sh_idx + 1, :]
    o_ref[...] = (y * (1.0 + sc) + sh).astype(o_ref.dtype)


def _prenorm(x, mod3, g, *, sh_idx, sc_idx, ctx_row=None, tm=512):
    bn, L, d = x.shape
    tm = min(tm, L)
    if ctx_row is None:
        mod_map = lambda b, i: (b, 0, 0)
    else:
        mod_map = lambda b, i: (ctx_row, 0, 0)
    return pl.pallas_call(
        functools.partial(_prenorm_kernel, sh_idx=sh_idx, sc_idx=sc_idx),
        grid=(bn, L // tm),
        in_specs=[pl.BlockSpec((None, tm, d), lambda b, i: (b, i, 0)),
                  pl.BlockSpec((None, N_MOD, d), mod_map),
                  pl.BlockSpec((1, d), lambda b, i: (0, 0))],
        out_specs=pl.BlockSpec((None, tm, d), lambda b, i: (b, i, 0)),
        out_shape=jax.ShapeDtypeStruct((bn, L, d), BF16),
        compiler_params=_cparams(("parallel", "parallel")),
        name="prenorm",
    )(x, mod3, g.reshape(1, d))


PROJ_TN = 512
BLK_QA, BLK_KA, BLK_VA, BLK_OA, BLK_QB, BLK_KB, BLK_VB, BLK_MG = 0, 1, 2, 4, 6, 8, 10, 12
N_PROJ_BLK = 16


def _swap16(x):
    n = x.shape[-1]
    lane = lax.broadcasted_iota(jnp.int32, x.shape, x.ndim - 1)
    first = (lane & 31) < 16
    return jnp.where(first, pltpu.roll(x, n - 16, x.ndim - 1), pltpu.roll(x, 16, x.ndim - 1))


def _proj_kernel(a_ref, w_ref, aux_ref, cos_ref, sin_ref, gsum_ref, o_ref, *, rope):
    j = pl.program_id(1)
    acc = jnp.dot(a_ref[...], w_ref[...], preferred_element_type=F32)
    scale = aux_ref[...]
    is_rope = j < BLK_VA
    is_norm = jnp.logical_and(j >= BLK_QB, j < BLK_VB)

    if rope:
        @pl.when(is_rope)
        def _():
            reps = PROJ_TN // LANES
            cos = jnp.concatenate([cos_ref[...]] * reps, axis=1)
            sin = jnp.concatenate([sin_ref[...]] * reps, axis=1)
            y = acc * cos + _swap16(acc) * sin
            o_ref[...] = (y * scale).astype(o_ref.dtype)
        plain = jnp.logical_not(jnp.logical_or(is_rope, is_norm))
    else:
        plain = jnp.logical_not(is_norm)

    @pl.when(is_norm)
    def _():
        sq = (acc * acc).astype(BF16)
        ss = jnp.dot(sq, gsum_ref[...], preferred_element_type=F32)
        y = acc * lax.rsqrt(ss * (1.0 / DH_B) + EPS)
        o_ref[...] = (y * scale).astype(o_ref.dtype)

    @pl.when(plain)
    def _():
        o_ref[...] = (acc * scale).astype(o_ref.dtype)


def _projection(a, w, aux, cos, sin, gsum, *, rope, seq_len, tm=1024):
    m, d = a.shape
    tm = min(tm, seq_len)
    n = w.shape[1]
    nb = n // PROJ_TN
    pos_blocks = seq_len // tm
    return pl.pallas_call(
        functools.partial(_proj_kernel, rope=rope),
        grid=(m // tm, nb),
        in_specs=[pl.BlockSpec((tm, d), lambda i, j: (i, 0)),
                  pl.BlockSpec((d, PROJ_TN), lambda i, j: (0, j)),
                  pl.BlockSpec((None, 1, PROJ_TN), lambda i, j: (j, 0, 0)),
                  pl.BlockSpec((tm, LANES), lambda i, j: (i % pos_blocks, 0)),
                  pl.BlockSpec((tm, LANES), lambda i, j: (i % pos_blocks, 0)),
                  pl.BlockSpec((PROJ_TN, PROJ_TN), lambda i, j: (0, 0))],
        out_specs=pl.BlockSpec((tm, PROJ_TN), lambda i, j: (i, j)),
        out_shape=jax.ShapeDtypeStruct((m, n), BF16),
        compiler_params=_cparams(("parallel", "arbitrary")),
        name="projection",
    )(a, w, aux, cos, sin, gsum)


def _gate_kernel(a_ref, w_ref, b_ref, o_ref):
    g = jnp.dot(a_ref[...], w_ref[...], preferred_element_type=F32) + b_ref[...]
    g = GATE_CAP * jnp.tanh(g * (1.0 / GATE_CAP))
    col = lax.broadcasted_iota(jnp.int32, g.shape, 1)
    is_forget = (col & NH_A) != 0
    o_ref[...] = jnp.where(is_forget, jax.nn.log_sigmoid(g), g)


def _gate_projection(a, w_g, b_g, tm=1024):
    m, d = a.shape
    tm = min(tm, m)
    return pl.pallas_call(
        _gate_kernel,
        grid=(m // tm,),
        in_specs=[pl.BlockSpec((tm, d), lambda i: (i, 0)),
                  pl.BlockSpec((d, LANES), lambda i: (0, 0)),
                  pl.BlockSpec((1, LANES), lambda i: (0, 0))],
        out_specs=pl.BlockSpec((tm, LANES), lambda i: (i, 0)),
        out_shape=jax.ShapeDtypeStruct((m, LANES), F32),
        compiler_params=_cparams(("parallel",)),
        name="gate_projection",
    )(a, w_g, b_g)


def _mlstm_kernel(q_ref, k_ref, v_ref, grow_ref, gcol_ref, c0_ref, h_ref, cfin_ref, c_scr, *, rev, nc):
    c = pl.program_id(2)

    @pl.when(c == 0)
    def _():
        c_scr[...] = c0_ref[...]

    lc = q_ref.shape[0]
    q = q_ref[...]
    k = k_ref[...]
    v = v_ref[...]
    ones_blk = (lax.broadcasted_iota(jnp.int32, (lc, DV_A), 1) == 0).astype(v.dtype)
    v_ext = jnp.concatenate([v, ones_blk], axis=1)

    ig_row = grow_ref[0:1, :]
    lf_row = grow_ref[1:2, :]
    ig_col = gcol_ref[:, 0:1]
    lf_col = gcol_ref[:, 1:2]

    t_idx = lax.broadcasted_iota(jnp.int32, (lc, lc), 0)
    s_idx = lax.broadcasted_iota(jnp.int32, (lc, lc), 1)
    valid = (s_idx >= t_idx) if rev else (s_idx <= t_idx)
    b_col = jnp.sum(jnp.where(valid, lf_row, 0.0), axis=1, keepdims=True)
    to_s = (t_idx >= s_idx) if rev else (t_idx <= s_idx)
    b_row = jnp.sum(jnp.where(to_s, lf_col, 0.0), axis=0, keepdims=True)
    total = jnp.sum(lf_row, axis=1, keepdims=True)

    log_d = jnp.where(valid, b_col + (ig_row - b_row), -jnp.inf)
    s = lax.dot_general(q, k, (((1,), (1,)), ((), ())), preferred_element_type=F32)
    p = (s * jnp.exp(log_d)).astype(BF16)
    qs = (q.astype(F32) * jnp.exp(b_col)).astype(BF16)
    c_old = c_scr[...]
    nd = (jnp.dot(qs, c_old.astype(BF16), preferred_element_type=F32)
          + jnp.dot(p, v_ext, preferred_element_type=F32))
    num = nd[:, :DV_A]
    den = nd[:, DV_A:DV_A + 1]
    h_ref[...] = (num / jnp.maximum(jnp.abs(den), 1.0)).astype(h_ref.dtype)

    kw = (k.astype(F32) * jnp.exp(total - b_col + ig_col)).astype(BF16)
    upd = lax.dot_general(kw, v_ext, (((0,), (0,)), ((), ())), preferred_element_type=F32)
    c_new = jnp.exp(total) * c_old + upd
    c_scr[...] = c_new

    @pl.when(c == nc - 1)
    def _():
        cfin_ref[...] = c_new


def _mlstm_scan(q, k, v, grow, gcol, c0, *, rev):
    bn, nh, L, _ = q.shape
    lc = min(CHUNK, L)
    nc = L // lc
    cidx = (lambda c: nc - 1 - c) if rev else (lambda c: c)
    seq_map = lambda b, h, c: (b, h, cidx(c), 0)
    g_map = lambda b, h, c: (b, h, cidx(c), 0, 0)
    st_map = lambda b, h, c: (b, h, 0, 0)
    return pl.pallas_call(
        functools.partial(_mlstm_kernel, rev=rev, nc=nc),
        grid=(bn, nh, nc),
        in_specs=[pl.BlockSpec((None, None, lc, DK_A), seq_map),
                  pl.BlockSpec((None, None, lc, DK_A), seq_map),
                  pl.BlockSpec((None, None, lc, DV_A), seq_map),
                  pl.BlockSpec((None, None, None, 2, lc), g_map),
                  pl.BlockSpec((None, None, None, lc, 2), g_map),
                  pl.BlockSpec((None, None, DK_A, 2 * DV_A), st_map)],
        out_specs=[pl.BlockSpec((None, None, lc, DV_A), seq_map),
                   pl.BlockSpec((None, None, DK_A, 2 * DV_A), st_map)],
        out_shape=[jax.ShapeDtypeStruct((bn, nh, L, DV_A), F32),
                   jax.ShapeDtypeStruct((bn, nh, DK_A, 2 * DV_A), F32)],
        scratch_shapes=[pltpu.VMEM((DK_A, 2 * DV_A), F32)],
        compiler_params=_cparams(("parallel", "parallel", "arbitrary")),
        name="mlstm_rev" if rev else "mlstm_fwd",
    )(q, k, v, grow, gcol, c0)


def _natten_kernel(q_ref, k_ref, v_ref, kc_ref, vc_ref, bias_ref, o_ref, *, seq_len):
    i = pl.program_id(2)
    nq = QROWS * GRID_W
    nk = KROWS * GRID_W
    start = jnp.clip(i * nq - (WIN_R // 2) * GRID_W, 0, seq_len - nk)
    start = pl.multiple_of(start, GRID_W)
    q = q_ref[...]
    kb = k_ref[pl.ds(start, nk), :]
    vb = v_ref[pl.ds(start, nk), :]
    nt = (((1,), (1,)), ((), ()))
    s_win = lax.dot_general(q, kb, nt, preferred_element_type=F32) + bias_ref[...]
    s_ctx = lax.dot_general(q, kc_ref[...], nt, preferred_element_type=F32)
    m = jnp.maximum(jnp.max(s_win, axis=-1, keepdims=True), jnp.max(s_ctx, axis=-1, keepdims=True))
    p_win = jnp.exp(s_win - m)
    p_ctx = jnp.exp(s_ctx - m)
    l = jnp.sum(p_win, axis=-1, keepdims=True) + jnp.sum(p_ctx, axis=-1, keepdims=True)
    o = (jnp.dot(p_win.astype(BF16), vb, preferred_element_type=F32)
         + jnp.dot(p_ctx.astype(BF16), vc_ref[...], preferred_element_type=F32))
    o_ref[...] = (o / l).astype(o_ref.dtype)


def _natten_bias(rpb, rows):
    nh = rpb.shape[0]
    nblk = rows // QROWS
    pats = []
    for blk in (0, 1 if nblk > 2 else 0, nblk - 1):
        r0 = blk * QROWS
        k0 = int(np.clip(r0 - WIN_R // 2, 0, rows - KROWS))
        qr = r0 + np.arange(QROWS)
        qc = np.arange(GRID_W)
        kr = k0 + np.arange(KROWS)
        kc = np.arange(GRID_W)
        wr0 = np.clip(qr - WIN_R // 2, 0, rows - WIN_R)
        wc0 = np.clip(qc - WIN_C // 2, 0, GRID_W - WIN_C)
        in_r = (kr[None, :] >= wr0[:, None]) & (kr[None, :] < wr0[:, None] + WIN_R)
        in_c = (kc[None, :] >= wc0[:, None]) & (kc[None, :] < wc0[:, None] + WIN_C)
        dr = np.clip(kr[None, :] - qr[:, None] + WIN_R - 1, 0, 2 * WIN_R - 2)
        dc = np.clip(kc[None, :] - qc[:, None] + WIN_C - 1, 0, 2 * WIN_C - 2)
        mask = in_r[:, None, :, None] & in_c[None, :, None, :]
        b = rpb[:, dr[:, None, :, None], dc[None, :, None, :]].astype(F32)
        b = jnp.where(mask[None], b, -jnp.inf)
        pats.append(b.reshape(nh, QROWS * GRID_W, KROWS * GRID_W))
    return jnp.stack(pats)


def _natten(q, k, v, kc, vc, bias):
    bn, nh, S, d = q.shape
    nq = QROWS * GRID_W
    nk = KROWS * GRID_W
    nblk = S // nq
    lctx = kc.shape[2]

    def bias_map(b, h, i):
        pat = jnp.where(i == 0, 0, jnp.where(i == nblk - 1, 2, 1))
        return (pat, h, 0, 0)

    return pl.pallas_call(
        functools.partial(_natten_kernel, seq_len=S),
        grid=(bn, nh, nblk),
        in_specs=[pl.BlockSpec((None, None, nq, d), lambda b, h, i: (b, h, i, 0)),
                  pl.BlockSpec((None, None, S, d), lambda b, h, i: (b, h, 0, 0)),
                  pl.BlockSpec((None, None, S, d), lambda b, h, i: (b, h, 0, 0)),
                  pl.BlockSpec((None, None, lctx, d), lambda b, h, i: (b, h, 0, 0)),
                  pl.BlockSpec((None, None, lctx, d), lambda b, h, i: (b, h, 0, 0)),
                  pl.BlockSpec((None, None, nq, nk), bias_map)],
        out_specs=pl.BlockSpec((None, None, nq, d), lambda b, h, i: (b, h, i, 0)),
        out_shape=jax.ShapeDtypeStruct((bn, nh, S, d), BF16),
        compiler_params=_cparams(("parallel", "parallel", "arbitrary")),
        name="natten",
    )(q, k, v, kc, vc, bias)


def _merge_kernel(hf_ref, hb_ref, oa_ref, na_ref, mg_ref, x_ref, mod_ref, gn_ref,
                  wa_ref, wb_ref, wo_ref, o_ref):
    h = hf_ref[...] + hb_ref[...]
    parts = []
    for hd in range(NH_A):
        hh = h[:, hd * DV_A:(hd + 1) * DV_A]
        ms = jnp.mean(hh * hh, axis=-1, keepdims=True)
        parts.append(hh * lax.rsqrt(ms + EPS))
    ha = jnp.concatenate(parts, axis=1) * gn_ref[...]
    ha = (ha * jax.nn.sigmoid(oa_ref[...].astype(F32))).astype(BF16)
    d = wa_ref.shape[1]
    gates = jax.nn.sigmoid(mg_ref[...].astype(F32))
    t = (gates[:, :d] * jnp.dot(ha, wa_ref[...], preferred_element_type=F32)
         + gates[:, d:] * jnp.dot(na_ref[...], wb_ref[...], preferred_element_type=F32))
    mix = jnp.dot(t.astype(BF16), wo_ref[...], preferred_element_type=F32)
    g1 = mod_ref[2:3, :]
    o_ref[...] = x_ref[...] + g1 * mix


def _merge(hf, hb, proj, na, x, mod3, gn, wa, wb, wo, tm=256):
    bn, S, d = x.shape
    tm = min(tm, S)
    nblk = S // tm
    d_a = hf.shape[-1]
    d_b = na.shape[-1]
    oa_blk = BLK_OA * PROJ_TN // d_a
    mg_blk = BLK_MG * PROJ_TN // (2 * d)
    tok = lambda b, i: (b * nblk + i, 0)
    const = lambda b, i: (0, 0)
    return pl.pallas_call(
        _merge_kernel,
        grid=(bn, nblk),
        in_specs=[pl.BlockSpec((None, tm, d_a), lambda b, i: (b, i, 0)),
                  pl.BlockSpec((None, tm, d_a), lambda b, i: (b, i, 0)),
                  pl.BlockSpec((tm, d_a), lambda b, i: (b * nblk + i, oa_blk)),
                  pl.BlockSpec((None, tm, d_b), lambda b, i: (b, i, 0)),
                  pl.BlockSpec((tm, 2 * d), lambda b, i: (b * nblk + i, mg_blk)),
                  pl.BlockSpec((None, tm, d), lambda b, i: (b, i, 0)),
                  pl.BlockSpec((None, N_MOD, d), lambda b, i: (b, 0, 0)),
                  pl.BlockSpec((1, d_a), const),
                  pl.BlockSpec((d_a, d), const),
                  pl.BlockSpec((d_b, d), const),
                  pl.BlockSpec((d, d), const)],
        out_specs=pl.BlockSpec((None, tm, d), lambda b, i: (b, i, 0)),
        out_shape=jax.ShapeDtypeStruct((bn, S, d), F32),
        compiler_params=_cparams(("parallel", "parallel")),
        name="merge",
    )(hf, hb, proj, na, proj, x, mod3, gn, wa, wb, wo)


def _ffn_kernel(x_ref, mod_ref, g_ref, wg_ref, wu_ref, wd_ref, o_ref):
    x = x_ref[...]
    ms = jnp.mean(x * x, axis=-1, keepdims=True)
    y = x * lax.rsqrt(ms + EPS) * g_ref[...]
    hx = (y * (1.0 + mod_ref[4:5, :]) + mod_ref[3:4, :]).astype(BF16)
    a = jnp.dot(hx, wg_ref[...], preferred_element_type=F32)
    u = jnp.dot(hx, wu_ref[...], preferred_element_type=F32)
    act = (a * jax.nn.sigmoid(a) * u).astype(BF16)
    f = jnp.dot(act, wd_ref[...], preferred_element_type=F32)
    o_ref[...] = x + mod_ref[5:6, :] * f


def _ffn(x, mod3, g, wg, wu, wd, tm=256):
    bn, S, d = x.shape
    tm = min(tm, S)
    dff = wg.shape[1]
    const = lambda b, i: (0, 0)
    return pl.pallas_call(
        _ffn_kernel,
        grid=(bn, S // tm),
        in_specs=[pl.BlockSpec((None, tm, d), lambda b, i: (b, i, 0)),
                  pl.BlockSpec((None, N_MOD, d), lambda b, i: (b, 0, 0)),
                  pl.BlockSpec((1, d), const),
                  pl.BlockSpec((d, dff), const),
                  pl.BlockSpec((d, dff), const),
                  pl.BlockSpec((dff, d), const)],
        out_specs=pl.BlockSpec((None, tm, d), lambda b, i: (b, i, 0)),
        out_shape=jax.ShapeDtypeStruct((bn, S, d), F32),
        compiler_params=_cparams(("parallel", "parallel")),
        name="ffn",
    )(x, mod3, g.reshape(1, d), wg, wu, wd)


def _rope_tables(S):
    t = np.arange(S)
    row, col = t // GRID_W, t % GRID_W
    half = DK_A // 4
    inv = ROPE_THETA ** (-np.arange(half, dtype=np.float64) / half)
    ang_r = row[:, None] * inv[None, :]
    ang_c = col[:, None] * inv[None, :]
    cos = np.concatenate([np.cos(ang_r)] * 2 + [np.cos(ang_c)] * 2, axis=1)
    sin = np.concatenate([-np.sin(ang_r), np.sin(ang_r), -np.sin(ang_c), np.sin(ang_c)], axis=1)
    reps = LANES // DK_A
    return (jnp.asarray(np.tile(cos, (1, reps)), F32), jnp.asarray(np.tile(sin, (1, reps)), F32))


def _split_heads(t, nh):
    bn, L, _ = t.shape
    return t.reshape(bn, L, nh, -1).transpose(0, 2, 1, 3)


def _gate_layouts(g, bn, L):
    lc = min(CHUNK, L)
    nc = L // lc
    g = g[:, :4 * NH_A].reshape(bn, nc, lc, 4, NH_A)
    out = []
    for d in range(2):
        pair = g[:, :, :, 2 * d:2 * d + 2, :]
        out.append((pair.transpose(0, 4, 1, 3, 2), pair.transpose(0, 4, 1, 2, 3)))
    return out


def kernel(x, c, ctx, c_ctx, w_mod, b_mod, norm1_g, w_in, b_gates, mlstm_norm_g, qn_g, kn_g, rpb,
           w_branch_a, w_branch_b, w_out, norm2_g, w_ffn_gate, w_ffn_up, w_ffn_down):
    bn, S, d = x.shape
    lctx = ctx.shape[1]
    rows = S // GRID_W
    depth = w_mod.shape[0]
    d_a, d_b = NH_A * DV_A, NH_B * DH_B
    dqk = NH_A * DK_A
    sizes = (dqk, dqk, d_a, d_a, 4 * NH_A, d_b, d_b, d_b, 2 * d)
    offs = np.cumsum((0,) + sizes)
    cos_t, sin_t = _rope_tables(S)
    eye = np.kron(np.eye(PROJ_TN // DH_B), np.ones((DH_B, DH_B)))
    gsum = jnp.asarray(eye, BF16)

    for l in range(depth):
        assert l == depth - 1, "context-stream update for non-final layers is not implemented"
        n_rows = -(-(bn + 1) // 8) * 8
        cc = jnp.concatenate([c, c_ctx[None, :], jnp.zeros((n_rows - bn - 1, d), F32)], axis=0)
        mod3 = _modulation(cc, w_mod[l], b_mod[l]).reshape(n_rows, N_MOD, d)

        w = w_in[l]
        seg = lambda i: w[:, int(offs[i]):int(offs[i + 1])]
        w_main = jnp.concatenate([seg(0), seg(1), seg(2), seg(3), seg(5), seg(6), seg(7), seg(8)],
                                 axis=1).astype(BF16)
        w_g = jnp.pad(seg(4), ((0, 0), (0, LANES - 4 * NH_A))).astype(BF16)
        b_g = jnp.pad(b_gates[l].reshape(1, 4 * NH_A), ((0, 0), (0, LANES - 4 * NH_A)))
        aux = np.ones((N_PROJ_BLK, 1, PROJ_TN), np.float32)
        aux = jnp.asarray(aux)
        qscale = jnp.full((PROJ_TN,), DK_A ** -0.5, F32)
        qn = jnp.tile(qn_g[l], PROJ_TN // DH_B) * DH_B ** -0.5
        kn = jnp.tile(kn_g[l], PROJ_TN // DH_B)
        aux = aux.at[BLK_QA, 0].set(qscale)
        aux = aux.at[BLK_QB, 0].set(qn).at[BLK_QB + 1, 0].set(qn)
        aux = aux.at[BLK_KB, 0].set(kn).at[BLK_KB + 1, 0].set(kn)

        hx = _prenorm(x, mod3, norm1_g[l], sh_idx=0, sc_idx=1)
        hc = _prenorm(ctx, mod3, norm1_g[l], sh_idx=0, sc_idx=1, ctx_row=bn)
        px = _projection(hx.reshape(bn * S, d), w_main, aux, cos_t, sin_t, gsum, rope=True, seq_len=S)
        pc = _projection(hc.reshape(bn * lctx, d), w_main, aux, cos_t, sin_t, gsum, rope=False, seq_len=lctx)
        gx = _gate_projection(hx.reshape(bn * S, d), w_g, b_g)
        gc = _gate_projection(hc.reshape(bn * lctx, d), w_g, b_g)

        px3 = px.reshape(bn, S, -1)
        pc3 = pc.reshape(bn, lctx, -1)
        col = lambda t, blk, width: t[:, :, blk * PROJ_TN: blk * PROJ_TN + width]

        zero_state = jnp.zeros((bn, NH_A, DK_A, 2 * DV_A), F32)
        qa_c, ka_c, va_c = (_split_heads(col(pc3, b_, w_), NH_A) for b_, w_ in
                            ((BLK_QA, dqk), (BLK_KA, dqk), (BLK_VA, d_a)))
        qa_x, ka_x, va_x = (_split_heads(col(px3, b_, w_), NH_A) for b_, w_ in
                            ((BLK_QA, dqk), (BLK_KA, dqk), (BLK_VA, d_a)))
        gl_c = _gate_layouts(gc, bn, lctx)
        gl_x = _gate_layouts(gx, bn, S)
        h_dirs = []
        for dirn in range(2):
            rev = dirn == 1
            _, st = _mlstm_scan(qa_c, ka_c, va_c, gl_c[dirn][0], gl_c[dirn][1], zero_state, rev=rev)
            h, _ = _mlstm_scan(qa_x, ka_x, va_x, gl_x[dirn][0], gl_x[dirn][1], st, rev=rev)
            h_dirs.append(h.transpose(0, 2, 1, 3).reshape(bn, S, d_a))

        qb = _split_heads(col(px3, BLK_QB, d_b), NH_B)
        kb = _split_heads(col(px3, BLK_KB, d_b), NH_B)
        vb = _split_heads(col(px3, BLK_VB, d_b), NH_B)
        kcb = _split_heads(col(pc3, BLK_KB, d_b), NH_B)
        vcb = _split_heads(col(pc3, BLK_VB, d_b), NH_B)
        bias = _natten_bias(rpb[l], rows)
        na = _natten(qb, kb, vb, kcb, vcb, bias).transpose(0, 2, 1, 3).reshape(bn, S, d_b)

        x_mid = _merge(h_dirs[0], h_dirs[1], px, na, x, mod3, mlstm_norm_g[l].reshape(1, d_a),
                       w_branch_a[l].astype(BF16), w_branch_b[l].astype(BF16), w_out[l].astype(BF16))
        x = _ffn(x_mid, mod3, norm2_g[l], w_ffn_gate[l].astype(BF16), w_ffn_up[l].astype(BF16),
                 w_ffn_down[l].astype(BF16))
    return x
```

```python
import functools

import numpy as np
import jax
import jax.numpy as jnp
from jax import lax
from jax.experimental import pallas as pl
from jax.experimental.pallas import tpu as pltpu

F32 = jnp.float32
BF16 = jnp.bfloat16

GRID_W = 64
NH_A, DK_A, DV_A = 8, 64, 128
NH_B, DH_B = 16, 64
CHUNK = 128
GATE_CAP = 15.0
WIN_R, WIN_C = 8, 16
ROPE_THETA = 10000.0
EPS = 1e-6
N_MOD = 6
LANES = 128
VMEM_LIMIT = 56 * 1024 * 1024

QROWS = 4
KROWS = QROWS + WIN_R - 1

N_PAIR = NH_A // 2
C_EXT = 2 * DV_A

G_A, G_WL, G_B, G_TOT = 0, 32, 72, 104


def _cparams(sem):
    return pltpu.CompilerParams(dimension_semantics=sem, vmem_limit_bytes=VMEM_LIMIT)


def _mod_kernel(c_ref, w_ref, b_ref, o_ref):
    c = c_ref[...]
    h = c * jax.nn.sigmoid(c)
    o_ref[...] = jnp.dot(h, w_ref[...], preferred_element_type=F32,
                         precision=lax.Precision.HIGHEST) + b_ref[...]


def _modulation(cc, w_mod, b_mod):
    rows, d = cc.shape
    n = w_mod.shape[1]
    tn = 1024
    return pl.pallas_call(
        _mod_kernel,
        grid=(n // tn,),
        in_specs=[pl.BlockSpec((rows, d), lambda j: (0, 0)),
                  pl.BlockSpec((d, tn), lambda j: (0, j)),
                  pl.BlockSpec((1, tn), lambda j: (0, j))],
        out_specs=pl.BlockSpec((rows, tn), lambda j: (0, j)),
        out_shape=jax.ShapeDtypeStruct((rows, n), F32),
        compiler_params=_cparams(("arbitrary",)),
        name="modulation",
    )(cc, w_mod, b_mod.reshape(1, n))


def _prenorm_kernel(x_ref, mod_ref, g_ref, o_ref, *, sh_idx, sc_idx):
    x = x_ref[...]
    ms = jnp.mean(x * x, axis=-1, keepdims=True)
    y = x * lax.rsqrt(ms + EPS) * g_ref[...]
    sc = mod_ref[sc_idx:sc_idx + 1, :]
    sh = mod_ref[sh_idx:sh_idx + 1, :]
    o_ref[...] = (y * (1.0 + sc) + sh).astype(o_ref.dtype)


def _prenorm(x, mod3, g, *, sh_idx, sc_idx, ctx_row=None, tm=512):
    bn, L, d = x.shape
    tm = min(tm, L)
    if ctx_row is None:
        mod_map = lambda b, i: (b, 0, 0)
    else:
        mod_map = lambda b, i: (ctx_row, 0, 0)
    return pl.pallas_call(
        functools.partial(_prenorm_kernel, sh_idx=sh_idx, sc_idx=sc_idx),
        grid=(bn, L // tm),
        in_specs=[pl.BlockSpec((None, tm, d), lambda b, i: (b, i, 0)),
                  pl.BlockSpec((None, N_MOD, d), mod_map),
                  pl.BlockSpec((1, d), lambda b, i: (0, 0))],
        out_specs=pl.BlockSpec((None, tm, d), lambda b, i: (b, i, 0)),
        out_shape=jax.ShapeDtypeStruct((bn, L, d), BF16),
        compiler_params=_cparams(("parallel", "parallel")),
        name="prenorm",
    )(x, mod3, g.reshape(1, d))


PROJ_TN = 512
BLK_QA, BLK_KA, BLK_VA, BLK_OA, BLK_QB, BLK_KB, BLK_VB, BLK_MG = 0, 1, 2, 4, 6, 8, 10, 12
N_PROJ_BLK = 16


def _swap16(x):
    n = x.shape[-1]
    lane = lax.broadcasted_iota(jnp.int32, x.shape, x.ndim - 1)
    first = (lane & 31) < 16
    return jnp.where(first, pltpu.roll(x, n - 16, x.ndim - 1), pltpu.roll(x, 16, x.ndim - 1))


def _proj_kernel(a_ref, w_ref, aux_ref, cos_ref, sin_ref, gsum_ref, o_ref, *, rope):
    j = pl.program_id(1)
    acc = jnp.dot(a_ref[...], w_ref[...], preferred_element_type=F32)
    scale = aux_ref[...]
    is_rope = j < BLK_VA
    is_norm = jnp.logical_and(j >= BLK_QB, j < BLK_VB)

    if rope:
        @pl.when(is_rope)
        def _():
            reps = PROJ_TN // LANES
            cos = jnp.concatenate([cos_ref[...]] * reps, axis=1)
            sin = jnp.concatenate([sin_ref[...]] * reps, axis=1)
            y = acc * cos + _swap16(acc) * sin
            o_ref[...] = (y * scale).astype(o_ref.dtype)
        plain = jnp.logical_not(jnp.logical_or(is_rope, is_norm))
    else:
        plain = jnp.logical_not(is_norm)

    @pl.when(is_norm)
    def _():
        sq = (acc * acc).astype(BF16)
        ss = jnp.dot(sq, gsum_ref[...], preferred_element_type=F32)
        y = acc * lax.rsqrt(ss * (1.0 / DH_B) + EPS)
        o_ref[...] = (y * scale).astype(o_ref.dtype)

    @pl.when(plain)
    def _():
        o_ref[...] = (acc * scale).astype(o_ref.dtype)


def _projection(a, w, aux, cos, sin, gsum, *, rope, seq_len, tm=1024):
    m, d = a.shape
    tm = min(tm, seq_len)
    n = w.shape[1]
    nb = n // PROJ_TN
    pos_blocks = seq_len // tm
    return pl.pallas_call(
        functools.partial(_proj_kernel, rope=rope),
        grid=(m // tm, nb),
        in_specs=[pl.BlockSpec((tm, d), lambda i, j: (i, 0)),
                  pl.BlockSpec((d, PROJ_TN), lambda i, j: (0, j)),
                  pl.BlockSpec((None, 1, PROJ_TN), lambda i, j: (j, 0, 0)),
                  pl.BlockSpec((tm, LANES), lambda i, j: (i % pos_blocks, 0)),
                  pl.BlockSpec((tm, LANES), lambda i, j: (i % pos_blocks, 0)),
                  pl.BlockSpec((PROJ_TN, PROJ_TN), lambda i, j: (0, 0))],
        out_specs=pl.BlockSpec((tm, PROJ_TN), lambda i, j: (i, j)),
        out_shape=jax.ShapeDtypeStruct((m, n), BF16),
        compiler_params=_cparams(("parallel", "arbitrary")),
        name="projection",
    )(a, w, aux, cos, sin, gsum)


def _gate_kernel(a_ref, w_ref, b_ref, gc_ref, gr_ref):
    tm = a_ref.shape[0]
    g = jnp.dot(a_ref[...], w_ref[...], preferred_element_type=F32) + b_ref[...]
    g = GATE_CAP * jnp.tanh(g * (1.0 / GATE_CAP))
    lane_t = lax.broadcasted_iota(jnp.int32, g.shape, 1)
    is_forget = (lane_t & NH_A) != 0
    x = jnp.where(is_forget, jax.nn.log_sigmoid(g), g)
    x = jnp.where(lane_t < 4 * NH_A, x, 0.0)

    t_idx = lax.broadcasted_iota(jnp.int32, (CHUNK, CHUNK), 0)
    s_idx = lax.broadcasted_iota(jnp.int32, (CHUNK, CHUNK), 1)
    tri_f = (s_idx <= t_idx).astype(F32)
    tri_b = (s_idx >= t_idx).astype(F32)
    lane = s_idx
    fwd_half = lane < 2 * NH_A
    low = lane < 4 * NH_A
    hi = lax.Precision.HIGHEST
    for ci in range(tm // CHUNK):
        xc = x[ci * CHUNK:(ci + 1) * CHUNK]
        cf = jnp.dot(tri_f, xc, preferred_element_type=F32, precision=hi)
        cb = jnp.dot(tri_b, xc, preferred_element_type=F32, precision=hi)
        tot = jnp.broadcast_to(jnp.sum(xc, axis=0, keepdims=True), xc.shape)
        r_b = jnp.where(fwd_half, pltpu.roll(cf, LANES - NH_A, 1), pltpu.roll(cb, LANES - NH_A, 1))
        a = xc - r_b
        wl = xc + (pltpu.roll(tot, LANES - NH_A, 1) - r_b)
        bsel = jnp.where(fwd_half, cf, cb)
        z = lambda v: jnp.where(low, v, 0.0)
        packed = z(a) + pltpu.roll(z(wl), 32, 1) + pltpu.roll(z(bsel), 64, 1) + pltpu.roll(z(tot), 96, 1)
        gc_ref[ci * CHUNK:(ci + 1) * CHUNK, :] = packed
        gr_ref[:, ci * CHUNK:(ci + 1) * CHUNK] = packed.T


def _gate_projection(a, w_g, b_g, tm=512):
    m, d = a.shape
    tm = min(tm, m)
    return pl.pallas_call(
        _gate_kernel,
        grid=(m // tm,),
        in_specs=[pl.BlockSpec((tm, d), lambda i: (i, 0)),
                  pl.BlockSpec((d, LANES), lambda i: (0, 0)),
                  pl.BlockSpec((1, LANES), lambda i: (0, 0))],
        out_specs=[pl.BlockSpec((tm, LANES), lambda i: (i, 0)),
                   pl.BlockSpec((LANES, tm), lambda i: (0, i))],
        out_shape=[jax.ShapeDtypeStruct((m, LANES), F32),
                   jax.ShapeDtypeStruct((LANES, m), F32)],
        compiler_params=_cparams(("parallel",)),
        name="gate_projection",
    )(a, w_g, b_g)


def _ones_block(rows):
    return (lax.broadcasted_iota(jnp.int32, (rows, DV_A), 1) == 0).astype(BF16)


def _mlstm_state_kernel(kf_ref, vf_ref, gf_ref, kb_ref, vb_ref, gb_ref, c0_ref,
                        cf_ref, cb_ref, cfin_ref, st, *, nc):
    s = pl.program_id(1)

    @pl.when(s == 0)
    def _():
        st[...] = c0_ref[...]

    lc = kf_ref.shape[0]
    lo = lax.broadcasted_iota(jnp.int32, (lc, 2 * DK_A), 1) < DK_A
    row = lax.broadcasted_iota(jnp.int32, (2 * DK_A, 2 * C_EXT), 0)
    col = lax.broadcasted_iota(jnp.int32, (2 * DK_A, 2 * C_EXT), 1)
    top = row < DK_A
    diag = top == (col < C_EXT)
    ones_blk = _ones_block(lc)
    for d, (k_ref, v_ref, g_ref, out_ref) in enumerate(((kf_ref, vf_ref, gf_ref, cf_ref),
                                                        (kb_ref, vb_ref, gb_ref, cb_ref))):
        g = g_ref[...]
        for p in range(N_PAIR):
            h0, h1 = 2 * p, 2 * p + 1
            wl = G_WL + 16 * d
            w = jnp.where(lo, jnp.exp(g[:, wl + h0:wl + h0 + 1]), jnp.exp(g[:, wl + h1:wl + h1 + 1]))
            kw = (k_ref[:, p * 2 * DK_A:(p + 1) * 2 * DK_A].astype(F32) * w).astype(BF16)
            vext = jnp.concatenate([v_ref[:, h0 * DV_A:(h0 + 1) * DV_A], ones_blk,
                                    v_ref[:, h1 * DV_A:(h1 + 1) * DV_A], ones_blk], axis=1)
            upd = lax.dot_general(kw, vext, (((0,), (0,)), ((), ())), preferred_element_type=F32)
            c_old = st[d, p]
            out_ref[p, 0] = c_old[:DK_A, :C_EXT].astype(out_ref.dtype)
            out_ref[p, 1] = c_old[DK_A:, C_EXT:].astype(out_ref.dtype)
            tl = G_TOT + 16 * d
            dec = jnp.where(top, jnp.exp(g[0:1, tl + h0:tl + h0 + 1]), jnp.exp(g[0:1, tl + h1:tl + h1 + 1]))
            st[d, p] = dec * c_old + jnp.where(diag, upd, 0.0)

    @pl.when(s == nc - 1)
    def _():
        cfin_ref[...] = st[...]


def _mlstm_states(proj, gc, c0, *, bn, seq_len):
    lc = min(CHUNK, seq_len)
    nc = seq_len // lc
    kcol = BLK_KA * PROJ_TN // (NH_A * DK_A)
    vcol = BLK_VA * PROJ_TN // (NH_A * DV_A)
    fwd = lambda b, s: b * nc + s
    bwd = lambda b, s: b * nc + nc - 1 - s
    st_shape = (2, N_PAIR, 2 * DK_A, 2 * C_EXT)
    out_blk = (None, None, N_PAIR, 2, DK_A, C_EXT)
    return pl.pallas_call(
        functools.partial(_mlstm_state_kernel, nc=nc),
        grid=(bn, nc),
        in_specs=[pl.BlockSpec((lc, NH_A * DK_A), lambda b, s: (fwd(b, s), kcol)),
                  pl.BlockSpec((lc, NH_A * DV_A), lambda b, s: (fwd(b, s), vcol)),
                  pl.BlockSpec((lc, LANES), lambda b, s: (fwd(b, s), 0)),
                  pl.BlockSpec((lc, NH_A * DK_A), lambda b, s: (bwd(b, s), kcol)),
                  pl.BlockSpec((lc, NH_A * DV_A), lambda b, s: (bwd(b, s), vcol)),
                  pl.BlockSpec((lc, LANES), lambda b, s: (bwd(b, s), 0)),
                  pl.BlockSpec((None,) + st_shape, lambda b, s: (b, 0, 0, 0, 0))],
        out_specs=[pl.BlockSpec(out_blk, lambda b, s: (b, s, 0, 0, 0, 0)),
                   pl.BlockSpec(out_blk, lambda b, s: (b, nc - 1 - s, 0, 0, 0, 0)),
                   pl.BlockSpec((None,) + st_shape, lambda b, s: (b, 0, 0, 0, 0))],
        out_shape=[jax.ShapeDtypeStruct((bn, nc, N_PAIR, 2, DK_A, C_EXT), BF16),
                   jax.ShapeDtypeStruct((bn, nc, N_PAIR, 2, DK_A, C_EXT), BF16),
                   jax.ShapeDtypeStruct((bn,) + st_shape, F32)],
        scratch_shapes=[pltpu.VMEM(st_shape, F32)],
        compiler_params=_cparams(("parallel", "arbitrary")),
        name="mlstm_states",
    )(proj, proj, gc, proj, proj, gc, c0)


def _mlstm_out_kernel(q_ref, k_ref, v_ref, gc_ref, gr_ref, cf_ref, cb_ref, h_ref):
    lc = q_ref.shape[0]
    gc = gc_ref[...]
    gr = gr_ref[...]
    t_idx = lax.broadcasted_iota(jnp.int32, (lc, lc), 0)
    s_idx = lax.broadcasted_iota(jnp.int32, (lc, lc), 1)
    visible = (s_idx <= t_idx, s_idx >= t_idx)
    lo = lax.broadcasted_iota(jnp.int32, (lc, 2 * DK_A), 1) < DK_A
    ones_blk = _ones_block(lc)
    zero_c = jnp.zeros((DK_A, C_EXT), BF16)
    nt = (((1,), (1,)), ((), ()))
    for p in range(N_PAIR):
        qp = q_ref[:, p * 2 * DK_A:(p + 1) * 2 * DK_A]
        kp = k_ref[:, p * 2 * DK_A:(p + 1) * 2 * DK_A]
        zq = jnp.zeros_like(qp)
        q_stack = jnp.concatenate([jnp.where(lo, qp, zq), jnp.where(lo, zq, qp)], axis=0)
        s_both = lax.dot_general(q_stack, kp, nt, preferred_element_type=F32)
        inter = []
        for d, c_ref in enumerate((cf_ref, cb_ref)):
            bl = G_B + 16 * d
            e = jnp.where(lo, jnp.exp(gc[:, bl + 2 * p:bl + 2 * p + 1]),
                          jnp.exp(gc[:, bl + 2 * p + 1:bl + 2 * p + 2]))
            qs = (qp.astype(F32) * e).astype(BF16)
            c_pair = jnp.concatenate([jnp.concatenate([c_ref[p, 0], zero_c], axis=1),
                                      jnp.concatenate([zero_c, c_ref[p, 1]], axis=1)], axis=0)
            inter.append(jnp.dot(qs, c_pair, preferred_element_type=F32))
        for hh in range(2):
            h = 2 * p + hh
            s_h = s_both[hh * lc:(hh + 1) * lc]
            vext = jnp.concatenate([v_ref[:, h * DV_A:(h + 1) * DV_A], ones_blk], axis=1)
            acc = None
            for d in range(2):
                b_col = gc[:, G_B + 16 * d + h:G_B + 16 * d + h + 1]
                a_row = gr[G_A + 16 * d + h:G_A + 16 * d + h + 1, :]
                log_d = jnp.where(visible[d], b_col + a_row, -jnp.inf)
                pm = (s_h * jnp.exp(log_d)).astype(BF16)
                nd = jnp.dot(pm, vext, preferred_element_type=F32) + inter[d][:, hh * C_EXT:(hh + 1) * C_EXT]
                hd = nd[:, :DV_A] / jnp.maximum(jnp.abs(nd[:, DV_A:DV_A + 1]), 1.0)
                acc = hd if acc is None else acc + hd
            h_ref[:, h * DV_A:(h + 1) * DV_A] = acc.astype(h_ref.dtype)


def _mlstm_outputs(proj, gc, gr, cf, cb, *, bn, seq_len):
    lc = min(CHUNK, seq_len)
    nc = seq_len // lc
    qcol = BLK_QA * PROJ_TN // (NH_A * DK_A)
    kcol = BLK_KA * PROJ_TN // (NH_A * DK_A)
    vcol = BLK_VA * PROJ_TN // (NH_A * DV_A)
    tok = lambda b, c: b * nc + c
    st_blk = (None, None, N_PAIR, 2, DK_A, C_EXT)
    st_map = lambda b, c: (b, c, 0, 0, 0, 0)
    return pl.pallas_call(
        _mlstm_out_kernel,
        grid=(bn, nc),
        in_specs=[pl.BlockSpec((lc, NH_A * DK_A), lambda b, c: (tok(b, c), qcol)),
                  pl.BlockSpec((lc, NH_A * DK_A), lambda b, c: (tok(b, c), kcol)),
                  pl.BlockSpec((lc, NH_A * DV_A), lambda b, c: (tok(b, c), vcol)),
                  pl.BlockSpec((lc, LANES), lambda b, c: (tok(b, c), 0)),
                  pl.BlockSpec((LANES, lc), lambda b, c: (0, tok(b, c))),
                  pl.BlockSpec(st_blk, st_map),
                  pl.BlockSpec(st_blk, st_map)],
        out_specs=pl.BlockSpec((lc, NH_A * DV_A), lambda b, c: (tok(b, c), 0)),
        out_shape=jax.ShapeDtypeStruct((bn * seq_len, NH_A * DV_A), F32),
        compiler_params=_cparams(("parallel", "parallel")),
        name="mlstm_outputs",
    )(proj, proj, proj, gc, gr, cf, cb)


def _natten_kernel(q_ref, k_ref, v_ref, kc_ref, vc_ref, bias_ref, o_ref, *, seq_len):
    i = pl.program_id(2)
    nq = QROWS * GRID_W
    nk = KROWS * GRID_W
    start = jnp.clip(i * nq - (WIN_R // 2) * GRID_W, 0, seq_len - nk)
    start = pl.multiple_of(start, GRID_W)
    q = q_ref[...]
    kb = k_ref[pl.ds(start, nk), :]
    vb = v_ref[pl.ds(start, nk), :]
    nt = (((1,), (1,)), ((), ()))
    s_win = lax.dot_general(q, kb, nt, preferred_element_type=F32) + bias_ref[...]
    s_ctx = lax.dot_general(q, kc_ref[...], nt, preferred_element_type=F32)
    m = jnp.maximum(jnp.max(s_win, axis=-1, keepdims=True), jnp.max(s_ctx, axis=-1, keepdims=True))
    p_win = jnp.exp(s_win - m)
    p_ctx = jnp.exp(s_ctx - m)
    l = jnp.sum(p_win, axis=-1, keepdims=True) + jnp.sum(p_ctx, axis=-1, keepdims=True)
    o = (jnp.dot(p_win.astype(BF16), vb, preferred_element_type=F32)
         + jnp.dot(p_ctx.astype(BF16), vc_ref[...], preferred_element_type=F32))
    o_ref[...] = (o / l).astype(o_ref.dtype)


def _natten_bias(rpb, rows):
    nh = rpb.shape[0]
    nblk = rows // QROWS
    pats = []
    for blk in (0, 1 if nblk > 2 else 0, nblk - 1):
        r0 = blk * QROWS
        k0 = int(np.clip(r0 - WIN_R // 2, 0, rows - KROWS))
        qr = r0 + np.arange(QROWS)
        qc = np.arange(GRID_W)
        kr = k0 + np.arange(KROWS)
        kc = np.arange(GRID_W)
        wr0 = np.clip(qr - WIN_R // 2, 0, rows - WIN_R)
        wc0 = np.clip(qc - WIN_C // 2, 0, GRID_W - WIN_C)
        in_r = (kr[None, :] >= wr0[:, None]) & (kr[None, :] < wr0[:, None] + WIN_R)
        in_c = (kc[None, :] >= wc0[:, None]) & (kc[None, :] < wc0[:, None] + WIN_C)
        dr = np.clip(kr[None, :] - qr[:, None] + WIN_R - 1, 0, 2 * WIN_R - 2)
        dc = np.clip(kc[None, :] - qc[:, None] + WIN_C - 1, 0, 2 * WIN_C - 2)
        mask = in_r[:, None, :, None] & in_c[None, :, None, :]
        sel_r = jnp.asarray(dr[:, :, None] == np.arange(2 * WIN_R - 1), F32)
        sel_c = jnp.asarray(dc[:, :, None] == np.arange(2 * WIN_C - 1), F32)
        b = jnp.einsum('xya,hab,uvb->hxuyv', sel_r, rpb.astype(F32), sel_c,
                       precision=lax.Precision.HIGHEST)
        b = jnp.where(mask[None], b, -jnp.inf)
        pats.append(b.reshape(nh, QROWS * GRID_W, KROWS * GRID_W))
    return jnp.stack(pats)


def _natten(q, k, v, kc, vc, bias):
    bn, nh, S, d = q.shape
    nq = QROWS * GRID_W
    nk = KROWS * GRID_W
    nblk = S // nq
    lctx = kc.shape[2]

    def bias_map(b, h, i):
        pat = jnp.where(i == 0, 0, jnp.where(i == nblk - 1, 2, 1))
        return (pat, h, 0, 0)

    return pl.pallas_call(
        functools.partial(_natten_kernel, seq_len=S),
        grid=(bn, nh, nblk),
        in_specs=[pl.BlockSpec((None, None, nq, d), lambda b, h, i: (b, h, i, 0)),
                  pl.BlockSpec((None, None, S, d), lambda b, h, i: (b, h, 0, 0)),
                  pl.BlockSpec((None, None, S, d), lambda b, h, i: (b, h, 0, 0)),
                  pl.BlockSpec((None, None, lctx, d), lambda b, h, i: (b, h, 0, 0)),
                  pl.BlockSpec((None, None, lctx, d), lambda b, h, i: (b, h, 0, 0)),
                  pl.BlockSpec((None, None, nq, nk), bias_map)],
        out_specs=pl.BlockSpec((None, None, nq, d), lambda b, h, i: (b, h, i, 0)),
        out_shape=jax.ShapeDtypeStruct((bn, nh, S, d), BF16),
        compiler_params=_cparams(("parallel", "parallel", "arbitrary")),
        name="natten",
    )(q, k, v, kc, vc, bias)


def _merge_kernel(h_ref, oa_ref, na_ref, mg_ref, x_ref, mod_ref, gn_ref, wa_ref, wb_ref, wo_ref, o_ref):
    h = h_ref[...]
    parts = []
    for hd in range(NH_A):
        hh = h[:, hd * DV_A:(hd + 1) * DV_A]
        ms = jnp.mean(hh * hh, axis=-1, keepdims=True)
        parts.append(hh * lax.rsqrt(ms + EPS))
    ha = jnp.concatenate(parts, axis=1) * gn_ref[...]
    ha = (ha * jax.nn.sigmoid(oa_ref[...].astype(F32))).astype(BF16)
    d = wa_ref.shape[1]
    gates = jax.nn.sigmoid(mg_ref[...].astype(F32))
    t = (gates[:, :d] * jnp.dot(ha, wa_ref[...], preferred_element_type=F32)
         + gates[:, d:] * jnp.dot(na_ref[...], wb_ref[...], preferred_element_type=F32))
    mix = jnp.dot(t.astype(BF16), wo_ref[...], preferred_element_type=F32)
    g1 = mod_ref[2:3, :]
    o_ref[...] = x_ref[...] + g1 * mix


def _merge(h, proj, na, x, mod3, gn, wa, wb, wo, tm=256):
    bn, S, d = x.shape
    tm = min(tm, S)
    nblk = S // tm
    d_a = h.shape[-1]
    d_b = na.shape[-1]
    oa_blk = BLK_OA * PROJ_TN // d_a
    mg_blk = BLK_MG * PROJ_TN // (2 * d)
    const = lambda b, i: (0, 0)
    return pl.pallas_call(
        _merge_kernel,
        grid=(bn, nblk),
        in_specs=[pl.BlockSpec((tm, d_a), lambda b, i: (b * nblk + i, 0)),
                  pl.BlockSpec((tm, d_a), lambda b, i: (b * nblk + i, oa_blk)),
                  pl.BlockSpec((None, tm, d_b), lambda b, i: (b, i, 0)),
                  pl.BlockSpec((tm, 2 * d), lambda b, i: (b * nblk + i, mg_blk)),
                  pl.BlockSpec((None, tm, d), lambda b, i: (b, i, 0)),
                  pl.BlockSpec((None, N_MOD, d), lambda b, i: (b, 0, 0)),
                  pl.BlockSpec((1, d_a), const),
                  pl.BlockSpec((d_a, d), const),
                  pl.BlockSpec((d_b, d), const),
                  pl.BlockSpec((d, d), const)],
        out_specs=pl.BlockSpec((None, tm, d), lambda b, i: (b, i, 0)),
        out_shape=jax.ShapeDtypeStruct((bn, S, d), F32),
        compiler_params=_cparams(("parallel", "parallel")),
        name="merge",
    )(h, proj, na, proj, x, mod3, gn, wa, wb, wo)


def _ffn_kernel(x_ref, mod_ref, g_ref, wg_ref, wu_ref, wd_ref, o_ref):
    x = x_ref[...]
    ms = jnp.mean(x * x, axis=-1, keepdims=True)
    y = x * lax.rsqrt(ms + EPS) * g_ref[...]
    hx = (y * (1.0 + mod_ref[4:5, :]) + mod_ref[3:4, :]).astype(BF16)
    a = jnp.dot(hx, wg_ref[...], preferred_element_type=F32)
    u = jnp.dot(hx, wu_ref[...], preferred_element_type=F32)
    act = (a * jax.nn.sigmoid(a) * u).astype(BF16)
    f = jnp.dot(act, wd_ref[...], preferred_element_type=F32)
    o_ref[...] = x + mod_ref[5:6, :] * f


def _ffn(x, mod3, g, wg, wu, wd, tm=256):
    bn, S, d = x.shape
    tm = min(tm, S)
    dff = wg.shape[1]
    const = lambda b, i: (0, 0)
    return pl.pallas_call(
        _ffn_kernel,
        grid=(bn, S // tm),
        in_specs=[pl.BlockSpec((None, tm, d), lambda b, i: (b, i, 0)),
                  pl.BlockSpec((None, N_MOD, d), lambda b, i: (b, 0, 0)),
                  pl.BlockSpec((1, d), const),
                  pl.BlockSpec((d, dff), const),
                  pl.BlockSpec((d, dff), const),
                  pl.BlockSpec((dff, d), const)],
        out_specs=pl.BlockSpec((None, tm, d), lambda b, i: (b, i, 0)),
        out_shape=jax.ShapeDtypeStruct((bn, S, d), F32),
        compiler_params=_cparams(("parallel", "parallel")),
        name="ffn",
    )(x, mod3, g.reshape(1, d), wg, wu, wd)


def _rope_tables(S):
    t = np.arange(S)
    row, col = t // GRID_W, t % GRID_W
    half = DK_A // 4
    inv = ROPE_THETA ** (-np.arange(half, dtype=np.float64) / half)
    ang_r = row[:, None] * inv[None, :]
    ang_c = col[:, None] * inv[None, :]
    cos = np.concatenate([np.cos(ang_r)] * 2 + [np.cos(ang_c)] * 2, axis=1)
    sin = np.concatenate([-np.sin(ang_r), np.sin(ang_r), -np.sin(ang_c), np.sin(ang_c)], axis=1)
    reps = LANES // DK_A
    return (jnp.asarray(np.tile(cos, (1, reps)), F32), jnp.asarray(np.tile(sin, (1, reps)), F32))


def _split_heads(t, nh):
    bn, L, _ = t.shape
    return t.reshape(bn, L, nh, -1).transpose(0, 2, 1, 3)


def kernel(x, c, ctx, c_ctx, w_mod, b_mod, norm1_g, w_in, b_gates, mlstm_norm_g, qn_g, kn_g, rpb,
           w_branch_a, w_branch_b, w_out, norm2_g, w_ffn_gate, w_ffn_up, w_ffn_down):
    bn, S, d = x.shape
    lctx = ctx.shape[1]
    rows = S // GRID_W
    depth = w_mod.shape[0]
    d_a, d_b = NH_A * DV_A, NH_B * DH_B
    dqk = NH_A * DK_A
    sizes = (dqk, dqk, d_a, d_a, 4 * NH_A, d_b, d_b, d_b, 2 * d)
    offs = np.cumsum((0,) + sizes)
    cos_t, sin_t = _rope_tables(S)
    eye = np.kron(np.eye(PROJ_TN // DH_B), np.ones((DH_B, DH_B)))
    gsum = jnp.asarray(eye, BF16)

    for l in range(depth):
        assert l == depth - 1, "context-stream update for non-final layers is not implemented"
        n_rows = -(-(bn + 1) // 8) * 8
        cc = jnp.concatenate([c, c_ctx[None, :], jnp.zeros((n_rows - bn - 1, d), F32)], axis=0)
        mod3 = _modulation(cc, w_mod[l], b_mod[l]).reshape(n_rows, N_MOD, d)

        w = w_in[l]
        seg = lambda i: w[:, int(offs[i]):int(offs[i + 1])]
        w_main = jnp.concatenate([seg(0), seg(1), seg(2), seg(3), seg(5), seg(6), seg(7), seg(8)],
                                 axis=1).astype(BF16)
        w_g = jnp.pad(seg(4), ((0, 0), (0, LANES - 4 * NH_A))).astype(BF16)
        b_g = jnp.pad(b_gates[l].reshape(1, 4 * NH_A), ((0, 0), (0, LANES - 4 * NH_A)))
        aux = np.ones((N_PROJ_BLK, 1, PROJ_TN), np.float32)
        aux = jnp.asarray(aux)
        qscale = jnp.full((PROJ_TN,), DK_A ** -0.5, F32)
        qn = jnp.tile(qn_g[l], PROJ_TN // DH_B) * DH_B ** -0.5
        kn = jnp.tile(kn_g[l], PROJ_TN // DH_B)
        aux = aux.at[BLK_QA, 0].set(qscale)
        aux = aux.at[BLK_QB, 0].set(qn).at[BLK_QB + 1, 0].set(qn)
        aux = aux.at[BLK_KB, 0].set(kn).at[BLK_KB + 1, 0].set(kn)

        hx = _prenorm(x, mod3, norm1_g[l], sh_idx=0, sc_idx=1)
        hc = _prenorm(ctx, mod3, norm1_g[l], sh_idx=0, sc_idx=1, ctx_row=bn)
        px = _projection(hx.reshape(bn * S, d), w_main, aux, cos_t, sin_t, gsum, rope=True, seq_len=S)
        pc = _projection(hc.reshape(bn * lctx, d), w_main, aux, cos_t, sin_t, gsum, rope=False, seq_len=lctx)
        gcx, grx = _gate_projection(hx.reshape(bn * S, d), w_g, b_g)
        gcc, _ = _gate_projection(hc.reshape(bn * lctx, d), w_g, b_g)

        zero_state = jnp.zeros((bn, 2, N_PAIR, 2 * DK_A, 2 * C_EXT), F32)
        _, _, st_ctx = _mlstm_states(pc, gcc, zero_state, bn=bn, seq_len=lctx)
        cf, cb, _ = _mlstm_states(px, gcx, st_ctx, bn=bn, seq_len=S)
        h_a = _mlstm_outputs(px, gcx, grx, cf, cb, bn=bn, seq_len=S)

        px3 = px.reshape(bn, S, -1)
        pc3 = pc.reshape(bn, lctx, -1)
        col = lambda t, blk, width: t[:, :, blk * PROJ_TN: blk * PROJ_TN + width]
        qb = _split_heads(col(px3, BLK_QB, d_b), NH_B)
        kb = _split_heads(col(px3, BLK_KB, d_b), NH_B)
        vb = _split_heads(col(px3, BLK_VB, d_b), NH_B)
        kcb = _split_heads(col(pc3, BLK_KB, d_b), NH_B)
        vcb = _split_heads(col(pc3, BLK_VB, d_b), NH_B)
        bias = _natten_bias(rpb[l], rows)
        na = _natten(qb, kb, vb, kcb, vcb, bias).transpose(0, 2, 1, 3).reshape(bn, S, d_b)

        x_mid = _merge(h_a, px, na, x, mod3, mlstm_norm_g[l].reshape(1, d_a),
                       w_branch_a[l].astype(BF16), w_branch_b[l].astype(BF16), w_out[l].astype(BF16))
        x = _ffn(x_mid, mod3, norm2_g[l], w_ffn_gate[l].astype(BF16), w_ffn_up[l].astype(BF16),
                 w_ffn_down[l].astype(BF16))
    return x
```

```python
import functools

import numpy as np
import jax
import jax.numpy as jnp
from jax import lax
from jax.experimental import pallas as pl
from jax.experimental.pallas import tpu as pltpu

F32 = jnp.float32
BF16 = jnp.bfloat16

GRID_W = 64
NH_A, DK_A, DV_A = 8, 64, 128
NH_B, DH_B = 16, 64
CHUNK = 128
GATE_CAP = 15.0
WIN_R, WIN_C = 8, 16
ROPE_THETA = 10000.0
EPS = 1e-6
N_MOD = 6
LANES = 128
VMEM_LIMIT = 56 * 1024 * 1024

QROWS = 4
KROWS = QROWS + WIN_R - 1

N_PAIR = NH_A // 2
C_EXT = 2 * DV_A

G_A, G_WL, G_B, G_TOT = 0, 32, 72, 104


def _cparams(sem):
    return pltpu.CompilerParams(dimension_semantics=sem, vmem_limit_bytes=VMEM_LIMIT)


def _mod_kernel(c_ref, w_ref, b_ref, o_ref):
    c = c_ref[...]
    h = c * jax.nn.sigmoid(c)
    o_ref[...] = jnp.dot(h, w_ref[...], preferred_element_type=F32,
                         precision=lax.Precision.HIGHEST) + b_ref[...]


def _modulation(cc, w_mod, b_mod):
    rows, d = cc.shape
    n = w_mod.shape[1]
    tn = 1024
    return pl.pallas_call(
        _mod_kernel,
        grid=(n // tn,),
        in_specs=[pl.BlockSpec((rows, d), lambda j: (0, 0)),
                  pl.BlockSpec((d, tn), lambda j: (0, j)),
                  pl.BlockSpec((1, tn), lambda j: (0, j))],
        out_specs=pl.BlockSpec((rows, tn), lambda j: (0, j)),
        out_shape=jax.ShapeDtypeStruct((rows, n), F32),
        compiler_params=_cparams(("arbitrary",)),
        name="modulation",
    )(cc, w_mod, b_mod.reshape(1, n))


def _prenorm_kernel(x_ref, mod_ref, g_ref, o_ref, *, sh_idx, sc_idx):
    x = x_ref[...]
    ms = jnp.mean(x * x, axis=-1, keepdims=True)
    y = x * lax.rsqrt(ms + EPS) * g_ref[...]
    sc = mod_ref[sc_idx:sc_idx + 1, :]
    sh = mod_ref[sh_idx:sh_idx + 1, :]
    o_ref[...] = (y * (1.0 + sc) + sh).astype(o_ref.dtype)


def _prenorm(x, mod3, g, *, sh_idx, sc_idx, ctx_row=None, tm=512):
    bn, L, d = x.shape
    tm = min(tm, L)
    if ctx_row is None:
        mod_map = lambda b, i: (b, 0, 0)
    else:
        mod_map = lambda b, i: (ctx_row, 0, 0)
    return pl.pallas_call(
        functools.partial(_prenorm_kernel, sh_idx=sh_idx, sc_idx=sc_idx),
        grid=(bn, L // tm),
        in_specs=[pl.BlockSpec((None, tm, d), lambda b, i: (b, i, 0)),
                  pl.BlockSpec((None, N_MOD, d), mod_map),
                  pl.BlockSpec((1, d), lambda b, i: (0, 0))],
        out_specs=pl.BlockSpec((None, tm, d), lambda b, i: (b, i, 0)),
        out_shape=jax.ShapeDtypeStruct((bn, L, d), BF16),
        compiler_params=_cparams(("parallel", "parallel")),
        name="prenorm",
    )(x, mod3, g.reshape(1, d))


PROJ_TN = 512
BLK_QA, BLK_KA, BLK_VA, BLK_OA, BLK_QB, BLK_KB, BLK_VB, BLK_MG = 0, 1, 2, 4, 6, 8, 10, 12
N_PROJ_BLK = 16


def _swap16(x):
    n = x.shape[-1]
    lane = lax.broadcasted_iota(jnp.int32, x.shape, x.ndim - 1)
    first = (lane & 31) < 16
    return jnp.where(first, pltpu.roll(x, n - 16, x.ndim - 1), pltpu.roll(x, 16, x.ndim - 1))


def _proj_kernel(a_ref, w_ref, aux_ref, cos_ref, sin_ref, gsum_ref, o_ref, *, rope):
    j = pl.program_id(1)
    acc = jnp.dot(a_ref[...], w_ref[...], preferred_element_type=F32)
    scale = aux_ref[...]
    is_rope = j < BLK_VA
    is_norm = jnp.logical_and(j >= BLK_QB, j < BLK_VB)

    if rope:
        @pl.when(is_rope)
        def _():
            reps = PROJ_TN // LANES
            cos = jnp.concatenate([cos_ref[...]] * reps, axis=1)
            sin = jnp.concatenate([sin_ref[...]] * reps, axis=1)
            y = acc * cos + _swap16(acc) * sin
            o_ref[...] = (y * scale).astype(o_ref.dtype)
        plain = jnp.logical_not(jnp.logical_or(is_rope, is_norm))
    else:
        plain = jnp.logical_not(is_norm)

    @pl.when(is_norm)
    def _():
        sq = (acc * acc).astype(BF16)
        ss = jnp.dot(sq, gsum_ref[...], preferred_element_type=F32)
        y = acc * lax.rsqrt(ss * (1.0 / DH_B) + EPS)
        o_ref[...] = (y * scale).astype(o_ref.dtype)

    @pl.when(plain)
    def _():
        o_ref[...] = (acc * scale).astype(o_ref.dtype)


def _projection(a, w, aux, cos, sin, gsum, *, rope, seq_len, tm=1024):
    m, d = a.shape
    tm = min(tm, seq_len)
    n = w.shape[1]
    nb = n // PROJ_TN
    pos_blocks = seq_len // tm
    return pl.pallas_call(
        functools.partial(_proj_kernel, rope=rope),
        grid=(m // tm, nb),
        in_specs=[pl.BlockSpec((tm, d), lambda i, j: (i, 0)),
                  pl.BlockSpec((d, PROJ_TN), lambda i, j: (0, j)),
                  pl.BlockSpec((None, 1, PROJ_TN), lambda i, j: (j, 0, 0)),
                  pl.BlockSpec((tm, LANES), lambda i, j: (i % pos_blocks, 0)),
                  pl.BlockSpec((tm, LANES), lambda i, j: (i % pos_blocks, 0)),
                  pl.BlockSpec((PROJ_TN, PROJ_TN), lambda i, j: (0, 0))],
        out_specs=pl.BlockSpec((tm, PROJ_TN), lambda i, j: (i, j)),
        out_shape=jax.ShapeDtypeStruct((m, n), BF16),
        compiler_params=_cparams(("parallel", "arbitrary")),
        name="projection",
    )(a, w, aux, cos, sin, gsum)


def _gate_kernel(a_ref, w_ref, b_ref, gc_ref, gr_ref):
    tm = a_ref.shape[0]
    g = jnp.dot(a_ref[...], w_ref[...], preferred_element_type=F32) + b_ref[...]
    g = GATE_CAP * jnp.tanh(g * (1.0 / GATE_CAP))
    lane_t = lax.broadcasted_iota(jnp.int32, g.shape, 1)
    is_forget = (lane_t & NH_A) != 0
    x = jnp.where(is_forget, jax.nn.log_sigmoid(g), g)
    x = jnp.where(lane_t < 4 * NH_A, x, 0.0)

    t_idx = lax.broadcasted_iota(jnp.int32, (CHUNK, CHUNK), 0)
    s_idx = lax.broadcasted_iota(jnp.int32, (CHUNK, CHUNK), 1)
    tri_f = (s_idx <= t_idx).astype(F32)
    tri_b = (s_idx >= t_idx).astype(F32)
    lane = s_idx
    fwd_half = lane < 2 * NH_A
    low = lane < 4 * NH_A
    hi = lax.Precision.HIGHEST
    for ci in range(tm // CHUNK):
        xc = x[ci * CHUNK:(ci + 1) * CHUNK]
        cf = jnp.dot(tri_f, xc, preferred_element_type=F32, precision=hi)
        cb = jnp.dot(tri_b, xc, preferred_element_type=F32, precision=hi)
        tot = jnp.broadcast_to(jnp.sum(xc, axis=0, keepdims=True), xc.shape)
        r_b = jnp.where(fwd_half, pltpu.roll(cf, LANES - NH_A, 1), pltpu.roll(cb, LANES - NH_A, 1))
        a = xc - r_b
        wl = xc + (pltpu.roll(tot, LANES - NH_A, 1) - r_b)
        bsel = jnp.where(fwd_half, cf, cb)
        z = lambda v: jnp.where(low, v, 0.0)
        packed = z(a) + pltpu.roll(z(wl), 32, 1) + pltpu.roll(z(bsel), 64, 1) + pltpu.roll(z(tot), 96, 1)
        gc_ref[ci * CHUNK:(ci + 1) * CHUNK, :] = packed
        gr_ref[:, ci * CHUNK:(ci + 1) * CHUNK] = packed.T


def _gate_projection(a, w_g, b_g, tm=512):
    m, d = a.shape
    tm = min(tm, m)
    return pl.pallas_call(
        _gate_kernel,
        grid=(m // tm,),
        in_specs=[pl.BlockSpec((tm, d), lambda i: (i, 0)),
                  pl.BlockSpec((d, LANES), lambda i: (0, 0)),
                  pl.BlockSpec((1, LANES), lambda i: (0, 0))],
        out_specs=[pl.BlockSpec((tm, LANES), lambda i: (i, 0)),
                   pl.BlockSpec((LANES, tm), lambda i: (0, i))],
        out_shape=[jax.ShapeDtypeStruct((m, LANES), F32),
                   jax.ShapeDtypeStruct((LANES, m), F32)],
        compiler_params=_cparams(("parallel",)),
        name="gate_projection",
    )(a, w_g, b_g)


def _ones_block(rows):
    return (lax.broadcasted_iota(jnp.int32, (rows, DV_A), 1) == 0).astype(BF16)


def _mlstm_state_kernel(kf_ref, vf_ref, gf_ref, kb_ref, vb_ref, gb_ref, c0_ref,
                        cf_ref, cb_ref, cfin_ref, st, *, nc):
    s = pl.program_id(1)

    @pl.when(s == 0)
    def _():
        st[...] = c0_ref[...]

    lc = kf_ref.shape[0]
    lo = lax.broadcasted_iota(jnp.int32, (lc, 2 * DK_A), 1) < DK_A
    row = lax.broadcasted_iota(jnp.int32, (2 * DK_A, 2 * C_EXT), 0)
    col = lax.broadcasted_iota(jnp.int32, (2 * DK_A, 2 * C_EXT), 1)
    top = row < DK_A
    diag = top == (col < C_EXT)
    ones_blk = _ones_block(lc)
    for d, (k_ref, v_ref, g_ref, out_ref) in enumerate(((kf_ref, vf_ref, gf_ref, cf_ref),
                                                        (kb_ref, vb_ref, gb_ref, cb_ref))):
        g = g_ref[...]
        for p in range(N_PAIR):
            h0, h1 = 2 * p, 2 * p + 1
            wl = G_WL + 16 * d
            w = jnp.where(lo, jnp.exp(g[:, wl + h0:wl + h0 + 1]), jnp.exp(g[:, wl + h1:wl + h1 + 1]))
            kw = (k_ref[:, p * 2 * DK_A:(p + 1) * 2 * DK_A].astype(F32) * w).astype(BF16)
            vext = jnp.concatenate([v_ref[:, h0 * DV_A:(h0 + 1) * DV_A], ones_blk,
                                    v_ref[:, h1 * DV_A:(h1 + 1) * DV_A], ones_blk], axis=1)
            upd = lax.dot_general(kw, vext, (((0,), (0,)), ((), ())), preferred_element_type=F32)
            c_old = st[d, p]
            out_ref[p, 0] = c_old[:DK_A, :C_EXT].astype(out_ref.dtype)
            out_ref[p, 1] = c_old[DK_A:, C_EXT:].astype(out_ref.dtype)
            tl = G_TOT + 16 * d
            dec = jnp.where(top, jnp.exp(g[0:1, tl + h0:tl + h0 + 1]), jnp.exp(g[0:1, tl + h1:tl + h1 + 1]))
            st[d, p] = dec * c_old + jnp.where(diag, upd, 0.0)

    @pl.when(s == nc - 1)
    def _():
        cfin_ref[...] = st[...]


def _mlstm_states(proj, gc, c0, *, bn, seq_len):
    lc = min(CHUNK, seq_len)
    nc = seq_len // lc
    kcol = BLK_KA * PROJ_TN // (NH_A * DK_A)
    vcol = BLK_VA * PROJ_TN // (NH_A * DV_A)
    fwd = lambda b, s: b * nc + s
    bwd = lambda b, s: b * nc + nc - 1 - s
    st_shape = (2, N_PAIR, 2 * DK_A, 2 * C_EXT)
    out_blk = (None, None, N_PAIR, 2, DK_A, C_EXT)
    return pl.pallas_call(
        functools.partial(_mlstm_state_kernel, nc=nc),
        grid=(bn, nc),
        in_specs=[pl.BlockSpec((lc, NH_A * DK_A), lambda b, s: (fwd(b, s), kcol)),
                  pl.BlockSpec((lc, NH_A * DV_A), lambda b, s: (fwd(b, s), vcol)),
                  pl.BlockSpec((lc, LANES), lambda b, s: (fwd(b, s), 0)),
                  pl.BlockSpec((lc, NH_A * DK_A), lambda b, s: (bwd(b, s), kcol)),
                  pl.BlockSpec((lc, NH_A * DV_A), lambda b, s: (bwd(b, s), vcol)),
                  pl.BlockSpec((lc, LANES), lambda b, s: (bwd(b, s), 0)),
                  pl.BlockSpec((None,) + st_shape, lambda b, s: (b, 0, 0, 0, 0))],
        out_specs=[pl.BlockSpec(out_blk, lambda b, s: (b, s, 0, 0, 0, 0)),
                   pl.BlockSpec(out_blk, lambda b, s: (b, nc - 1 - s, 0, 0, 0, 0)),
                   pl.BlockSpec((None,) + st_shape, lambda b, s: (b, 0, 0, 0, 0))],
        out_shape=[jax.ShapeDtypeStruct((bn, nc, N_PAIR, 2, DK_A, C_EXT), BF16),
                   jax.ShapeDtypeStruct((bn, nc, N_PAIR, 2, DK_A, C_EXT), BF16),
                   jax.ShapeDtypeStruct((bn,) + st_shape, F32)],
        scratch_shapes=[pltpu.VMEM(st_shape, F32)],
        compiler_params=_cparams(("parallel", "arbitrary")),
        name="mlstm_states",
    )(proj, proj, gc, proj, proj, gc, c0)


def _mlstm_out_kernel(q_ref, k_ref, v_ref, gc_ref, gr_ref, cf_ref, cb_ref, h_ref):
    lc = q_ref.shape[0]
    gc = gc_ref[...]
    gr = gr_ref[...]
    t_idx = lax.broadcasted_iota(jnp.int32, (lc, lc), 0)
    s_idx = lax.broadcasted_iota(jnp.int32, (lc, lc), 1)
    visible = (s_idx <= t_idx, s_idx >= t_idx)
    lo = lax.broadcasted_iota(jnp.int32, (lc, 2 * DK_A), 1) < DK_A
    ones_blk = _ones_block(lc)
    zero_c = jnp.zeros((DK_A, C_EXT), BF16)
    nt = (((1,), (1,)), ((), ()))
    for p in range(N_PAIR):
        qp = q_ref[:, p * 2 * DK_A:(p + 1) * 2 * DK_A]
        kp = k_ref[:, p * 2 * DK_A:(p + 1) * 2 * DK_A]
        zq = jnp.zeros_like(qp)
        q_stack = jnp.concatenate([jnp.where(lo, qp, zq), jnp.where(lo, zq, qp)], axis=0)
        s_both = lax.dot_general(q_stack, kp, nt, preferred_element_type=F32)
        inter = []
        for d, c_ref in enumerate((cf_ref, cb_ref)):
            bl = G_B + 16 * d
            e = jnp.where(lo, jnp.exp(gc[:, bl + 2 * p:bl + 2 * p + 1]),
                          jnp.exp(gc[:, bl + 2 * p + 1:bl + 2 * p + 2]))
            qs = (qp.astype(F32) * e).astype(BF16)
            c_pair = jnp.concatenate([jnp.concatenate([c_ref[p, 0], zero_c], axis=1),
                                      jnp.concatenate([zero_c, c_ref[p, 1]], axis=1)], axis=0)
            inter.append(jnp.dot(qs, c_pair, preferred_element_type=F32))
        for hh in range(2):
            h = 2 * p + hh
            s_h = s_both[hh * lc:(hh + 1) * lc]
            vext = jnp.concatenate([v_ref[:, h * DV_A:(h + 1) * DV_A], ones_blk], axis=1)
            acc = None
            for d in range(2):
                b_col = gc[:, G_B + 16 * d + h:G_B + 16 * d + h + 1]
                a_row = gr[G_A + 16 * d + h:G_A + 16 * d + h + 1, :]
                log_d = jnp.where(visible[d], b_col + a_row, -jnp.inf)
                pm = (s_h * jnp.exp(log_d)).astype(BF16)
                nd = jnp.dot(pm, vext, preferred_element_type=F32) + inter[d][:, hh * C_EXT:(hh + 1) * C_EXT]
                hd = nd[:, :DV_A] / jnp.maximum(jnp.abs(nd[:, DV_A:DV_A + 1]), 1.0)
                acc = hd if acc is None else acc + hd
            h_ref[:, h * DV_A:(h + 1) * DV_A] = acc.astype(h_ref.dtype)


def _mlstm_outputs(proj, gc, gr, cf, cb, *, bn, seq_len):
    lc = min(CHUNK, seq_len)
    nc = seq_len // lc
    qcol = BLK_QA * PROJ_TN // (NH_A * DK_A)
    kcol = BLK_KA * PROJ_TN // (NH_A * DK_A)
    vcol = BLK_VA * PROJ_TN // (NH_A * DV_A)
    tok = lambda b, c: b * nc + c
    st_blk = (None, None, N_PAIR, 2, DK_A, C_EXT)
    st_map = lambda b, c: (b, c, 0, 0, 0, 0)
    return pl.pallas_call(
        _mlstm_out_kernel,
        grid=(bn, nc),
        in_specs=[pl.BlockSpec((lc, NH_A * DK_A), lambda b, c: (tok(b, c), qcol)),
                  pl.BlockSpec((lc, NH_A * DK_A), lambda b, c: (tok(b, c), kcol)),
                  pl.BlockSpec((lc, NH_A * DV_A), lambda b, c: (tok(b, c), vcol)),
                  pl.BlockSpec((lc, LANES), lambda b, c: (tok(b, c), 0)),
                  pl.BlockSpec((LANES, lc), lambda b, c: (0, tok(b, c))),
                  pl.BlockSpec(st_blk, st_map),
                  pl.BlockSpec(st_blk, st_map)],
        out_specs=pl.BlockSpec((lc, NH_A * DV_A), lambda b, c: (tok(b, c), 0)),
        out_shape=jax.ShapeDtypeStruct((bn * seq_len, NH_A * DV_A), F32),
        compiler_params=_cparams(("parallel", "parallel")),
        name="mlstm_outputs",
    )(proj, proj, proj, gc, gr, cf, cb)


NAT_QB = 2
LOG2E = 1.4426950408889634


def _natten_kernel(q_ref, k_ref, v_ref, kc_ref, vc_ref, ba_ref, bb_ref, o_ref, *, seq_len):
    i2 = pl.program_id(2)
    nq = QROWS * GRID_W
    nk = KROWS * GRID_W
    lctx = kc_ref.shape[0]
    nt = (((1,), (1,)), ((), ()))

    def halves(rows):
        lane = lax.broadcasted_iota(jnp.int32, (rows, 2 * DH_B), 1)
        return lane, lane < DH_B

    _, lo_q = halves(nq)
    lane_k, lo_k = halves(nk)
    lane_c, lo_c = halves(lctx)
    kc = kc_ref[...]
    vc = vc_ref[...]
    zq = jnp.zeros((nq, 2 * DH_B), BF16)
    for j, bias_ref in enumerate((ba_ref, bb_ref)):
        blk = NAT_QB * i2 + j
        start = jnp.clip(blk * nq - (WIN_R // 2) * GRID_W, 0, seq_len - nk)
        start = pl.multiple_of(start, GRID_W)
        q = q_ref[j * nq:(j + 1) * nq, :]
        kb = k_ref[pl.ds(start, nk), :]
        vb = v_ref[pl.ds(start, nk), :]
        outs = []
        for h in range(2):
            one_lane = DH_B if h == 0 else 0
            own_q, own_k, own_c = (lo_q, lo_k, lo_c) if h == 0 else (~lo_q, ~lo_k, ~lo_c)
            qh = jnp.where(own_q, q, zq)
            s_win = lax.dot_general(qh, kb, nt, preferred_element_type=F32) + bias_ref[h]
            s_ctx = lax.dot_general(qh, kc, nt, preferred_element_type=F32)
            m = jnp.maximum(jnp.max(s_win, axis=-1, keepdims=True), jnp.max(s_ctx, axis=-1, keepdims=True))
            p_win = jnp.exp2(s_win - m).astype(BF16)
            p_ctx = jnp.exp2(s_ctx - m).astype(BF16)
            vh = jnp.where(own_k, vb, (lane_k == one_lane).astype(BF16))
            vch = jnp.where(own_c, vc, (lane_c == one_lane).astype(BF16))
            o = (jnp.dot(p_win, vh, preferred_element_type=F32)
                 + jnp.dot(p_ctx, vch, preferred_element_type=F32))
            outs.append(o / o[:, one_lane:one_lane + 1])
        o_ref[j * nq:(j + 1) * nq, :] = jnp.where(lo_q, outs[0], outs[1]).astype(o_ref.dtype)


def _natten_patterns(rows):
    nblk = rows // QROWS
    pats = []
    for blk in (0, 1 if nblk > 2 else 0, nblk - 1):
        r0 = blk * QROWS
        k0 = int(np.clip(r0 - WIN_R // 2, 0, rows - KROWS))
        qr = r0 + np.arange(QROWS)
        kr = k0 + np.arange(KROWS)
        wr0 = np.clip(qr - WIN_R // 2, 0, rows - WIN_R)
        valid = (kr[None, :] >= wr0[:, None]) & (kr[None, :] < wr0[:, None] + WIN_R)
        dr = np.clip(kr[None, :] - qr[:, None] + WIN_R - 1, 0, 2 * WIN_R - 2)
        pats.append((valid, dr))
    return pats


def _bias_kernel(t_ref, o_ref, *, patterns):
    pat = pl.program_id(0)
    for ps, (valid, dr) in enumerate(patterns):
        @pl.when(pat == ps)
        def _():
            for xq in range(QROWS):
                for yk in range(KROWS):
                    if valid[xq, yk]:
                        blk = t_ref[int(dr[xq, yk])]
                    else:
                        blk = jnp.full((GRID_W, GRID_W), -jnp.inf, F32)
                    o_ref[xq * GRID_W:(xq + 1) * GRID_W, yk * GRID_W:(yk + 1) * GRID_W] = blk


def _natten_bias(rpb, rows):
    nh = rpb.shape[0]
    qc = np.arange(GRID_W)
    kc = np.arange(GRID_W)
    wc0 = np.clip(qc - WIN_C // 2, 0, GRID_W - WIN_C)
    in_c = (kc[None, :] >= wc0[:, None]) & (kc[None, :] < wc0[:, None] + WIN_C)
    dc = np.clip(kc[None, :] - qc[:, None] + WIN_C - 1, 0, 2 * WIN_C - 2)
    sel_c = jnp.asarray(dc[:, :, None] == np.arange(2 * WIN_C - 1), F32)
    tcol = jnp.einsum('hab,uvb->hauv', rpb.astype(F32) * LOG2E, sel_c, precision=lax.Precision.HIGHEST)
    tcol = jnp.where(in_c[None, None], tcol, -jnp.inf)
    nq, nk = QROWS * GRID_W, KROWS * GRID_W
    n_dr = 2 * WIN_R - 1
    return pl.pallas_call(
        functools.partial(_bias_kernel, patterns=_natten_patterns(rows)),
        grid=(3, nh),
        in_specs=[pl.BlockSpec((None, n_dr, GRID_W, GRID_W), lambda p, h: (h, 0, 0, 0))],
        out_specs=pl.BlockSpec((None, None, nq, nk), lambda p, h: (p, h, 0, 0)),
        out_shape=jax.ShapeDtypeStruct((3, nh, nq, nk), F32),
        compiler_params=_cparams(("arbitrary", "arbitrary")),
        name="natten_bias",
    )(tcol)


def _natten(px, pc, bias, *, bn, seq_len, lctx):
    nq = QROWS * GRID_W
    nk = KROWS * GRID_W
    nstep = seq_len // (NAT_QB * nq)
    pw = 2 * DH_B
    n_pair = NH_B // 2
    qcol, kcol, vcol = (blk * PROJ_TN // pw for blk in (BLK_QB, BLK_KB, BLK_VB))
    last = nstep - 1
    first_pat = lambda i: jnp.where(i == 0, 0, 1)
    second_pat = lambda i: jnp.where(i == last, 2, 1)
    return pl.pallas_call(
        functools.partial(_natten_kernel, seq_len=seq_len),
        grid=(bn, n_pair, nstep),
        in_specs=[pl.BlockSpec((NAT_QB * nq, pw), lambda b, p, i: (b * nstep + i, qcol + p)),
                  pl.BlockSpec((seq_len, pw), lambda b, p, i: (b, kcol + p)),
                  pl.BlockSpec((seq_len, pw), lambda b, p, i: (b, vcol + p)),
                  pl.BlockSpec((lctx, pw), lambda b, p, i: (b, kcol + p)),
                  pl.BlockSpec((lctx, pw), lambda b, p, i: (b, vcol + p)),
                  pl.BlockSpec((None, 2, nq, nk), lambda b, p, i: (first_pat(i), p, 0, 0)),
                  pl.BlockSpec((None, 2, nq, nk), lambda b, p, i: (second_pat(i), p, 0, 0))],
        out_specs=pl.BlockSpec((NAT_QB * nq, pw), lambda b, p, i: (b * nstep + i, p)),
        out_shape=jax.ShapeDtypeStruct((bn * seq_len, NH_B * DH_B), BF16),
        compiler_params=_cparams(("parallel", "parallel", "arbitrary")),
        name="natten",
    )(px, px, px, pc, pc, bias, bias)


def _merge_kernel(h_ref, oa_ref, na_ref, mg_ref, x_ref, mod_ref, gn_ref, wa_ref, wb_ref, wo_ref, o_ref):
    h = h_ref[...]
    parts = []
    for hd in range(NH_A):
        hh = h[:, hd * DV_A:(hd + 1) * DV_A]
        ms = jnp.mean(hh * hh, axis=-1, keepdims=True)
        parts.append(hh * lax.rsqrt(ms + EPS))
    ha = jnp.concatenate(parts, axis=1) * gn_ref[...]
    ha = (ha * jax.nn.sigmoid(oa_ref[...].astype(F32))).astype(BF16)
    d = wa_ref.shape[1]
    gates = jax.nn.sigmoid(mg_ref[...].astype(F32))
    t = (gates[:, :d] * jnp.dot(ha, wa_ref[...], preferred_element_type=F32)
         + gates[:, d:] * jnp.dot(na_ref[...], wb_ref[...], preferred_element_type=F32))
    mix = jnp.dot(t.astype(BF16), wo_ref[...], preferred_element_type=F32)
    g1 = mod_ref[2:3, :]
    o_ref[...] = x_ref[...] + g1 * mix


def _merge(h, proj, na, x, mod3, gn, wa, wb, wo, tm=256):
    bn, S, d = x.shape
    tm = min(tm, S)
    nblk = S // tm
    d_a = h.shape[-1]
    d_b = na.shape[-1]
    oa_blk = BLK_OA * PROJ_TN // d_a
    mg_blk = BLK_MG * PROJ_TN // (2 * d)
    const = lambda b, i: (0, 0)
    return pl.pallas_call(
        _merge_kernel,
        grid=(bn, nblk),
        in_specs=[pl.BlockSpec((tm, d_a), lambda b, i: (b * nblk + i, 0)),
                  pl.BlockSpec((tm, d_a), lambda b, i: (b * nblk + i, oa_blk)),
                  pl.BlockSpec((tm, d_b), lambda b, i: (b * nblk + i, 0)),
                  pl.BlockSpec((tm, 2 * d), lambda b, i: (b * nblk + i, mg_blk)),
                  pl.BlockSpec((None, tm, d), lambda b, i: (b, i, 0)),
                  pl.BlockSpec((None, N_MOD, d), lambda b, i: (b, 0, 0)),
                  pl.BlockSpec((1, d_a), const),
                  pl.BlockSpec((d_a, d), const),
                  pl.BlockSpec((d_b, d), const),
                  pl.BlockSpec((d, d), const)],
        out_specs=pl.BlockSpec((None, tm, d), lambda b, i: (b, i, 0)),
        out_shape=jax.ShapeDtypeStruct((bn, S, d), F32),
        compiler_params=_cparams(("parallel", "parallel")),
        name="merge",
    )(h, proj, na, proj, x, mod3, gn, wa, wb, wo)


def _ffn_kernel(x_ref, mod_ref, g_ref, wg_ref, wu_ref, wd_ref, o_ref):
    x = x_ref[...]
    ms = jnp.mean(x * x, axis=-1, keepdims=True)
    y = x * lax.rsqrt(ms + EPS) * g_ref[...]
    hx = (y * (1.0 + mod_ref[4:5, :]) + mod_ref[3:4, :]).astype(BF16)
    a = jnp.dot(hx, wg_ref[...], preferred_element_type=F32)
    u = jnp.dot(hx, wu_ref[...], preferred_element_type=F32)
    act = (a * jax.nn.sigmoid(a) * u).astype(BF16)
    f = jnp.dot(act, wd_ref[...], preferred_element_type=F32)
    o_ref[...] = x + mod_ref[5:6, :] * f


def _ffn(x, mod3, g, wg, wu, wd, tm=256):
    bn, S, d = x.shape
    tm = min(tm, S)
    dff = wg.shape[1]
    const = lambda b, i: (0, 0)
    return pl.pallas_call(
        _ffn_kernel,
        grid=(bn, S // tm),
        in_specs=[pl.BlockSpec((None, tm, d), lambda b, i: (b, i, 0)),
                  pl.BlockSpec((None, N_MOD, d), lambda b, i: (b, 0, 0)),
                  pl.BlockSpec((1, d), const),
                  pl.BlockSpec((d, dff), const),
                  pl.BlockSpec((d, dff), const),
                  pl.BlockSpec((dff, d), const)],
        out_specs=pl.BlockSpec((None, tm, d), lambda b, i: (b, i, 0)),
        out_shape=jax.ShapeDtypeStruct((bn, S, d), F32),
        compiler_params=_cparams(("parallel", "parallel")),
        name="ffn",
    )(x, mod3, g.reshape(1, d), wg, wu, wd)


def _rope_tables(S):
    t = np.arange(S)
    row, col = t // GRID_W, t % GRID_W
    half = DK_A // 4
    inv = ROPE_THETA ** (-np.arange(half, dtype=np.float64) / half)
    ang_r = row[:, None] * inv[None, :]
    ang_c = col[:, None] * inv[None, :]
    cos = np.concatenate([np.cos(ang_r)] * 2 + [np.cos(ang_c)] * 2, axis=1)
    sin = np.concatenate([-np.sin(ang_r), np.sin(ang_r), -np.sin(ang_c), np.sin(ang_c)], axis=1)
    reps = LANES // DK_A
    return (jnp.asarray(np.tile(cos, (1, reps)), F32), jnp.asarray(np.tile(sin, (1, reps)), F32))


def kernel(x, c, ctx, c_ctx, w_mod, b_mod, norm1_g, w_in, b_gates, mlstm_norm_g, qn_g, kn_g, rpb,
           w_branch_a, w_branch_b, w_out, norm2_g, w_ffn_gate, w_ffn_up, w_ffn_down):
    bn, S, d = x.shape
    lctx = ctx.shape[1]
    rows = S // GRID_W
    depth = w_mod.shape[0]
    d_a, d_b = NH_A * DV_A, NH_B * DH_B
    dqk = NH_A * DK_A
    sizes = (dqk, dqk, d_a, d_a, 4 * NH_A, d_b, d_b, d_b, 2 * d)
    offs = np.cumsum((0,) + sizes)
    cos_t, sin_t = _rope_tables(S)
    eye = np.kron(np.eye(PROJ_TN // DH_B), np.ones((DH_B, DH_B)))
    gsum = jnp.asarray(eye, BF16)

    for l in range(depth):
        assert l == depth - 1, "context-stream update for non-final layers is not implemented"
        n_rows = -(-(bn + 1) // 8) * 8
        cc = jnp.concatenate([c, c_ctx[None, :], jnp.zeros((n_rows - bn - 1, d), F32)], axis=0)
        mod3 = _modulation(cc, w_mod[l], b_mod[l]).reshape(n_rows, N_MOD, d)

        w = w_in[l]
        seg = lambda i: w[:, int(offs[i]):int(offs[i + 1])]
        w_main = jnp.concatenate([seg(0), seg(1), seg(2), seg(3), seg(5), seg(6), seg(7), seg(8)],
                                 axis=1).astype(BF16)
        w_g = jnp.pad(seg(4), ((0, 0), (0, LANES - 4 * NH_A))).astype(BF16)
        b_g = jnp.pad(b_gates[l].reshape(1, 4 * NH_A), ((0, 0), (0, LANES - 4 * NH_A)))
        aux = np.ones((N_PROJ_BLK, 1, PROJ_TN), np.float32)
        aux = jnp.asarray(aux)
        qscale = jnp.full((PROJ_TN,), DK_A ** -0.5, F32)
        qn = jnp.tile(qn_g[l], PROJ_TN // DH_B) * (DH_B ** -0.5 * LOG2E)
        kn = jnp.tile(kn_g[l], PROJ_TN // DH_B)
        aux = aux.at[BLK_QA, 0].set(qscale)
        aux = aux.at[BLK_QB, 0].set(qn).at[BLK_QB + 1, 0].set(qn)
        aux = aux.at[BLK_KB, 0].set(kn).at[BLK_KB + 1, 0].set(kn)

        hx = _prenorm(x, mod3, norm1_g[l], sh_idx=0, sc_idx=1)
        hc = _prenorm(ctx, mod3, norm1_g[l], sh_idx=0, sc_idx=1, ctx_row=bn)
        px = _projection(hx.reshape(bn * S, d), w_main, aux, cos_t, sin_t, gsum, rope=True, seq_len=S)
        pc = _projection(hc.reshape(bn * lctx, d), w_main, aux, cos_t, sin_t, gsum, rope=False, seq_len=lctx)
        gcx, grx = _gate_projection(hx.reshape(bn * S, d), w_g, b_g)
        gcc, _ = _gate_projection(hc.reshape(bn * lctx, d), w_g, b_g)

        zero_state = jnp.zeros((bn, 2, N_PAIR, 2 * DK_A, 2 * C_EXT), F32)
        _, _, st_ctx = _mlstm_states(pc, gcc, zero_state, bn=bn, seq_len=lctx)
        cf, cb, _ = _mlstm_states(px, gcx, st_ctx, bn=bn, seq_len=S)
        h_a = _mlstm_outputs(px, gcx, grx, cf, cb, bn=bn, seq_len=S)

        bias = _natten_bias(rpb[l], rows)
        na = _natten(px, pc, bias, bn=bn, seq_len=S, lctx=lctx)

        x_mid = _merge(h_a, px, na, x, mod3, mlstm_norm_g[l].reshape(1, d_a),
                       w_branch_a[l].astype(BF16), w_branch_b[l].astype(BF16), w_out[l].astype(BF16))
        x = _ffn(x_mid, mod3, norm2_g[l], w_ffn_gate[l].astype(BF16), w_ffn_up[l].astype(BF16),
                 w_ffn_down[l].astype(BF16))
    return x
```

```python
import functools

import numpy as np
import jax
import jax.numpy as jnp
from jax import lax
from jax.experimental import pallas as pl
from jax.experimental.pallas import tpu as pltpu

F32 = jnp.float32
BF16 = jnp.bfloat16

GRID_W = 64
NH_A, DK_A, DV_A = 8, 64, 128
NH_B, DH_B = 16, 64
CHUNK = 128
GATE_CAP = 15.0
WIN_R, WIN_C = 8, 16
ROPE_THETA = 10000.0
EPS = 1e-6
N_MOD = 6
LANES = 128
VMEM_LIMIT = 56 * 1024 * 1024

QROWS = 4
KROWS = QROWS + WIN_R - 1

N_PAIR = NH_A // 2
C_EXT = 2 * DV_A

G_A, G_WL, G_B, G_TOT = 0, 32, 72, 104


def _cparams(sem):
    return pltpu.CompilerParams(dimension_semantics=sem, vmem_limit_bytes=VMEM_LIMIT)


def _mod_kernel(c_ref, w_ref, b_ref, o_ref):
    c = c_ref[...]
    h = c * jax.nn.sigmoid(c)
    o_ref[...] = jnp.dot(h, w_ref[...], preferred_element_type=F32,
                         precision=lax.Precision.HIGHEST) + b_ref[...]


def _modulation(cc, w_mod, b_mod):
    rows, d = cc.shape
    n = w_mod.shape[1]
    tn = 1024
    return pl.pallas_call(
        _mod_kernel,
        grid=(n // tn,),
        in_specs=[pl.BlockSpec((rows, d), lambda j: (0, 0)),
                  pl.BlockSpec((d, tn), lambda j: (0, j)),
                  pl.BlockSpec((1, tn), lambda j: (0, j))],
        out_specs=pl.BlockSpec((rows, tn), lambda j: (0, j)),
        out_shape=jax.ShapeDtypeStruct((rows, n), F32),
        compiler_params=_cparams(("arbitrary",)),
        name="modulation",
    )(cc, w_mod, b_mod.reshape(1, n))


def _mod_norm(x, gain, scale, shift):
    ms = jnp.mean(x * x, axis=-1, keepdims=True)
    y = x * lax.rsqrt(ms + EPS) * gain
    return y * (1.0 + scale) + shift


def _gate_pack(g, gc_ref, gr_ref):
    tm = g.shape[0]
    g = GATE_CAP * jnp.tanh(g * (1.0 / GATE_CAP))
    lane_t = lax.broadcasted_iota(jnp.int32, g.shape, 1)
    is_forget = (lane_t & NH_A) != 0
    x = jnp.where(is_forget, jax.nn.log_sigmoid(g), g)
    x = jnp.where(lane_t < 4 * NH_A, x, 0.0)

    t_idx = lax.broadcasted_iota(jnp.int32, (CHUNK, CHUNK), 0)
    s_idx = lax.broadcasted_iota(jnp.int32, (CHUNK, CHUNK), 1)
    tri_f = (s_idx <= t_idx).astype(F32)
    tri_b = (s_idx >= t_idx).astype(F32)
    lane = s_idx
    fwd_half = lane < 2 * NH_A
    low = lane < 4 * NH_A
    hi = lax.Precision.HIGHEST
    for ci in range(tm // CHUNK):
        xc = x[ci * CHUNK:(ci + 1) * CHUNK]
        cf = jnp.dot(tri_f, xc, preferred_element_type=F32, precision=hi)
        cb = jnp.dot(tri_b, xc, preferred_element_type=F32, precision=hi)
        tot = jnp.broadcast_to(jnp.sum(xc, axis=0, keepdims=True), xc.shape)
        r_b = jnp.where(fwd_half, pltpu.roll(cf, LANES - NH_A, 1), pltpu.roll(cb, LANES - NH_A, 1))
        a = xc - r_b
        wl = xc + (pltpu.roll(tot, LANES - NH_A, 1) - r_b)
        bsel = jnp.where(fwd_half, cf, cb)
        z = lambda v: jnp.where(low, v, 0.0)
        packed = z(a) + pltpu.roll(z(wl), 32, 1) + pltpu.roll(z(bsel), 64, 1) + pltpu.roll(z(tot), 96, 1)
        gc_ref[ci * CHUNK:(ci + 1) * CHUNK, :] = packed
        gr_ref[:, ci * CHUNK:(ci + 1) * CHUNK] = packed.T


PROJ_TN = 512
BLK_QA, BLK_KA, BLK_VA, BLK_OA, BLK_QB, BLK_KB, BLK_VB, BLK_MG = 0, 1, 2, 4, 6, 8, 10, 12
N_PROJ_BLK = 16
LOG2E = 1.4426950408889634


def _swap16(x):
    n = x.shape[-1]
    lane = lax.broadcasted_iota(jnp.int32, x.shape, x.ndim - 1)
    first = (lane & 31) < 16
    return jnp.where(first, pltpu.roll(x, n - 16, x.ndim - 1), pltpu.roll(x, 16, x.ndim - 1))


def _mixer_in_kernel(x_ref, mod_ref, g_ref, w_ref, wg_ref, bg_ref, qk_ref, cos_ref, sin_ref, gsum_ref,
                     o_ref, gc_ref, gr_ref, *, rope):
    hx = _mod_norm(x_ref[...], g_ref[...], mod_ref[1:2, :], mod_ref[0:1, :]).astype(BF16)
    if rope:
        reps = PROJ_TN // LANES
        cos = jnp.concatenate([cos_ref[...]] * reps, axis=1)
        sin = jnp.concatenate([sin_ref[...]] * reps, axis=1)
    for blk in range(N_PROJ_BLK):
        cols = slice(blk * PROJ_TN, (blk + 1) * PROJ_TN)
        acc = jnp.dot(hx, w_ref[:, cols], preferred_element_type=F32)
        if blk < BLK_VA:
            if rope:
                acc = acc * cos + _swap16(acc) * sin
            if blk == BLK_QA:
                acc = acc * DK_A ** -0.5
        elif BLK_QB <= blk < BLK_VB:
            sq = (acc * acc).astype(BF16)
            ss = jnp.dot(sq, gsum_ref[...], preferred_element_type=F32)
            gain = qk_ref[0:1, :] if blk < BLK_KB else qk_ref[1:2, :]
            acc = acc * lax.rsqrt(ss * (1.0 / DH_B) + EPS) * gain
        o_ref[:, cols] = acc.astype(o_ref.dtype)
    _gate_pack(jnp.dot(hx, wg_ref[...], preferred_element_type=F32) + bg_ref[...], gc_ref, gr_ref)


def _mixer_in(x, mod3, g, w, w_g, b_g, qk_gain, cos, sin, gsum, *, rope, ctx_row=None, tm=512):
    bn, L, d = x.shape
    tm = min(tm, L)
    nblk = L // tm
    n = w.shape[1]
    mod_map = (lambda b, i: (b, 0, 0)) if ctx_row is None else (lambda b, i: (ctx_row, 0, 0))
    const = lambda b, i: (0, 0)
    tok = lambda b, i: (b * nblk + i, 0)
    resident = pl.Buffered(1)
    return pl.pallas_call(
        functools.partial(_mixer_in_kernel, rope=rope),
        grid=(bn, nblk),
        in_specs=[pl.BlockSpec((None, tm, d), lambda b, i: (b, i, 0)),
                  pl.BlockSpec((None, N_MOD, d), mod_map),
                  pl.BlockSpec((1, d), const),
                  pl.BlockSpec((d, n), const, pipeline_mode=resident),
                  pl.BlockSpec((d, LANES), const, pipeline_mode=resident),
                  pl.BlockSpec((1, LANES), const),
                  pl.BlockSpec((2, PROJ_TN), const),
                  pl.BlockSpec((tm, LANES), lambda b, i: (i, 0)),
                  pl.BlockSpec((tm, LANES), lambda b, i: (i, 0)),
                  pl.BlockSpec((PROJ_TN, PROJ_TN), const, pipeline_mode=resident)],
        out_specs=[pl.BlockSpec((tm, n), tok),
                   pl.BlockSpec((tm, LANES), tok),
                   pl.BlockSpec((LANES, tm), lambda b, i: (0, b * nblk + i))],
        out_shape=[jax.ShapeDtypeStruct((bn * L, n), BF16),
                   jax.ShapeDtypeStruct((bn * L, LANES), F32),
                   jax.ShapeDtypeStruct((LANES, bn * L), F32)],
        compiler_params=_cparams(("parallel", "parallel")),
        name="mixer_in",
    )(x, mod3, g.reshape(1, d), w, w_g, b_g, qk_gain, cos, sin, gsum)


def _ones_block(rows):
    return (lax.broadcasted_iota(jnp.int32, (rows, DV_A), 1) == 0).astype(BF16)


def _mlstm_state_kernel(kf_ref, vf_ref, gf_ref, kb_ref, vb_ref, gb_ref, c0_ref,
                        cf_ref, cb_ref, cfin_ref, st, *, nc):
    s = pl.program_id(1)

    @pl.when(s == 0)
    def _():
        st[...] = c0_ref[...]

    lc = kf_ref.shape[0]
    lo = lax.broadcasted_iota(jnp.int32, (lc, 2 * DK_A), 1) < DK_A
    row = lax.broadcasted_iota(jnp.int32, (2 * DK_A, 2 * C_EXT), 0)
    col = lax.broadcasted_iota(jnp.int32, (2 * DK_A, 2 * C_EXT), 1)
    top = row < DK_A
    diag = top == (col < C_EXT)
    ones_blk = _ones_block(lc)
    for d, (k_ref, v_ref, g_ref, out_ref) in enumerate(((kf_ref, vf_ref, gf_ref, cf_ref),
                                                        (kb_ref, vb_ref, gb_ref, cb_ref))):
        g = g_ref[...]
        for p in range(N_PAIR):
            h0, h1 = 2 * p, 2 * p + 1
            wl = G_WL + 16 * d
            w = jnp.where(lo, jnp.exp(g[:, wl + h0:wl + h0 + 1]), jnp.exp(g[:, wl + h1:wl + h1 + 1]))
            kw = (k_ref[:, p * 2 * DK_A:(p + 1) * 2 * DK_A].astype(F32) * w).astype(BF16)
            vext = jnp.concatenate([v_ref[:, h0 * DV_A:(h0 + 1) * DV_A], ones_blk,
                                    v_ref[:, h1 * DV_A:(h1 + 1) * DV_A], ones_blk], axis=1)
            upd = lax.dot_general(kw, vext, (((0,), (0,)), ((), ())), preferred_element_type=F32)
            c_old = st[d, p]
            out_ref[p, 0] = c_old[:DK_A, :C_EXT].astype(out_ref.dtype)
            out_ref[p, 1] = c_old[DK_A:, C_EXT:].astype(out_ref.dtype)
            tl = G_TOT + 16 * d
            dec = jnp.where(top, jnp.exp(g[0:1, tl + h0:tl + h0 + 1]), jnp.exp(g[0:1, tl + h1:tl + h1 + 1]))
            st[d, p] = dec * c_old + jnp.where(diag, upd, 0.0)

    @pl.when(s == nc - 1)
    def _():
        cfin_ref[...] = st[...]


def _mlstm_states(proj, gc, c0, *, bn, seq_len):
    lc = min(CHUNK, seq_len)
    nc = seq_len // lc
    kcol = BLK_KA * PROJ_TN // (NH_A * DK_A)
    vcol = BLK_VA * PROJ_TN // (NH_A * DV_A)
    fwd = lambda b, s: b * nc + s
    bwd = lambda b, s: b * nc + nc - 1 - s
    st_shape = (2, N_PAIR, 2 * DK_A, 2 * C_EXT)
    out_blk = (None, None, N_PAIR, 2, DK_A, C_EXT)
    return pl.pallas_call(
        functools.partial(_mlstm_state_kernel, nc=nc),
        grid=(bn, nc),
        in_specs=[pl.BlockSpec((lc, NH_A * DK_A), lambda b, s: (fwd(b, s), kcol)),
                  pl.BlockSpec((lc, NH_A * DV_A), lambda b, s: (fwd(b, s), vcol)),
                  pl.BlockSpec((lc, LANES), lambda b, s: (fwd(b, s), 0)),
                  pl.BlockSpec((lc, NH_A * DK_A), lambda b, s: (bwd(b, s), kcol)),
                  pl.BlockSpec((lc, NH_A * DV_A), lambda b, s: (bwd(b, s), vcol)),
                  pl.BlockSpec((lc, LANES), lambda b, s: (bwd(b, s), 0)),
                  pl.BlockSpec((None,) + st_shape, lambda b, s: (b, 0, 0, 0, 0))],
        out_specs=[pl.BlockSpec(out_blk, lambda b, s: (b, s, 0, 0, 0, 0)),
                   pl.BlockSpec(out_blk, lambda b, s: (b, nc - 1 - s, 0, 0, 0, 0)),
                   pl.BlockSpec((None,) + st_shape, lambda b, s: (b, 0, 0, 0, 0))],
        out_shape=[jax.ShapeDtypeStruct((bn, nc, N_PAIR, 2, DK_A, C_EXT), BF16),
                   jax.ShapeDtypeStruct((bn, nc, N_PAIR, 2, DK_A, C_EXT), BF16),
                   jax.ShapeDtypeStruct((bn,) + st_shape, F32)],
        scratch_shapes=[pltpu.VMEM(st_shape, F32)],
        compiler_params=_cparams(("parallel", "arbitrary")),
        name="mlstm_states",
    )(proj, proj, gc, proj, proj, gc, c0)


def _mlstm_out_kernel(q_ref, k_ref, v_ref, gc_ref, gr_ref, cf_ref, cb_ref, h_ref):
    lc = q_ref.shape[0]
    gc = gc_ref[...]
    gr = gr_ref[...]
    t_idx = lax.broadcasted_iota(jnp.int32, (lc, lc), 0)
    s_idx = lax.broadcasted_iota(jnp.int32, (lc, lc), 1)
    visible = (s_idx <= t_idx, s_idx >= t_idx)
    lo = lax.broadcasted_iota(jnp.int32, (lc, 2 * DK_A), 1) < DK_A
    ones_blk = _ones_block(lc)
    zero_c = jnp.zeros((DK_A, C_EXT), BF16)
    nt = (((1,), (1,)), ((), ()))
    for p in range(N_PAIR):
        qp = q_ref[:, p * 2 * DK_A:(p + 1) * 2 * DK_A]
        kp = k_ref[:, p * 2 * DK_A:(p + 1) * 2 * DK_A]
        zq = jnp.zeros_like(qp)
        q_stack = jnp.concatenate([jnp.where(lo, qp, zq), jnp.where(lo, zq, qp)], axis=0)
        s_both = lax.dot_general(q_stack, kp, nt, preferred_element_type=F32)
        inter = []
        for d, c_ref in enumerate((cf_ref, cb_ref)):
            bl = G_B + 16 * d
            e = jnp.where(lo, jnp.exp(gc[:, bl + 2 * p:bl + 2 * p + 1]),
                          jnp.exp(gc[:, bl + 2 * p + 1:bl + 2 * p + 2]))
            qs = (qp.astype(F32) * e).astype(BF16)
            c_pair = jnp.concatenate([jnp.concatenate([c_ref[p, 0], zero_c], axis=1),
                                      jnp.concatenate([zero_c, c_ref[p, 1]], axis=1)], axis=0)
            inter.append(jnp.dot(qs, c_pair, preferred_element_type=F32))
        for hh in range(2):
            h = 2 * p + hh
            s_h = s_both[hh * lc:(hh + 1) * lc]
            vext = jnp.concatenate([v_ref[:, h * DV_A:(h + 1) * DV_A], ones_blk], axis=1)
            acc = None
            for d in range(2):
                b_col = gc[:, G_B + 16 * d + h:G_B + 16 * d + h + 1]
                a_row = gr[G_A + 16 * d + h:G_A + 16 * d + h + 1, :]
                log_d = jnp.where(visible[d], b_col + a_row, -jnp.inf)
                pm = (s_h * jnp.exp(log_d)).astype(BF16)
                nd = jnp.dot(pm, vext, preferred_element_type=F32) + inter[d][:, hh * C_EXT:(hh + 1) * C_EXT]
                hd = nd[:, :DV_A] / jnp.maximum(jnp.abs(nd[:, DV_A:DV_A + 1]), 1.0)
                acc = hd if acc is None else acc + hd
            h_ref[:, h * DV_A:(h + 1) * DV_A] = acc.astype(h_ref.dtype)


def _mlstm_outputs(proj, gc, gr, cf, cb, *, bn, seq_len):
    lc = min(CHUNK, seq_len)
    nc = seq_len // lc
    qcol = BLK_QA * PROJ_TN // (NH_A * DK_A)
    kcol = BLK_KA * PROJ_TN // (NH_A * DK_A)
    vcol = BLK_VA * PROJ_TN // (NH_A * DV_A)
    tok = lambda b, c: b * nc + c
    st_blk = (None, None, N_PAIR, 2, DK_A, C_EXT)
    st_map = lambda b, c: (b, c, 0, 0, 0, 0)
    return pl.pallas_call(
        _mlstm_out_kernel,
        grid=(bn, nc),
        in_specs=[pl.BlockSpec((lc, NH_A * DK_A), lambda b, c: (tok(b, c), qcol)),
                  pl.BlockSpec((lc, NH_A * DK_A), lambda b, c: (tok(b, c), kcol)),
                  pl.BlockSpec((lc, NH_A * DV_A), lambda b, c: (tok(b, c), vcol)),
                  pl.BlockSpec((lc, LANES), lambda b, c: (tok(b, c), 0)),
                  pl.BlockSpec((LANES, lc), lambda b, c: (0, tok(b, c))),
                  pl.BlockSpec(st_blk, st_map),
                  pl.BlockSpec(st_blk, st_map)],
        out_specs=pl.BlockSpec((lc, NH_A * DV_A), lambda b, c: (tok(b, c), 0)),
        out_shape=jax.ShapeDtypeStruct((bn * seq_len, NH_A * DV_A), F32),
        compiler_params=_cparams(("parallel", "parallel")),
        name="mlstm_outputs",
    )(proj, proj, proj, gc, gr, cf, cb)


NAT_QB = 2


def _natten_kernel(q_ref, k_ref, v_ref, kc_ref, vc_ref, ba_ref, bb_ref, o_ref, *, seq_len):
    i2 = pl.program_id(2)
    nq = QROWS * GRID_W
    nk = KROWS * GRID_W
    lctx = kc_ref.shape[0]
    nt = (((1,), (1,)), ((), ()))

    def halves(rows):
        lane = lax.broadcasted_iota(jnp.int32, (rows, 2 * DH_B), 1)
        return lane, lane < DH_B

    _, lo_q = halves(nq)
    lane_k, lo_k = halves(nk)
    lane_c, lo_c = halves(lctx)
    kc = kc_ref[...]
    vc = vc_ref[...]
    zq = jnp.zeros((nq, 2 * DH_B), BF16)
    for j, bias_ref in enumerate((ba_ref, bb_ref)):
        blk = NAT_QB * i2 + j
        start = jnp.clip(blk * nq - (WIN_R // 2) * GRID_W, 0, seq_len - nk)
        start = pl.multiple_of(start, GRID_W)
        q = q_ref[j * nq:(j + 1) * nq, :]
        kb = k_ref[pl.ds(start, nk), :]
        vb = v_ref[pl.ds(start, nk), :]
        outs = []
        for h in range(2):
            one_lane = DH_B if h == 0 else 0
            own_q, own_k, own_c = (lo_q, lo_k, lo_c) if h == 0 else (~lo_q, ~lo_k, ~lo_c)
            qh = jnp.where(own_q, q, zq)
            s_win = lax.dot_general(qh, kb, nt, preferred_element_type=F32) + bias_ref[h]
            s_ctx = lax.dot_general(qh, kc, nt, preferred_element_type=F32)
            m = jnp.maximum(jnp.max(s_win, axis=-1, keepdims=True), jnp.max(s_ctx, axis=-1, keepdims=True))
            p_win = jnp.exp2(s_win - m).astype(BF16)
            p_ctx = jnp.exp2(s_ctx - m).astype(BF16)
            vh = jnp.where(own_k, vb, (lane_k == one_lane).astype(BF16))
            vch = jnp.where(own_c, vc, (lane_c == one_lane).astype(BF16))
            o = (jnp.dot(p_win, vh, preferred_element_type=F32)
                 + jnp.dot(p_ctx, vch, preferred_element_type=F32))
            outs.append(o / o[:, one_lane:one_lane + 1])
        o_ref[j * nq:(j + 1) * nq, :] = jnp.where(lo_q, outs[0], outs[1]).astype(o_ref.dtype)


def _natten_patterns(rows):
    nblk = rows // QROWS
    pats = []
    for blk in (0, 1 if nblk > 2 else 0, nblk - 1):
        r0 = blk * QROWS
        k0 = int(np.clip(r0 - WIN_R // 2, 0, rows - KROWS))
        qr = r0 + np.arange(QROWS)
        kr = k0 + np.arange(KROWS)
        wr0 = np.clip(qr - WIN_R // 2, 0, rows - WIN_R)
        valid = (kr[None, :] >= wr0[:, None]) & (kr[None, :] < wr0[:, None] + WIN_R)
        dr = np.clip(kr[None, :] - qr[:, None] + WIN_R - 1, 0, 2 * WIN_R - 2)
        pats.append((valid, dr))
    return pats


def _bias_kernel(t_ref, o_ref, *, patterns):
    pat = pl.program_id(0)
    for ps, (valid, dr) in enumerate(patterns):
        @pl.when(pat == ps)
        def _():
            for xq in range(QROWS):
                for yk in range(KROWS):
                    if valid[xq, yk]:
                        blk = t_ref[int(dr[xq, yk])]
                    else:
                        blk = jnp.full((GRID_W, GRID_W), -jnp.inf, F32)
                    o_ref[xq * GRID_W:(xq + 1) * GRID_W, yk * GRID_W:(yk + 1) * GRID_W] = blk


def _natten_bias(rpb, rows):
    nh = rpb.shape[0]
    qc = np.arange(GRID_W)
    kc = np.arange(GRID_W)
    wc0 = np.clip(qc - WIN_C // 2, 0, GRID_W - WIN_C)
    in_c = (kc[None, :] >= wc0[:, None]) & (kc[None, :] < wc0[:, None] + WIN_C)
    dc = np.clip(kc[None, :] - qc[:, None] + WIN_C - 1, 0, 2 * WIN_C - 2)
    sel_c = jnp.asarray(dc[:, :, None] == np.arange(2 * WIN_C - 1), F32)
    tcol = jnp.einsum('hab,uvb->hauv', rpb.astype(F32) * LOG2E, sel_c, precision=lax.Precision.HIGHEST)
    tcol = jnp.where(in_c[None, None], tcol, -jnp.inf)
    nq, nk = QROWS * GRID_W, KROWS * GRID_W
    n_dr = 2 * WIN_R - 1
    return pl.pallas_call(
        functools.partial(_bias_kernel, patterns=_natten_patterns(rows)),
        grid=(3, nh),
        in_specs=[pl.BlockSpec((None, n_dr, GRID_W, GRID_W), lambda p, h: (h, 0, 0, 0))],
        out_specs=pl.BlockSpec((None, None, nq, nk), lambda p, h: (p, h, 0, 0)),
        out_shape=jax.ShapeDtypeStruct((3, nh, nq, nk), F32),
        compiler_params=_cparams(("arbitrary", "arbitrary")),
        name="natten_bias",
    )(tcol)


def _natten(px, pc, bias, *, bn, seq_len, lctx):
    nq = QROWS * GRID_W
    nk = KROWS * GRID_W
    nstep = seq_len // (NAT_QB * nq)
    pw = 2 * DH_B
    n_pair = NH_B // 2
    qcol, kcol, vcol = (blk * PROJ_TN // pw for blk in (BLK_QB, BLK_KB, BLK_VB))
    last = nstep - 1
    first_pat = lambda i: jnp.where(i == 0, 0, 1)
    second_pat = lambda i: jnp.where(i == last, 2, 1)
    return pl.pallas_call(
        functools.partial(_natten_kernel, seq_len=seq_len),
        grid=(bn, n_pair, nstep),
        in_specs=[pl.BlockSpec((NAT_QB * nq, pw), lambda b, p, i: (b * nstep + i, qcol + p)),
                  pl.BlockSpec((seq_len, pw), lambda b, p, i: (b, kcol + p)),
                  pl.BlockSpec((seq_len, pw), lambda b, p, i: (b, vcol + p)),
                  pl.BlockSpec((lctx, pw), lambda b, p, i: (b, kcol + p)),
                  pl.BlockSpec((lctx, pw), lambda b, p, i: (b, vcol + p)),
                  pl.BlockSpec((None, 2, nq, nk), lambda b, p, i: (first_pat(i), p, 0, 0)),
                  pl.BlockSpec((None, 2, nq, nk), lambda b, p, i: (second_pat(i), p, 0, 0))],
        out_specs=pl.BlockSpec((NAT_QB * nq, pw), lambda b, p, i: (b * nstep + i, p)),
        out_shape=jax.ShapeDtypeStruct((bn * seq_len, NH_B * DH_B), BF16),
        compiler_params=_cparams(("parallel", "parallel", "arbitrary")),
        name="natten",
    )(px, px, px, pc, pc, bias, bias)


def _merge_kernel(h_ref, oa_ref, na_ref, mg_ref, x_ref, mod_ref, gn_ref, wa_ref, wb_ref, wo_ref, o_ref):
    h = h_ref[...]
    parts = []
    for hd in range(NH_A):
        hh = h[:, hd * DV_A:(hd + 1) * DV_A]
        ms = jnp.mean(hh * hh, axis=-1, keepdims=True)
        parts.append(hh * lax.rsqrt(ms + EPS))
    ha = jnp.concatenate(parts, axis=1) * gn_ref[...]
    ha = (ha * jax.nn.sigmoid(oa_ref[...].astype(F32))).astype(BF16)
    d = wa_ref.shape[1]
    gates = jax.nn.sigmoid(mg_ref[...].astype(F32))
    t = (gates[:, :d] * jnp.dot(ha, wa_ref[...], preferred_element_type=F32)
         + gates[:, d:] * jnp.dot(na_ref[...], wb_ref[...], preferred_element_type=F32))
    mix = jnp.dot(t.astype(BF16), wo_ref[...], preferred_element_type=F32)
    g1 = mod_ref[2:3, :]
    o_ref[...] = x_ref[...] + g1 * mix


def _merge(h, proj, na, x, mod3, gn, wa, wb, wo, tm=512):
    bn, S, d = x.shape
    tm = min(tm, S)
    nblk = S // tm
    d_a = h.shape[-1]
    d_b = na.shape[-1]
    oa_blk = BLK_OA * PROJ_TN // d_a
    mg_blk = BLK_MG * PROJ_TN // (2 * d)
    const = lambda b, i: (0, 0)
    return pl.pallas_call(
        _merge_kernel,
        grid=(bn, nblk),
        in_specs=[pl.BlockSpec((tm, d_a), lambda b, i: (b * nblk + i, 0)),
                  pl.BlockSpec((tm, d_a), lambda b, i: (b * nblk + i, oa_blk)),
                  pl.BlockSpec((tm, d_b), lambda b, i: (b * nblk + i, 0)),
                  pl.BlockSpec((tm, 2 * d), lambda b, i: (b * nblk + i, mg_blk)),
                  pl.BlockSpec((None, tm, d), lambda b, i: (b, i, 0)),
                  pl.BlockSpec((None, N_MOD, d), lambda b, i: (b, 0, 0)),
                  pl.BlockSpec((1, d_a), const),
                  pl.BlockSpec((d_a, d), const, pipeline_mode=pl.Buffered(1)),
                  pl.BlockSpec((d_b, d), const, pipeline_mode=pl.Buffered(1)),
                  pl.BlockSpec((d, d), const, pipeline_mode=pl.Buffered(1))],
        out_specs=pl.BlockSpec((None, tm, d), lambda b, i: (b, i, 0)),
        out_shape=jax.ShapeDtypeStruct((bn, S, d), F32),
        compiler_params=_cparams(("parallel", "parallel")),
        name="merge",
    )(h, proj, na, proj, x, mod3, gn, wa, wb, wo)


def _ffn_kernel(x_ref, mod_ref, g_ref, wg_ref, wu_ref, wd_ref, o_ref):
    x = x_ref[...]
    hx = _mod_norm(x, g_ref[...], mod_ref[4:5, :], mod_ref[3:4, :]).astype(BF16)
    a = jnp.dot(hx, wg_ref[...], preferred_element_type=F32)
    u = jnp.dot(hx, wu_ref[...], preferred_element_type=F32)
    act = (a * jax.nn.sigmoid(a) * u).astype(BF16)
    f = jnp.dot(act, wd_ref[...], preferred_element_type=F32)
    o_ref[...] = x + mod_ref[5:6, :] * f


def _ffn(x, mod3, g, wg, wu, wd, tm=512):
    bn, S, d = x.shape
    tm = min(tm, S)
    dff = wg.shape[1]
    const = lambda b, i: (0, 0)
    return pl.pallas_call(
        _ffn_kernel,
        grid=(bn, S // tm),
        in_specs=[pl.BlockSpec((None, tm, d), lambda b, i: (b, i, 0)),
                  pl.BlockSpec((None, N_MOD, d), lambda b, i: (b, 0, 0)),
                  pl.BlockSpec((1, d), const),
                  pl.BlockSpec((d, dff), const, pipeline_mode=pl.Buffered(1)),
                  pl.BlockSpec((d, dff), const, pipeline_mode=pl.Buffered(1)),
                  pl.BlockSpec((dff, d), const, pipeline_mode=pl.Buffered(1))],
        out_specs=pl.BlockSpec((None, tm, d), lambda b, i: (b, i, 0)),
        out_shape=jax.ShapeDtypeStruct((bn, S, d), F32),
        compiler_params=_cparams(("parallel", "parallel")),
        name="ffn",
    )(x, mod3, g.reshape(1, d), wg, wu, wd)


def _rope_tables(S):
    t = np.arange(S)
    row, col = t // GRID_W, t % GRID_W
    half = DK_A // 4
    inv = ROPE_THETA ** (-np.arange(half, dtype=np.float64) / half)
    ang_r = row[:, None] * inv[None, :]
    ang_c = col[:, None] * inv[None, :]
    cos = np.concatenate([np.cos(ang_r)] * 2 + [np.cos(ang_c)] * 2, axis=1)
    sin = np.concatenate([-np.sin(ang_r), np.sin(ang_r), -np.sin(ang_c), np.sin(ang_c)], axis=1)
    reps = LANES // DK_A
    return (jnp.asarray(np.tile(cos, (1, reps)), F32), jnp.asarray(np.tile(sin, (1, reps)), F32))


def kernel(x, c, ctx, c_ctx, w_mod, b_mod, norm1_g, w_in, b_gates, mlstm_norm_g, qn_g, kn_g, rpb,
           w_branch_a, w_branch_b, w_out, norm2_g, w_ffn_gate, w_ffn_up, w_ffn_down):
    bn, S, d = x.shape
    lctx = ctx.shape[1]
    rows = S // GRID_W
    depth = w_mod.shape[0]
    d_a, d_b = NH_A * DV_A, NH_B * DH_B
    dqk = NH_A * DK_A
    sizes = (dqk, dqk, d_a, d_a, 4 * NH_A, d_b, d_b, d_b, 2 * d)
    offs = np.cumsum((0,) + sizes)
    cos_t, sin_t = _rope_tables(S)
    eye = np.kron(np.eye(PROJ_TN // DH_B), np.ones((DH_B, DH_B)))
    gsum = jnp.asarray(eye, BF16)

    for l in range(depth):
        assert l == depth - 1, "context-stream update for non-final layers is not implemented"
        n_rows = -(-(bn + 1) // 8) * 8
        cc = jnp.concatenate([c, c_ctx[None, :], jnp.zeros((n_rows - bn - 1, d), F32)], axis=0)
        mod3 = _modulation(cc, w_mod[l], b_mod[l]).reshape(n_rows, N_MOD, d)

        w = w_in[l]
        seg = lambda i: w[:, int(offs[i]):int(offs[i + 1])]
        w_main = jnp.concatenate([seg(0), seg(1), seg(2), seg(3), seg(5), seg(6), seg(7), seg(8)],
                                 axis=1).astype(BF16)
        w_g = jnp.pad(seg(4), ((0, 0), (0, LANES - 4 * NH_A))).astype(BF16)
        b_g = jnp.pad(b_gates[l].reshape(1, 4 * NH_A), ((0, 0), (0, LANES - 4 * NH_A)))
        qk_gain = jnp.stack([jnp.tile(qn_g[l], PROJ_TN // DH_B) * (DH_B ** -0.5 * LOG2E),
                             jnp.tile(kn_g[l], PROJ_TN // DH_B)])

        px, gcx, grx = _mixer_in(x, mod3, norm1_g[l], w_main, w_g, b_g, qk_gain, cos_t, sin_t, gsum,
                                 rope=True)
        pc, gcc, _ = _mixer_in(ctx, mod3, norm1_g[l], w_main, w_g, b_g, qk_gain, cos_t, sin_t, gsum,
                               rope=False, ctx_row=bn)

        zero_state = jnp.zeros((bn, 2, N_PAIR, 2 * DK_A, 2 * C_EXT), F32)
        _, _, st_ctx = _mlstm_states(pc, gcc, zero_state, bn=bn, seq_len=lctx)
        cf, cb, _ = _mlstm_states(px, gcx, st_ctx, bn=bn, seq_len=S)
        h_a = _mlstm_outputs(px, gcx, grx, cf, cb, bn=bn, seq_len=S)

        bias = _natten_bias(rpb[l], rows)
        na = _natten(px, pc, bias, bn=bn, seq_len=S, lctx=lctx)

        x_mid = _merge(h_a, px, na, x, mod3, mlstm_norm_g[l].reshape(1, d_a),
                       w_branch_a[l].astype(BF16), w_branch_b[l].astype(BF16), w_out[l].astype(BF16))
        x = _ffn(x_mid, mod3, norm2_g[l], w_ffn_gate[l].astype(BF16), w_ffn_up[l].astype(BF16),
                 w_ffn_down[l].astype(BF16))
    return x
```

```python
import functools

import numpy as np
import jax
import jax.numpy as jnp
from jax import lax
from jax.experimental import pallas as pl
from jax.experimental.pallas import tpu as pltpu

F32 = jnp.float32
BF16 = jnp.bfloat16

GRID_W = 64
NH_A, DK_A, DV_A = 8, 64, 128
NH_B, DH_B = 16, 64
CHUNK = 128
GATE_CAP = 15.0
WIN_R, WIN_C = 8, 16
ROPE_THETA = 10000.0
EPS = 1e-6
N_MOD = 6
LANES = 128
MXU_TILE = 256
VMEM_LIMIT = 56 * 1024 * 1024

QROWS = 4
KROWS = QROWS + WIN_R - 1

N_PAIR = NH_A // 2
C_EXT = 2 * DV_A

G_A, G_WL, G_B, G_TOT = 0, 32, 72, 104


def _cparams(sem):
    return pltpu.CompilerParams(dimension_semantics=sem, vmem_limit_bytes=VMEM_LIMIT)


def _mod_kernel(c_ref, w_ref, b_ref, o_ref):
    c = c_ref[...]
    h = c * jax.nn.sigmoid(c)
    o_ref[...] = jnp.dot(h, w_ref[...], preferred_element_type=F32,
                         precision=lax.Precision.HIGHEST) + b_ref[...]


def _modulation(cc, w_mod, b_mod):
    rows, d = cc.shape
    n = w_mod.shape[1]
    tn = 1024
    return pl.pallas_call(
        _mod_kernel,
        grid=(n // tn,),
        in_specs=[pl.BlockSpec((rows, d), lambda j: (0, 0)),
                  pl.BlockSpec((d, tn), lambda j: (0, j)),
                  pl.BlockSpec((1, tn), lambda j: (0, j))],
        out_specs=pl.BlockSpec((rows, tn), lambda j: (0, j)),
        out_shape=jax.ShapeDtypeStruct((rows, n), F32),
        compiler_params=_cparams(("arbitrary",)),
        name="modulation",
    )(cc, w_mod, b_mod.reshape(1, n))


def _mod_norm(x, gain, scale, shift):
    ms = jnp.mean(x * x, axis=-1, keepdims=True)
    y = x * lax.rsqrt(ms + EPS) * gain
    return y * (1.0 + scale) + shift


def _gate_pack(g, gc_ref, gr_ref):
    tm = g.shape[0]
    g = GATE_CAP * jnp.tanh(g * (1.0 / GATE_CAP))
    lane_t = lax.broadcasted_iota(jnp.int32, g.shape, 1)
    is_forget = (lane_t & NH_A) != 0
    x = jnp.where(is_forget, jax.nn.log_sigmoid(g), g)
    x = jnp.where(lane_t < 4 * NH_A, x, 0.0)

    t_idx = lax.broadcasted_iota(jnp.int32, (CHUNK, CHUNK), 0)
    s_idx = lax.broadcasted_iota(jnp.int32, (CHUNK, CHUNK), 1)
    tri_f = (s_idx <= t_idx).astype(F32)
    tri_b = (s_idx >= t_idx).astype(F32)
    lane = s_idx
    fwd_half = lane < 2 * NH_A
    low = lane < 4 * NH_A
    hi = lax.Precision.HIGHEST
    for ci in range(tm // CHUNK):
        xc = x[ci * CHUNK:(ci + 1) * CHUNK]
        cf = jnp.dot(tri_f, xc, preferred_element_type=F32, precision=hi)
        cb = jnp.dot(tri_b, xc, preferred_element_type=F32, precision=hi)
        tot = jnp.broadcast_to(jnp.sum(xc, axis=0, keepdims=True), xc.shape)
        r_b = jnp.where(fwd_half, pltpu.roll(cf, LANES - NH_A, 1), pltpu.roll(cb, LANES - NH_A, 1))
        a = xc - r_b
        wl = xc + (pltpu.roll(tot, LANES - NH_A, 1) - r_b)
        bsel = jnp.where(fwd_half, cf, cb)
        z = lambda v: jnp.where(low, v, 0.0)
        packed = z(a) + pltpu.roll(z(wl), 32, 1) + pltpu.roll(z(bsel), 64, 1) + pltpu.roll(z(tot), 96, 1)
        gc_ref[ci * CHUNK:(ci + 1) * CHUNK, :] = packed
        gr_ref[:, ci * CHUNK:(ci + 1) * CHUNK] = packed.T


PROJ_TN = 512
BLK_QA, BLK_KA, BLK_VA, BLK_OA, BLK_QB, BLK_KB, BLK_VB, BLK_MG = 0, 1, 2, 4, 6, 8, 10, 12
N_PROJ_BLK = 16
LOG2E = 1.4426950408889634


def _swap16(x):
    n = x.shape[-1]
    lane = lax.broadcasted_iota(jnp.int32, x.shape, x.ndim - 1)
    first = (lane & 31) < 16
    return jnp.where(first, pltpu.roll(x, n - 16, x.ndim - 1), pltpu.roll(x, 16, x.ndim - 1))


def _mixer_in_kernel(x_ref, mod_ref, g_ref, w_ref, wg_ref, bg_ref, qk_ref, cos_ref, sin_ref, gsum_ref,
                     o_ref, gc_ref, gr_ref, *, rope):
    hx = _mod_norm(x_ref[...], g_ref[...], mod_ref[1:2, :], mod_ref[0:1, :]).astype(BF16)
    if rope:
        reps = PROJ_TN // LANES
        cos = jnp.concatenate([cos_ref[...]] * reps, axis=1)
        sin = jnp.concatenate([sin_ref[...]] * reps, axis=1)
    for blk in range(N_PROJ_BLK):
        cols = slice(blk * PROJ_TN, (blk + 1) * PROJ_TN)
        acc = jnp.dot(hx, w_ref[:, cols], preferred_element_type=F32)
        if blk < BLK_VA:
            if rope:
                acc = acc * cos + _swap16(acc) * sin
            if blk == BLK_QA:
                acc = acc * DK_A ** -0.5
        elif BLK_QB <= blk < BLK_VB:
            sq = (acc * acc).astype(BF16)
            gw = gsum_ref.shape[0]
            ss = jnp.concatenate([jnp.dot(sq[:, c0:c0 + gw], gsum_ref[...], preferred_element_type=F32)
                                  for c0 in range(0, PROJ_TN, gw)], axis=1)
            gain = qk_ref[0:1, :] if blk < BLK_KB else qk_ref[1:2, :]
            acc = acc * lax.rsqrt(ss * (1.0 / DH_B) + EPS) * gain
        o_ref[:, cols] = acc.astype(o_ref.dtype)
    _gate_pack(jnp.dot(hx, wg_ref[...], preferred_element_type=F32) + bg_ref[...], gc_ref, gr_ref)


def _mixer_in(x, mod3, g, w, w_g, b_g, qk_gain, cos, sin, gsum, *, rope, ctx_row=None, tm=512):
    bn, L, d = x.shape
    tm = min(tm, L)
    nblk = L // tm
    n = w.shape[1]
    mod_map = (lambda b, i: (b, 0, 0)) if ctx_row is None else (lambda b, i: (ctx_row, 0, 0))
    const = lambda b, i: (0, 0)
    tok = lambda b, i: (b * nblk + i, 0)
    resident = pl.Buffered(1)
    return pl.pallas_call(
        functools.partial(_mixer_in_kernel, rope=rope),
        grid=(bn, nblk),
        in_specs=[pl.BlockSpec((None, tm, d), lambda b, i: (b, i, 0)),
                  pl.BlockSpec((None, N_MOD, d), mod_map),
                  pl.BlockSpec((1, d), const),
                  pl.BlockSpec((d, n), const, pipeline_mode=resident),
                  pl.BlockSpec((d, LANES), const, pipeline_mode=resident),
                  pl.BlockSpec((1, LANES), const),
                  pl.BlockSpec((2, PROJ_TN), const),
                  pl.BlockSpec((tm, LANES), lambda b, i: (i, 0)),
                  pl.BlockSpec((tm, LANES), lambda b, i: (i, 0)),
                  pl.BlockSpec(gsum.shape, const, pipeline_mode=resident)],
        out_specs=[pl.BlockSpec((tm, n), tok),
                   pl.BlockSpec((tm, LANES), tok),
                   pl.BlockSpec((LANES, tm), lambda b, i: (0, b * nblk + i))],
        out_shape=[jax.ShapeDtypeStruct((bn * L, n), BF16),
                   jax.ShapeDtypeStruct((bn * L, LANES), F32),
                   jax.ShapeDtypeStruct((LANES, bn * L), F32)],
        compiler_params=_cparams(("parallel", "parallel")),
        name="mixer_in",
    )(x, mod3, g.reshape(1, d), w, w_g, b_g, qk_gain, cos, sin, gsum)


def _mlstm_state_kernel(kf_ref, vf_ref, gf_ref, kb_ref, vb_ref, gb_ref, c0_ref,
                        cf_ref, cb_ref, cfin_ref, st, *, nc):
    s = pl.program_id(1)

    @pl.when(s == 0)
    def _():
        st[...] = c0_ref[...]

    lc = kf_ref.shape[0]
    lo = lax.broadcasted_iota(jnp.int32, (lc, 2 * DK_A), 1) < DK_A
    top = lax.broadcasted_iota(jnp.int32, (2 * DK_A, C_EXT), 0) < DK_A
    ones_blk = jnp.ones((lc, DV_A), BF16)
    tn = (((0,), (0,)), ((), ()))
    for d, (k_ref, v_ref, g_ref, out_ref) in enumerate(((kf_ref, vf_ref, gf_ref, cf_ref),
                                                        (kb_ref, vb_ref, gb_ref, cb_ref))):
        g = g_ref[...]
        for p in range(N_PAIR):
            h0, h1 = 2 * p, 2 * p + 1
            wl = G_WL + 16 * d
            w0 = jnp.exp(g[:, wl + h0:wl + h0 + 1])
            w1 = jnp.exp(g[:, wl + h1:wl + h1 + 1])
            kf32 = k_ref[:, p * 2 * DK_A:(p + 1) * 2 * DK_A].astype(F32)
            kw0 = jnp.where(lo, kf32 * w0, 0.0).astype(BF16)
            kw1 = jnp.where(lo, 0.0, kf32 * w1).astype(BF16)
            vext0 = jnp.concatenate([v_ref[:, h0 * DV_A:(h0 + 1) * DV_A], ones_blk], axis=1)
            vext1 = jnp.concatenate([v_ref[:, h1 * DV_A:(h1 + 1) * DV_A], ones_blk], axis=1)
            upd = (lax.dot_general(kw0, vext0, tn, preferred_element_type=F32)
                   + lax.dot_general(kw1, vext1, tn, preferred_element_type=F32))
            c_old = st[d, p]
            out_ref[p, 0] = c_old[:DK_A].astype(out_ref.dtype)
            out_ref[p, 1] = c_old[DK_A:].astype(out_ref.dtype)
            tl = G_TOT + 16 * d
            dec = jnp.where(top, jnp.exp(g[0:1, tl + h0:tl + h0 + 1]), jnp.exp(g[0:1, tl + h1:tl + h1 + 1]))
            st[d, p] = dec * c_old + upd

    @pl.when(s == nc - 1)
    def _():
        cfin_ref[...] = st[...]


def _mlstm_states(proj, gc, c0, *, bn, seq_len):
    lc = min(CHUNK, seq_len)
    nc = seq_len // lc
    kcol = BLK_KA * PROJ_TN // (NH_A * DK_A)
    vcol = BLK_VA * PROJ_TN // (NH_A * DV_A)
    fwd = lambda b, s: b * nc + s
    bwd = lambda b, s: b * nc + nc - 1 - s
    st_shape = (2, N_PAIR, 2 * DK_A, C_EXT)
    out_blk = (None, None, N_PAIR, 2, DK_A, C_EXT)
    return pl.pallas_call(
        functools.partial(_mlstm_state_kernel, nc=nc),
        grid=(bn, nc),
        in_specs=[pl.BlockSpec((lc, NH_A * DK_A), lambda b, s: (fwd(b, s), kcol)),
                  pl.BlockSpec((lc, NH_A * DV_A), lambda b, s: (fwd(b, s), vcol)),
                  pl.BlockSpec((lc, LANES), lambda b, s: (fwd(b, s), 0)),
                  pl.BlockSpec((lc, NH_A * DK_A), lambda b, s: (bwd(b, s), kcol)),
                  pl.BlockSpec((lc, NH_A * DV_A), lambda b, s: (bwd(b, s), vcol)),
                  pl.BlockSpec((lc, LANES), lambda b, s: (bwd(b, s), 0)),
                  pl.BlockSpec((None,) + st_shape, lambda b, s: (b, 0, 0, 0, 0))],
        out_specs=[pl.BlockSpec(out_blk, lambda b, s: (b, s, 0, 0, 0, 0)),
                   pl.BlockSpec(out_blk, lambda b, s: (b, nc - 1 - s, 0, 0, 0, 0)),
                   pl.BlockSpec((None,) + st_shape, lambda b, s: (b, 0, 0, 0, 0))],
        out_shape=[jax.ShapeDtypeStruct((bn, nc, N_PAIR, 2, DK_A, C_EXT), BF16),
                   jax.ShapeDtypeStruct((bn, nc, N_PAIR, 2, DK_A, C_EXT), BF16),
                   jax.ShapeDtypeStruct((bn,) + st_shape, F32)],
        scratch_shapes=[pltpu.VMEM(st_shape, F32)],
        compiler_params=_cparams(("parallel", "arbitrary")),
        name="mlstm_states",
    )(proj, proj, gc, proj, proj, gc, c0)


def _mlstm_out_kernel(q_ref, k_ref, v_ref, gc_ref, gr_ref, cf_ref, cb_ref, h_ref):
    lc = q_ref.shape[0]
    gc = gc_ref[...]
    gr = gr_ref[...]
    t_idx = lax.broadcasted_iota(jnp.int32, (lc, lc), 0)
    s_idx = lax.broadcasted_iota(jnp.int32, (lc, lc), 1)
    visible = (s_idx <= t_idx, s_idx >= t_idx)
    lo = lax.broadcasted_iota(jnp.int32, (lc, 2 * DK_A), 1) < DK_A
    ones_blk = jnp.ones((lc, DV_A), BF16)
    zero_c = jnp.zeros((DK_A, C_EXT), BF16)
    nt = (((1,), (1,)), ((), ()))
    for p in range(N_PAIR):
        qp = q_ref[:, p * 2 * DK_A:(p + 1) * 2 * DK_A]
        kp = k_ref[:, p * 2 * DK_A:(p + 1) * 2 * DK_A]
        zq = jnp.zeros_like(qp)
        q_stack = jnp.concatenate([jnp.where(lo, qp, zq), jnp.where(lo, zq, qp)], axis=0)
        s_both = lax.dot_general(q_stack, kp, nt, preferred_element_type=F32)
        b_rep = [[jnp.broadcast_to(gc[:, G_B + 16 * d + 2 * p + hh:G_B + 16 * d + 2 * p + hh + 1], (lc, lc))
                  for hh in range(2)] for d in range(2)]
        inter = []
        for d, c_ref in enumerate((cf_ref, cb_ref)):
            e = jnp.where(lo, jnp.exp(b_rep[d][0]), jnp.exp(b_rep[d][1]))
            qs = (qp.astype(F32) * e).astype(BF16)
            c_pair = jnp.concatenate([jnp.concatenate([c_ref[p, 0], zero_c], axis=1),
                                      jnp.concatenate([zero_c, c_ref[p, 1]], axis=1)], axis=0)
            inter.append(jnp.dot(qs, c_pair, preferred_element_type=F32))
        for hh in range(2):
            h = 2 * p + hh
            s_h = s_both[hh * lc:(hh + 1) * lc]
            vext = jnp.concatenate([v_ref[:, h * DV_A:(h + 1) * DV_A], ones_blk], axis=1)
            acc = None
            for d in range(2):
                a_row = gr[G_A + 16 * d + h:G_A + 16 * d + h + 1, :]
                log_d = jnp.where(visible[d], b_rep[d][hh] + a_row, -jnp.inf)
                pm = (s_h * jnp.exp(log_d)).astype(BF16)
                nd = jnp.dot(pm, vext, preferred_element_type=F32) + inter[d][:, hh * C_EXT:(hh + 1) * C_EXT]
                hd = nd[:, :DV_A] / jnp.maximum(jnp.abs(nd[:, DV_A:]), 1.0)
                acc = hd if acc is None else acc + hd
            h_ref[:, h * DV_A:(h + 1) * DV_A] = acc.astype(h_ref.dtype)


def _mlstm_outputs(proj, gc, gr, cf, cb, *, bn, seq_len):
    lc = min(CHUNK, seq_len)
    nc = seq_len // lc
    qcol = BLK_QA * PROJ_TN // (NH_A * DK_A)
    kcol = BLK_KA * PROJ_TN // (NH_A * DK_A)
    vcol = BLK_VA * PROJ_TN // (NH_A * DV_A)
    tok = lambda b, c: b * nc + c
    st_blk = (None, None, N_PAIR, 2, DK_A, C_EXT)
    st_map = lambda b, c: (b, c, 0, 0, 0, 0)
    return pl.pallas_call(
        _mlstm_out_kernel,
        grid=(bn, nc),
        in_specs=[pl.BlockSpec((lc, NH_A * DK_A), lambda b, c: (tok(b, c), qcol)),
                  pl.BlockSpec((lc, NH_A * DK_A), lambda b, c: (tok(b, c), kcol)),
                  pl.BlockSpec((lc, NH_A * DV_A), lambda b, c: (tok(b, c), vcol)),
                  pl.BlockSpec((lc, LANES), lambda b, c: (tok(b, c), 0)),
                  pl.BlockSpec((LANES, lc), lambda b, c: (0, tok(b, c))),
                  pl.BlockSpec(st_blk, st_map),
                  pl.BlockSpec(st_blk, st_map)],
        out_specs=pl.BlockSpec((lc, NH_A * DV_A), lambda b, c: (tok(b, c), 0)),
        out_shape=jax.ShapeDtypeStruct((bn * seq_len, NH_A * DV_A), F32),
        compiler_params=_cparams(("parallel", "parallel")),
        name="mlstm_outputs",
    )(proj, proj, proj, gc, gr, cf, cb)


NAT_QB = 4


def _natten_kernel(q_ref, k_ref, v_ref, kc_ref, vc_ref, ba_ref, bm_ref, bb_ref, o_ref, *, seq_len):
    i2 = pl.program_id(2)
    nq = QROWS * GRID_W
    nk = KROWS * GRID_W
    lctx = kc_ref.shape[0]
    nt = (((1,), (1,)), ((), ()))

    def halves(rows):
        lane = lax.broadcasted_iota(jnp.int32, (rows, 2 * DH_B), 1)
        return lane, lane < DH_B

    _, lo_q = halves(nq)
    lane_k, lo_k = halves(nk)
    lane_c, lo_c = halves(lctx)
    kc = kc_ref[...]
    vc = vc_ref[...]
    zq = jnp.zeros((nq, 2 * DH_B), BF16)
    vch_pair = [jnp.where(lo_c if h == 0 else ~lo_c, vc, (lane_c == (DH_B if h == 0 else 0)).astype(BF16))
                for h in range(2)]
    bias_refs = (ba_ref,) + (bm_ref,) * (NAT_QB - 2) + (bb_ref,)
    for j, bias_ref in enumerate(bias_refs):
        blk = NAT_QB * i2 + j
        start = jnp.clip(blk * nq - (WIN_R // 2) * GRID_W, 0, seq_len - nk)
        start = pl.multiple_of(start, GRID_W)
        q = q_ref[j * nq:(j + 1) * nq, :]
        kb = k_ref[pl.ds(start, nk), :]
        vb = v_ref[pl.ds(start, nk), :]
        outs = []
        for h in range(2):
            one_lane = DH_B if h == 0 else 0
            own_q, own_k = (lo_q, lo_k) if h == 0 else (~lo_q, ~lo_k)
            qh = jnp.where(own_q, q, zq)
            s_win = lax.dot_general(qh, kb, nt, preferred_element_type=F32) + bias_ref[h]
            s_ctx = lax.dot_general(qh, kc, nt, preferred_element_type=F32)
            m = jnp.maximum(jnp.max(s_win, axis=-1, keepdims=True), jnp.max(s_ctx, axis=-1, keepdims=True))
            p_win = jnp.exp2(s_win - m).astype(BF16)
            p_ctx = jnp.exp2(s_ctx - m).astype(BF16)
            vh = jnp.where(own_k, vb, (lane_k == one_lane).astype(BF16))
            o = (jnp.dot(p_win, vh, preferred_element_type=F32)
                 + jnp.dot(p_ctx, vch_pair[h], preferred_element_type=F32))
            outs.append(o / o[:, one_lane:one_lane + 1])
        o_ref[j * nq:(j + 1) * nq, :] = jnp.where(lo_q, outs[0], outs[1]).astype(o_ref.dtype)


def _natten_patterns(rows):
    nblk = rows // QROWS
    pats = []
    for blk in (0, 1 if nblk > 2 else 0, nblk - 1):
        r0 = blk * QROWS
        k0 = int(np.clip(r0 - WIN_R // 2, 0, rows - KROWS))
        qr = r0 + np.arange(QROWS)
        kr = k0 + np.arange(KROWS)
        wr0 = np.clip(qr - WIN_R // 2, 0, rows - WIN_R)
        valid = (kr[None, :] >= wr0[:, None]) & (kr[None, :] < wr0[:, None] + WIN_R)
        dr = np.clip(kr[None, :] - qr[:, None] + WIN_R - 1, 0, 2 * WIN_R - 2)
        pats.append((valid, dr))
    return pats


def _bias_kernel(t_ref, o_ref, *, patterns):
    pat = pl.program_id(0)
    for ps, (valid, dr) in enumerate(patterns):
        @pl.when(pat == ps)
        def _():
            for xq in range(QROWS):
                for yk in range(KROWS):
                    if valid[xq, yk]:
                        blk = t_ref[int(dr[xq, yk])]
                    else:
                        blk = jnp.full((GRID_W, GRID_W), -jnp.inf, F32)
                    o_ref[xq * GRID_W:(xq + 1) * GRID_W, yk * GRID_W:(yk + 1) * GRID_W] = blk


def _natten_bias(rpb, rows):
    nh = rpb.shape[0]
    qc = np.arange(GRID_W)
    kc = np.arange(GRID_W)
    wc0 = np.clip(qc - WIN_C // 2, 0, GRID_W - WIN_C)
    in_c = (kc[None, :] >= wc0[:, None]) & (kc[None, :] < wc0[:, None] + WIN_C)
    dc = np.clip(kc[None, :] - qc[:, None] + WIN_C - 1, 0, 2 * WIN_C - 2)
    sel_c = jnp.asarray(dc[:, :, None] == np.arange(2 * WIN_C - 1), F32)
    tcol = jnp.einsum('hab,uvb->hauv', rpb.astype(F32) * LOG2E, sel_c, precision=lax.Precision.HIGHEST)
    tcol = jnp.where(in_c[None, None], tcol, -jnp.inf)
    nq, nk = QROWS * GRID_W, KROWS * GRID_W
    n_dr = 2 * WIN_R - 1
    return pl.pallas_call(
        functools.partial(_bias_kernel, patterns=_natten_patterns(rows)),
        grid=(3, nh),
        in_specs=[pl.BlockSpec((None, n_dr, GRID_W, GRID_W), lambda p, h: (h, 0, 0, 0))],
        out_specs=pl.BlockSpec((None, None, nq, nk), lambda p, h: (p, h, 0, 0)),
        out_shape=jax.ShapeDtypeStruct((3, nh, nq, nk), F32),
        compiler_params=_cparams(("arbitrary", "arbitrary")),
        name="natten_bias",
    )(tcol)


def _natten(px, pc, bias, *, bn, seq_len, lctx):
    nq = QROWS * GRID_W
    nk = KROWS * GRID_W
    nstep = seq_len // (NAT_QB * nq)
    pw = 2 * DH_B
    n_pair = NH_B // 2
    qcol, kcol, vcol = (blk * PROJ_TN // pw for blk in (BLK_QB, BLK_KB, BLK_VB))
    last = nstep - 1
    first_pat = lambda i: jnp.where(i == 0, 0, 1)
    second_pat = lambda i: jnp.where(i == last, 2, 1)
    return pl.pallas_call(
        functools.partial(_natten_kernel, seq_len=seq_len),
        grid=(bn, n_pair, nstep),
        in_specs=[pl.BlockSpec((NAT_QB * nq, pw), lambda b, p, i: (b * nstep + i, qcol + p)),
                  pl.BlockSpec((seq_len, pw), lambda b, p, i: (b, kcol + p)),
                  pl.BlockSpec((seq_len, pw), lambda b, p, i: (b, vcol + p)),
                  pl.BlockSpec((lctx, pw), lambda b, p, i: (b, kcol + p)),
                  pl.BlockSpec((lctx, pw), lambda b, p, i: (b, vcol + p)),
                  pl.BlockSpec((None, 2, nq, nk), lambda b, p, i: (first_pat(i), p, 0, 0)),
                  pl.BlockSpec((None, 2, nq, nk), lambda b, p, i: (1, p, 0, 0)),
                  pl.BlockSpec((None, 2, nq, nk), lambda b, p, i: (second_pat(i), p, 0, 0))],
        out_specs=pl.BlockSpec((NAT_QB * nq, pw), lambda b, p, i: (b * nstep + i, p)),
        out_shape=jax.ShapeDtypeStruct((bn * seq_len, NH_B * DH_B), BF16),
        compiler_params=_cparams(("parallel", "parallel", "arbitrary")),
        name="natten",
    )(px, px, px, pc, pc, bias, bias, bias)


def _merge_kernel(h_ref, oa_ref, na_ref, mg_ref, x_ref, mod_ref, gn_ref, wa_ref, wb_ref, wo_ref, o_ref):
    h = h_ref[...]
    parts = []
    for hd in range(NH_A):
        hh = h[:, hd * DV_A:(hd + 1) * DV_A]
        ms = jnp.mean(hh * hh, axis=-1, keepdims=True)
        parts.append(hh * lax.rsqrt(ms + EPS))
    ha = jnp.concatenate(parts, axis=1) * gn_ref[...]
    ha = (ha * jax.nn.sigmoid(oa_ref[...].astype(F32))).astype(BF16)
    d = wa_ref.shape[1]
    gates = jax.nn.sigmoid(mg_ref[...].astype(F32))
    t = (gates[:, :d] * jnp.dot(ha, wa_ref[...], preferred_element_type=F32)
         + gates[:, d:] * jnp.dot(na_ref[...], wb_ref[...], preferred_element_type=F32))
    mix = jnp.dot(t.astype(BF16), wo_ref[...], preferred_element_type=F32)
    g1 = mod_ref[2:3, :]
    o_ref[...] = x_ref[...] + g1 * mix


def _merge(h, proj, na, x, mod3, gn, wa, wb, wo, tm=512):
    bn, S, d = x.shape
    tm = min(tm, S)
    nblk = S // tm
    d_a = h.shape[-1]
    d_b = na.shape[-1]
    oa_blk = BLK_OA * PROJ_TN // d_a
    mg_blk = BLK_MG * PROJ_TN // (2 * d)
    const = lambda b, i: (0, 0)
    return pl.pallas_call(
        _merge_kernel,
        grid=(bn, nblk),
        in_specs=[pl.BlockSpec((tm, d_a), lambda b, i: (b * nblk + i, 0)),
                  pl.BlockSpec((tm, d_a), lambda b, i: (b * nblk + i, oa_blk)),
                  pl.BlockSpec((tm, d_b), lambda b, i: (b * nblk + i, 0)),
                  pl.BlockSpec((tm, 2 * d), lambda b, i: (b * nblk + i, mg_blk)),
                  pl.BlockSpec((None, tm, d), lambda b, i: (b, i, 0)),
                  pl.BlockSpec((None, N_MOD, d), lambda b, i: (b, 0, 0)),
                  pl.BlockSpec((1, d_a), const),
                  pl.BlockSpec((d_a, d), const, pipeline_mode=pl.Buffered(1)),
                  pl.BlockSpec((d_b, d), const, pipeline_mode=pl.Buffered(1)),
                  pl.BlockSpec((d, d), const, pipeline_mode=pl.Buffered(1))],
        out_specs=pl.BlockSpec((None, tm, d), lambda b, i: (b, i, 0)),
        out_shape=jax.ShapeDtypeStruct((bn, S, d), F32),
        compiler_params=_cparams(("parallel", "parallel")),
        name="merge",
    )(h, proj, na, proj, x, mod3, gn, wa, wb, wo)


def _ffn_kernel(x_ref, mod_ref, g_ref, wg_ref, wu_ref, wd_ref, o_ref):
    x = x_ref[...]
    hx = _mod_norm(x, g_ref[...], mod_ref[4:5, :], mod_ref[3:4, :]).astype(BF16)
    a = jnp.dot(hx, wg_ref[...], preferred_element_type=F32)
    u = jnp.dot(hx, wu_ref[...], preferred_element_type=F32)
    act = (a * jax.nn.sigmoid(a) * u).astype(BF16)
    f = jnp.dot(act, wd_ref[...], preferred_element_type=F32)
    o_ref[...] = x + mod_ref[5:6, :] * f


def _ffn(x, mod3, g, wg, wu, wd, tm=512):
    bn, S, d = x.shape
    tm = min(tm, S)
    dff = wg.shape[1]
    const = lambda b, i: (0, 0)
    return pl.pallas_call(
        _ffn_kernel,
        grid=(bn, S // tm),
        in_specs=[pl.BlockSpec((None, tm, d), lambda b, i: (b, i, 0)),
                  pl.BlockSpec((None, N_MOD, d), lambda b, i: (b, 0, 0)),
                  pl.BlockSpec((1, d), const),
                  pl.BlockSpec((d, dff), const, pipeline_mode=pl.Buffered(1)),
                  pl.BlockSpec((d, dff), const, pipeline_mode=pl.Buffered(1)),
                  pl.BlockSpec((dff, d), const, pipeline_mode=pl.Buffered(1))],
        out_specs=pl.BlockSpec((None, tm, d), lambda b, i: (b, i, 0)),
        out_shape=jax.ShapeDtypeStruct((bn, S, d), F32),
        compiler_params=_cparams(("parallel", "parallel")),
        name="ffn",
    )(x, mod3, g.reshape(1, d), wg, wu, wd)


def _rope_tables(S):
    t = np.arange(S)
    row, col = t // GRID_W, t % GRID_W
    half = DK_A // 4
    inv = ROPE_THETA ** (-np.arange(half, dtype=np.float64) / half)
    ang_r = row[:, None] * inv[None, :]
    ang_c = col[:, None] * inv[None, :]
    cos = np.concatenate([np.cos(ang_r)] * 2 + [np.cos(ang_c)] * 2, axis=1)
    sin = np.concatenate([-np.sin(ang_r), np.sin(ang_r), -np.sin(ang_c), np.sin(ang_c)], axis=1)
    reps = LANES // DK_A
    return (jnp.asarray(np.tile(cos, (1, reps)), F32), jnp.asarray(np.tile(sin, (1, reps)), F32))


def kernel(x, c, ctx, c_ctx, w_mod, b_mod, norm1_g, w_in, b_gates, mlstm_norm_g, qn_g, kn_g, rpb,
           w_branch_a, w_branch_b, w_out, norm2_g, w_ffn_gate, w_ffn_up, w_ffn_down):
    bn, S, d = x.shape
    lctx = ctx.shape[1]
    rows = S // GRID_W
    depth = w_mod.shape[0]
    d_a, d_b = NH_A * DV_A, NH_B * DH_B
    dqk = NH_A * DK_A
    sizes = (dqk, dqk, d_a, d_a, 4 * NH_A, d_b, d_b, d_b, 2 * d)
    offs = np.cumsum((0,) + sizes)
    cos_t, sin_t = _rope_tables(S)
    gsum = jnp.asarray(np.kron(np.eye(MXU_TILE // DH_B), np.ones((DH_B, DH_B))), BF16)

    for l in range(depth):
        assert l == depth - 1, "context-stream update for non-final layers is not implemented"
        n_rows = -(-(bn + 1) // 8) * 8
        cc = jnp.concatenate([c, c_ctx[None, :], jnp.zeros((n_rows - bn - 1, d), F32)], axis=0)
        mod3 = _modulation(cc, w_mod[l], b_mod[l]).reshape(n_rows, N_MOD, d)

        w = w_in[l]
        seg = lambda i: w[:, int(offs[i]):int(offs[i + 1])]
        w_main = jnp.concatenate([seg(0), seg(1), seg(2), seg(3), seg(5), seg(6), seg(7), seg(8)],
                                 axis=1).astype(BF16)
        w_g = jnp.pad(seg(4), ((0, 0), (0, LANES - 4 * NH_A))).astype(BF16)
        b_g = jnp.pad(b_gates[l].reshape(1, 4 * NH_A), ((0, 0), (0, LANES - 4 * NH_A)))
        qk_gain = jnp.stack([jnp.tile(qn_g[l], PROJ_TN // DH_B) * (DH_B ** -0.5 * LOG2E),
                             jnp.tile(kn_g[l], PROJ_TN // DH_B)])

        px, gcx, grx = _mixer_in(x, mod3, norm1_g[l], w_main, w_g, b_g, qk_gain, cos_t, sin_t, gsum,
                                 rope=True)
        pc, gcc, _ = _mixer_in(ctx, mod3, norm1_g[l], w_main, w_g, b_g, qk_gain, cos_t, sin_t, gsum,
                               rope=False, ctx_row=bn)

        zero_state = jnp.zeros((bn, 2, N_PAIR, 2 * DK_A, C_EXT), F32)
        _, _, st_ctx = _mlstm_states(pc, gcc, zero_state, bn=bn, seq_len=lctx)
        cf, cb, _ = _mlstm_states(px, gcx, st_ctx, bn=bn, seq_len=S)
        h_a = _mlstm_outputs(px, gcx, grx, cf, cb, bn=bn, seq_len=S)

        bias = _natten_bias(rpb[l], rows)
        na = _natten(px, pc, bias, bn=bn, seq_len=S, lctx=lctx)

        x_mid = _merge(h_a, px, na, x, mod3, mlstm_norm_g[l].reshape(1, d_a),
                       w_branch_a[l].astype(BF16), w_branch_b[l].astype(BF16), w_out[l].astype(BF16))
        x = _ffn(x_mid, mod3, norm2_g[l], w_ffn_gate[l].astype(BF16), w_ffn_up[l].astype(BF16),
                 w_ffn_down[l].astype(BF16))
    return x
```

```python
import functools

import numpy as np
import jax
import jax.numpy as jnp
from jax import lax
from jax.experimental import pallas as pl
from jax.experimental.pallas import tpu as pltpu

F32 = jnp.float32
BF16 = jnp.bfloat16

GRID_W = 64
NH_A, DK_A, DV_A = 8, 64, 128
NH_B, DH_B = 16, 64
CHUNK = 128
GATE_CAP = 15.0
WIN_R, WIN_C = 8, 16
ROPE_THETA = 10000.0
EPS = 1e-6
N_MOD = 6
LANES = 128
MXU_TILE = 256
VMEM_LIMIT = 56 * 1024 * 1024

QROWS = 4
KROWS = QROWS + WIN_R - 1

N_PAIR = NH_A // 2
MLSTM_CPS = 4
C_EXT = 2 * DV_A

G_A, G_WL, G_B, G_TOT = 0, 32, 72, 104


def _cparams(sem):
    return pltpu.CompilerParams(dimension_semantics=sem, vmem_limit_bytes=VMEM_LIMIT)


def _mod_kernel(c_ref, w_ref, b_ref, o_ref):
    c = c_ref[...]
    h = c * jax.nn.sigmoid(c)
    o_ref[...] = jnp.dot(h, w_ref[...], preferred_element_type=F32,
                         precision=lax.Precision.HIGHEST) + b_ref[...]


def _modulation(cc, w_mod, b_mod):
    rows, d = cc.shape
    n = w_mod.shape[1]
    tn = 1024
    return pl.pallas_call(
        _mod_kernel,
        grid=(n // tn,),
        in_specs=[pl.BlockSpec((rows, d), lambda j: (0, 0)),
                  pl.BlockSpec((d, tn), lambda j: (0, j)),
                  pl.BlockSpec((1, tn), lambda j: (0, j))],
        out_specs=pl.BlockSpec((rows, tn), lambda j: (0, j)),
        out_shape=jax.ShapeDtypeStruct((rows, n), F32),
        compiler_params=_cparams(("arbitrary",)),
        name="modulation",
    )(cc, w_mod, b_mod.reshape(1, n))


def _mod_norm(x, gain, scale, shift):
    ms = jnp.mean(x * x, axis=-1, keepdims=True)
    y = x * lax.rsqrt(ms + EPS) * gain
    return y * (1.0 + scale) + shift


def _gate_pack(g, gc_ref, gr_ref):
    tm = g.shape[0]
    g = GATE_CAP * jnp.tanh(g * (1.0 / GATE_CAP))
    lane_t = lax.broadcasted_iota(jnp.int32, g.shape, 1)
    is_forget = (lane_t & NH_A) != 0
    x = jnp.where(is_forget, jax.nn.log_sigmoid(g), g)
    x = jnp.where(lane_t < 4 * NH_A, x, 0.0)

    t_idx = lax.broadcasted_iota(jnp.int32, (CHUNK, CHUNK), 0)
    s_idx = lax.broadcasted_iota(jnp.int32, (CHUNK, CHUNK), 1)
    tri_f = (s_idx <= t_idx).astype(F32)
    tri_b = (s_idx >= t_idx).astype(F32)
    lane = s_idx
    fwd_half = lane < 2 * NH_A
    low = lane < 4 * NH_A
    hi = lax.Precision.HIGHEST
    for ci in range(tm // CHUNK):
        xc = x[ci * CHUNK:(ci + 1) * CHUNK]
        cf = jnp.dot(tri_f, xc, preferred_element_type=F32, precision=hi)
        cb = jnp.dot(tri_b, xc, preferred_element_type=F32, precision=hi)
        tot = jnp.broadcast_to(jnp.sum(xc, axis=0, keepdims=True), xc.shape)
        r_b = jnp.where(fwd_half, pltpu.roll(cf, LANES - NH_A, 1), pltpu.roll(cb, LANES - NH_A, 1))
        a = xc - r_b
        wl = xc + (pltpu.roll(tot, LANES - NH_A, 1) - r_b)
        bsel = jnp.where(fwd_half, cf, cb)
        z = lambda v: jnp.where(low, v, 0.0)
        packed = z(a) + pltpu.roll(z(wl), 32, 1) + pltpu.roll(z(bsel), 64, 1) + pltpu.roll(z(tot), 96, 1)
        gc_ref[ci * CHUNK:(ci + 1) * CHUNK, :] = packed
        gr_ref[:, ci * CHUNK:(ci + 1) * CHUNK] = packed.T


PROJ_TN = 512
BLK_QA, BLK_KA, BLK_VA, BLK_OA, BLK_QB, BLK_KB, BLK_VB, BLK_MG = 0, 1, 2, 4, 6, 8, 10, 12
N_PROJ_BLK = 16
LOG2E = 1.4426950408889634


def _swap16(x):
    n = x.shape[-1]
    lane = lax.broadcasted_iota(jnp.int32, x.shape, x.ndim - 1)
    first = (lane & 31) < 16
    return jnp.where(first, pltpu.roll(x, n - 16, x.ndim - 1), pltpu.roll(x, 16, x.ndim - 1))


def _mixer_in_kernel(x_ref, mod_ref, g_ref, w_ref, wg_ref, bg_ref, qk_ref, cos_ref, sin_ref, gsum_ref,
                     o_ref, kt_ref, gc_ref, gr_ref, *, rope):
    hx = _mod_norm(x_ref[...], g_ref[...], mod_ref[1:2, :], mod_ref[0:1, :]).astype(BF16)
    if rope:
        reps = PROJ_TN // LANES
        cos = jnp.concatenate([cos_ref[...]] * reps, axis=1)
        sin = jnp.concatenate([sin_ref[...]] * reps, axis=1)
    for blk in range(N_PROJ_BLK):
        cols = slice(blk * PROJ_TN, (blk + 1) * PROJ_TN)
        acc = jnp.dot(hx, w_ref[:, cols], preferred_element_type=F32)
        if blk < BLK_VA:
            if rope:
                acc = acc * cos + _swap16(acc) * sin
            if blk == BLK_QA:
                acc = acc * DK_A ** -0.5
        elif BLK_QB <= blk < BLK_VB:
            sq = (acc * acc).astype(BF16)
            gw = gsum_ref.shape[0]
            ss = jnp.concatenate([jnp.dot(sq[:, c0:c0 + gw], gsum_ref[...], preferred_element_type=F32)
                                  for c0 in range(0, PROJ_TN, gw)], axis=1)
            gain = qk_ref[0:1, :] if blk < BLK_KB else qk_ref[1:2, :]
            acc = acc * lax.rsqrt(ss * (1.0 / DH_B) + EPS) * gain
        if blk == BLK_KA:
            kt_ref[...] = acc.T.astype(kt_ref.dtype)
        o_ref[:, cols] = acc.astype(o_ref.dtype)
    _gate_pack(jnp.dot(hx, wg_ref[...], preferred_element_type=F32) + bg_ref[...], gc_ref, gr_ref)


def _mixer_in(x, mod3, g, w, w_g, b_g, qk_gain, cos, sin, gsum, *, rope, ctx_row=None, tm=512):
    bn, L, d = x.shape
    tm = min(tm, L)
    nblk = L // tm
    n = w.shape[1]
    mod_map = (lambda b, i: (b, 0, 0)) if ctx_row is None else (lambda b, i: (ctx_row, 0, 0))
    const = lambda b, i: (0, 0)
    tok = lambda b, i: (b * nblk + i, 0)
    resident = pl.Buffered(1)
    return pl.pallas_call(
        functools.partial(_mixer_in_kernel, rope=rope),
        grid=(bn, nblk),
        in_specs=[pl.BlockSpec((None, tm, d), lambda b, i: (b, i, 0)),
                  pl.BlockSpec((None, N_MOD, d), mod_map),
                  pl.BlockSpec((1, d), const),
                  pl.BlockSpec((d, n), const, pipeline_mode=resident),
                  pl.BlockSpec((d, LANES), const, pipeline_mode=resident),
                  pl.BlockSpec((1, LANES), const),
                  pl.BlockSpec((2, PROJ_TN), const),
                  pl.BlockSpec((tm, LANES), lambda b, i: (i, 0)),
                  pl.BlockSpec((tm, LANES), lambda b, i: (i, 0)),
                  pl.BlockSpec(gsum.shape, const, pipeline_mode=resident)],
        out_specs=[pl.BlockSpec((tm, n), tok),
                   pl.BlockSpec((PROJ_TN, tm), lambda b, i: (0, b * nblk + i)),
                   pl.BlockSpec((tm, LANES), tok),
                   pl.BlockSpec((LANES, tm), lambda b, i: (0, b * nblk + i))],
        out_shape=[jax.ShapeDtypeStruct((bn * L, n), BF16),
                   jax.ShapeDtypeStruct((PROJ_TN, bn * L), BF16),
                   jax.ShapeDtypeStruct((bn * L, LANES), F32),
                   jax.ShapeDtypeStruct((LANES, bn * L), F32)],
        compiler_params=_cparams(("parallel", "parallel")),
        name="mixer_in",
    )(x, mod3, g.reshape(1, d), w, w_g, b_g, qk_gain, cos, sin, gsum)


def _mlstm_state_kernel(kf_ref, vf_ref, gf_ref, kb_ref, vb_ref, gb_ref, c0_ref,
                        cf_ref, cb_ref, cfin_ref, st, *, nsteps, cps):
    s = pl.program_id(1)

    @pl.when(s == 0)
    def _():
        st[...] = c0_ref[...]

    lc = CHUNK
    top_k = lax.broadcasted_iota(jnp.int32, (2 * DK_A, lc), 0) < DK_A
    top = lax.broadcasted_iota(jnp.int32, (2 * DK_A, C_EXT), 0) < DK_A
    ones_blk = jnp.ones((lc, DV_A), BF16)
    for c in range(cps):
        for d, (k_ref, v_ref, g_ref, out_ref) in enumerate(((kf_ref, vf_ref, gf_ref, cf_ref),
                                                            (kb_ref, vb_ref, gb_ref, cb_ref))):
            cc = c if d == 0 else cps - 1 - c
            tok = slice(cc * lc, (cc + 1) * lc)
            g = g_ref[:, tok]
            for p in range(N_PAIR):
                h0, h1 = 2 * p, 2 * p + 1
                wl = G_WL + 16 * d
                w = jnp.where(top_k, jnp.exp(g[wl + h0:wl + h0 + 1, :]), jnp.exp(g[wl + h1:wl + h1 + 1, :]))
                kw = k_ref[p * 2 * DK_A:(p + 1) * 2 * DK_A, tok].astype(F32) * w
                kw0 = jnp.where(top_k, kw, 0.0).astype(BF16)
                kw1 = jnp.where(top_k, 0.0, kw).astype(BF16)
                vext0 = jnp.concatenate([v_ref[tok, h0 * DV_A:(h0 + 1) * DV_A], ones_blk], axis=1)
                vext1 = jnp.concatenate([v_ref[tok, h1 * DV_A:(h1 + 1) * DV_A], ones_blk], axis=1)
                upd = (jnp.dot(kw0, vext0, preferred_element_type=F32)
                       + jnp.dot(kw1, vext1, preferred_element_type=F32))
                c_old = st[d, p]
                out_ref[cc, p, 0] = c_old[:DK_A].astype(out_ref.dtype)
                out_ref[cc, p, 1] = c_old[DK_A:].astype(out_ref.dtype)
                tl = G_TOT + 16 * d
                dec = jnp.where(top, jnp.exp(g[tl + h0:tl + h0 + 1, 0:1]), jnp.exp(g[tl + h1:tl + h1 + 1, 0:1]))
                st[d, p] = dec * c_old + upd

    @pl.when(s == nsteps - 1)
    def _():
        cfin_ref[...] = st[...]


def _mlstm_states(proj, kt, gr, c0, *, bn, seq_len):
    nc = seq_len // CHUNK
    cps = min(MLSTM_CPS, nc)
    nsteps = nc // cps
    lb = cps * CHUNK
    vcol = BLK_VA * PROJ_TN // (NH_A * DV_A)
    fwd = lambda b, s: b * nsteps + s
    bwd = lambda b, s: b * nsteps + nsteps - 1 - s
    st_shape = (2, N_PAIR, 2 * DK_A, C_EXT)
    out_blk = (None, cps, N_PAIR, 2, DK_A, C_EXT)
    return pl.pallas_call(
        functools.partial(_mlstm_state_kernel, nsteps=nsteps, cps=cps),
        grid=(bn, nsteps),
        in_specs=[pl.BlockSpec((NH_A * DK_A, lb), lambda b, s: (0, fwd(b, s))),
                  pl.BlockSpec((lb, NH_A * DV_A), lambda b, s: (fwd(b, s), vcol)),
                  pl.BlockSpec((LANES, lb), lambda b, s: (0, fwd(b, s))),
                  pl.BlockSpec((NH_A * DK_A, lb), lambda b, s: (0, bwd(b, s))),
                  pl.BlockSpec((lb, NH_A * DV_A), lambda b, s: (bwd(b, s), vcol)),
                  pl.BlockSpec((LANES, lb), lambda b, s: (0, bwd(b, s))),
                  pl.BlockSpec((None,) + st_shape, lambda b, s: (b, 0, 0, 0, 0))],
        out_specs=[pl.BlockSpec(out_blk, lambda b, s: (b, s, 0, 0, 0, 0)),
                   pl.BlockSpec(out_blk, lambda b, s: (b, nsteps - 1 - s, 0, 0, 0, 0)),
                   pl.BlockSpec((None,) + st_shape, lambda b, s: (b, 0, 0, 0, 0))],
        out_shape=[jax.ShapeDtypeStruct((bn, nc, N_PAIR, 2, DK_A, C_EXT), BF16),
                   jax.ShapeDtypeStruct((bn, nc, N_PAIR, 2, DK_A, C_EXT), BF16),
                   jax.ShapeDtypeStruct((bn,) + st_shape, F32)],
        scratch_shapes=[pltpu.VMEM(st_shape, F32)],
        compiler_params=_cparams(("parallel", "arbitrary")),
        name="mlstm_states",
    )(kt, proj, gr, kt, proj, gr, c0)


def _mlstm_out_kernel(q_ref, kt_ref, v_ref, gc_ref, gr_ref, cf_ref, cb_ref, h_ref):
    lc = CHUNK
    t_idx = lax.broadcasted_iota(jnp.int32, (lc, lc), 0)
    s_idx = lax.broadcasted_iota(jnp.int32, (lc, lc), 1)
    visible = (s_idx <= t_idx, s_idx >= t_idx)
    lo = lax.broadcasted_iota(jnp.int32, (lc, 2 * DK_A), 1) < DK_A
    ones_blk = jnp.ones((lc, DV_A), BF16)
    zero_c = jnp.zeros((DK_A, C_EXT), BF16)
    for c in range(q_ref.shape[0] // lc):
        tok = slice(c * lc, (c + 1) * lc)
        gc = gc_ref[tok, :]
        gr = gr_ref[:, tok]
        for p in range(N_PAIR):
            qp = q_ref[tok, p * 2 * DK_A:(p + 1) * 2 * DK_A]
            ktp = kt_ref[p * 2 * DK_A:(p + 1) * 2 * DK_A, tok]
            zq = jnp.zeros_like(qp)
            q_stack = jnp.concatenate([jnp.where(lo, qp, zq), jnp.where(lo, zq, qp)], axis=0)
            s_both = jnp.dot(q_stack, ktp, preferred_element_type=F32)
            b_rep = [[jnp.broadcast_to(gc[:, G_B + 16 * d + 2 * p + hh:G_B + 16 * d + 2 * p + hh + 1], (lc, lc))
                      for hh in range(2)] for d in range(2)]
            inter = []
            for d, c_ref in enumerate((cf_ref, cb_ref)):
                e = jnp.where(lo, jnp.exp(b_rep[d][0]), jnp.exp(b_rep[d][1]))
                qs = (qp.astype(F32) * e).astype(BF16)
                c_pair = jnp.concatenate([jnp.concatenate([c_ref[c, p, 0], zero_c], axis=1),
                                          jnp.concatenate([zero_c, c_ref[c, p, 1]], axis=1)], axis=0)
                inter.append(jnp.dot(qs, c_pair, preferred_element_type=F32))
            for hh in range(2):
                h = 2 * p + hh
                s_h = s_both[hh * lc:(hh + 1) * lc]
                vext = jnp.concatenate([v_ref[tok, h * DV_A:(h + 1) * DV_A], ones_blk], axis=1)
                acc = None
                for d in range(2):
                    a_row = gr[G_A + 16 * d + h:G_A + 16 * d + h + 1, :]
                    log_d = jnp.where(visible[d], b_rep[d][hh] + a_row, -jnp.inf)
                    pm = (s_h * jnp.exp(log_d)).astype(BF16)
                    nd = (jnp.dot(pm, vext, preferred_element_type=F32)
                          + inter[d][:, hh * C_EXT:(hh + 1) * C_EXT])
                    hd = nd[:, :DV_A] / jnp.maximum(jnp.abs(nd[:, DV_A:]), 1.0)
                    acc = hd if acc is None else acc + hd
                h_ref[tok, h * DV_A:(h + 1) * DV_A] = acc.astype(h_ref.dtype)


def _mlstm_outputs(proj, kt, gc, gr, cf, cb, *, bn, seq_len):
    cps = min(MLSTM_CPS, seq_len // CHUNK)
    lc = cps * CHUNK
    nc = seq_len // lc
    qcol = BLK_QA * PROJ_TN // (NH_A * DK_A)
    vcol = BLK_VA * PROJ_TN // (NH_A * DV_A)
    tok = lambda b, c: b * nc + c
    st_blk = (None, cps, N_PAIR, 2, DK_A, C_EXT)
    st_map = lambda b, c: (b, c, 0, 0, 0, 0)
    return pl.pallas_call(
        _mlstm_out_kernel,
        grid=(bn, nc),
        in_specs=[pl.BlockSpec((lc, NH_A * DK_A), lambda b, c: (tok(b, c), qcol)),
                  pl.BlockSpec((NH_A * DK_A, lc), lambda b, c: (0, tok(b, c))),
                  pl.BlockSpec((lc, NH_A * DV_A), lambda b, c: (tok(b, c), vcol)),
                  pl.BlockSpec((lc, LANES), lambda b, c: (tok(b, c), 0)),
                  pl.BlockSpec((LANES, lc), lambda b, c: (0, tok(b, c))),
                  pl.BlockSpec(st_blk, st_map),
                  pl.BlockSpec(st_blk, st_map)],
        out_specs=pl.BlockSpec((lc, NH_A * DV_A), lambda b, c: (tok(b, c), 0)),
        out_shape=jax.ShapeDtypeStruct((bn * seq_len, NH_A * DV_A), F32),
        compiler_params=_cparams(("parallel", "parallel")),
        name="mlstm_outputs",
    )(proj, kt, proj, gc, gr, cf, cb)


NAT_QB = 4


def _natten_kernel(q_ref, k_ref, v_ref, kc_ref, vc_ref, ba_ref, bm_ref, bb_ref, o_ref, *, seq_len):
    i2 = pl.program_id(2)
    nq = QROWS * GRID_W
    nk = KROWS * GRID_W
    lctx = kc_ref.shape[0]
    nt = (((1,), (1,)), ((), ()))

    def halves(rows):
        lane = lax.broadcasted_iota(jnp.int32, (rows, 2 * DH_B), 1)
        return lane, lane < DH_B

    _, lo_q = halves(nq)
    lane_k, lo_k = halves(nk)
    lane_c, lo_c = halves(lctx)
    kc = kc_ref[...]
    vc = vc_ref[...]
    zq = jnp.zeros((nq, 2 * DH_B), BF16)
    vch_pair = [jnp.where(lo_c if h == 0 else ~lo_c, vc, (lane_c == (DH_B if h == 0 else 0)).astype(BF16))
                for h in range(2)]
    bias_refs = (ba_ref,) + (bm_ref,) * (NAT_QB - 2) + (bb_ref,)
    for j, bias_ref in enumerate(bias_refs):
        blk = NAT_QB * i2 + j
        start = jnp.clip(blk * nq - (WIN_R // 2) * GRID_W, 0, seq_len - nk)
        start = pl.multiple_of(start, GRID_W)
        q = q_ref[j * nq:(j + 1) * nq, :]
        kb = k_ref[pl.ds(start, nk), :]
        vb = v_ref[pl.ds(start, nk), :]
        outs = []
        for h in range(2):
            one_lane = DH_B if h == 0 else 0
            own_q, own_k = (lo_q, lo_k) if h == 0 else (~lo_q, ~lo_k)
            qh = jnp.where(own_q, q, zq)
            s_win = lax.dot_general(qh, kb, nt, preferred_element_type=F32) + bias_ref[h]
            s_ctx = lax.dot_general(qh, kc, nt, preferred_element_type=F32)
            m = jnp.maximum(jnp.max(s_win, axis=-1, keepdims=True), jnp.max(s_ctx, axis=-1, keepdims=True))
            p_win = jnp.exp2(s_win - m).astype(BF16)
            p_ctx = jnp.exp2(s_ctx - m).astype(BF16)
            vh = jnp.where(own_k, vb, (lane_k == one_lane).astype(BF16))
            o = (jnp.dot(p_win, vh, preferred_element_type=F32)
                 + jnp.dot(p_ctx, vch_pair[h], preferred_element_type=F32))
            outs.append(o / o[:, one_lane:one_lane + 1])
        o_ref[j * nq:(j + 1) * nq, :] = jnp.where(lo_q, outs[0], outs[1]).astype(o_ref.dtype)


def _natten_patterns(rows):
    nblk = rows // QROWS
    pats = []
    for blk in (0, 1 if nblk > 2 else 0, nblk - 1):
        r0 = blk * QROWS
        k0 = int(np.clip(r0 - WIN_R // 2, 0, rows - KROWS))
        qr = r0 + np.arange(QROWS)
        kr = k0 + np.arange(KROWS)
        wr0 = np.clip(qr - WIN_R // 2, 0, rows - WIN_R)
        valid = (kr[None, :] >= wr0[:, None]) & (kr[None, :] < wr0[:, None] + WIN_R)
        dr = np.clip(kr[None, :] - qr[:, None] + WIN_R - 1, 0, 2 * WIN_R - 2)
        pats.append((valid, dr))
    return pats


def _bias_kernel(t_ref, o_ref, *, patterns):
    pat = pl.program_id(0)
    for ps, (valid, dr) in enumerate(patterns):
        @pl.when(pat == ps)
        def _():
            for xq in range(QROWS):
                for yk in range(KROWS):
                    if valid[xq, yk]:
                        blk = t_ref[int(dr[xq, yk])]
                    else:
                        blk = jnp.full((GRID_W, GRID_W), -jnp.inf, F32)
                    o_ref[xq * GRID_W:(xq + 1) * GRID_W, yk * GRID_W:(yk + 1) * GRID_W] = blk


def _natten_bias(rpb, rows):
    nh = rpb.shape[0]
    qc = np.arange(GRID_W)
    kc = np.arange(GRID_W)
    wc0 = np.clip(qc - WIN_C // 2, 0, GRID_W - WIN_C)
    in_c = (kc[None, :] >= wc0[:, None]) & (kc[None, :] < wc0[:, None] + WIN_C)
    dc = np.clip(kc[None, :] - qc[:, None] + WIN_C - 1, 0, 2 * WIN_C - 2)
    sel_c = jnp.asarray(dc[:, :, None] == np.arange(2 * WIN_C - 1), F32)
    tcol = jnp.einsum('hab,uvb->hauv', rpb.astype(F32) * LOG2E, sel_c, precision=lax.Precision.HIGHEST)
    tcol = jnp.where(in_c[None, None], tcol, -jnp.inf)
    nq, nk = QROWS * GRID_W, KROWS * GRID_W
    n_dr = 2 * WIN_R - 1
    return pl.pallas_call(
        functools.partial(_bias_kernel, patterns=_natten_patterns(rows)),
        grid=(3, nh),
        in_specs=[pl.BlockSpec((None, n_dr, GRID_W, GRID_W), lambda p, h: (h, 0, 0, 0))],
        out_specs=pl.BlockSpec((None, None, nq, nk), lambda p, h: (p, h, 0, 0)),
        out_shape=jax.ShapeDtypeStruct((3, nh, nq, nk), F32),
        compiler_params=_cparams(("arbitrary", "arbitrary")),
        name="natten_bias",
    )(tcol)


def _natten(px, pc, bias, *, bn, seq_len, lctx):
    nq = QROWS * GRID_W
    nk = KROWS * GRID_W
    nstep = seq_len // (NAT_QB * nq)
    pw = 2 * DH_B
    n_pair = NH_B // 2
    qcol, kcol, vcol = (blk * PROJ_TN // pw for blk in (BLK_QB, BLK_KB, BLK_VB))
    last = nstep - 1
    first_pat = lambda i: jnp.where(i == 0, 0, 1)
    second_pat = lambda i: jnp.where(i == last, 2, 1)
    return pl.pallas_call(
        functools.partial(_natten_kernel, seq_len=seq_len),
        grid=(bn, n_pair, nstep),
        in_specs=[pl.BlockSpec((NAT_QB * nq, pw), lambda b, p, i: (b * nstep + i, qcol + p)),
                  pl.BlockSpec((seq_len, pw), lambda b, p, i: (b, kcol + p)),
                  pl.BlockSpec((seq_len, pw), lambda b, p, i: (b, vcol + p)),
                  pl.BlockSpec((lctx, pw), lambda b, p, i: (b, kcol + p)),
                  pl.BlockSpec((lctx, pw), lambda b, p, i: (b, vcol + p)),
                  pl.BlockSpec((None, 2, nq, nk), lambda b, p, i: (first_pat(i), p, 0, 0)),
                  pl.BlockSpec((None, 2, nq, nk), lambda b, p, i: (1, p, 0, 0)),
                  pl.BlockSpec((None, 2, nq, nk), lambda b, p, i: (second_pat(i), p, 0, 0))],
        out_specs=pl.BlockSpec((NAT_QB * nq, pw), lambda b, p, i: (b * nstep + i, p)),
        out_shape=jax.ShapeDtypeStruct((bn * seq_len, NH_B * DH_B), BF16),
        compiler_params=_cparams(("parallel", "parallel", "arbitrary")),
        name="natten",
    )(px, px, px, pc, pc, bias, bias, bias)


def _merge_kernel(h_ref, oa_ref, na_ref, mg_ref, x_ref, mod_ref, gn_ref, wa_ref, wb_ref, wo_ref, o_ref):
    h = h_ref[...]
    parts = []
    for hd in range(NH_A):
        hh = h[:, hd * DV_A:(hd + 1) * DV_A]
        ms = jnp.mean(hh * hh, axis=-1, keepdims=True)
        parts.append(hh * lax.rsqrt(ms + EPS))
    ha = jnp.concatenate(parts, axis=1) * gn_ref[...]
    ha = (ha * jax.nn.sigmoid(oa_ref[...].astype(F32))).astype(BF16)
    d = wa_ref.shape[1]
    gates = jax.nn.sigmoid(mg_ref[...].astype(F32))
    t = (gates[:, :d] * jnp.dot(ha, wa_ref[...], preferred_element_type=F32)
         + gates[:, d:] * jnp.dot(na_ref[...], wb_ref[...], preferred_element_type=F32))
    mix = jnp.dot(t.astype(BF16), wo_ref[...], preferred_element_type=F32)
    g1 = mod_ref[2:3, :]
    o_ref[...] = x_ref[...] + g1 * mix


def _merge(h, proj, na, x, mod3, gn, wa, wb, wo, tm=512):
    bn, S, d = x.shape
    tm = min(tm, S)
    nblk = S // tm
    d_a = h.shape[-1]
    d_b = na.shape[-1]
    oa_blk = BLK_OA * PROJ_TN // d_a
    mg_blk = BLK_MG * PROJ_TN // (2 * d)
    const = lambda b, i: (0, 0)
    return pl.pallas_call(
        _merge_kernel,
        grid=(bn, nblk),
        in_specs=[pl.BlockSpec((tm, d_a), lambda b, i: (b * nblk + i, 0)),
                  pl.BlockSpec((tm, d_a), lambda b, i: (b * nblk + i, oa_blk)),
                  pl.BlockSpec((tm, d_b), lambda b, i: (b * nblk + i, 0)),
                  pl.BlockSpec((tm, 2 * d), lambda b, i: (b * nblk + i, mg_blk)),
                  pl.BlockSpec((None, tm, d), lambda b, i: (b, i, 0)),
                  pl.BlockSpec((None, N_MOD, d), lambda b, i: (b, 0, 0)),
                  pl.BlockSpec((1, d_a), const),
                  pl.BlockSpec((d_a, d), const, pipeline_mode=pl.Buffered(1)),
                  pl.BlockSpec((d_b, d), const, pipeline_mode=pl.Buffered(1)),
                  pl.BlockSpec((d, d), const, pipeline_mode=pl.Buffered(1))],
        out_specs=pl.BlockSpec((None, tm, d), lambda b, i: (b, i, 0)),
        out_shape=jax.ShapeDtypeStruct((bn, S, d), F32),
        compiler_params=_cparams(("parallel", "parallel")),
        name="merge",
    )(h, proj, na, proj, x, mod3, gn, wa, wb, wo)


def _ffn_kernel(x_ref, mod_ref, g_ref, wg_ref, wu_ref, wd_ref, o_ref):
    x = x_ref[...]
    hx = _mod_norm(x, g_ref[...], mod_ref[4:5, :], mod_ref[3:4, :]).astype(BF16)
    a = jnp.dot(hx, wg_ref[...], preferred_element_type=F32)
    u = jnp.dot(hx, wu_ref[...], preferred_element_type=F32)
    act = (a * jax.nn.sigmoid(a) * u).astype(BF16)
    f = jnp.dot(act, wd_ref[...], preferred_element_type=F32)
    o_ref[...] = x + mod_ref[5:6, :] * f


def _ffn(x, mod3, g, wg, wu, wd, tm=512):
    bn, S, d = x.shape
    tm = min(tm, S)
    dff = wg.shape[1]
    const = lambda b, i: (0, 0)
    return pl.pallas_call(
        _ffn_kernel,
        grid=(bn, S // tm),
        in_specs=[pl.BlockSpec((None, tm, d), lambda b, i: (b, i, 0)),
                  pl.BlockSpec((None, N_MOD, d), lambda b, i: (b, 0, 0)),
                  pl.BlockSpec((1, d), const),
                  pl.BlockSpec((d, dff), const, pipeline_mode=pl.Buffered(1)),
                  pl.BlockSpec((d, dff), const, pipeline_mode=pl.Buffered(1)),
                  pl.BlockSpec((dff, d), const, pipeline_mode=pl.Buffered(1))],
        out_specs=pl.BlockSpec((None, tm, d), lambda b, i: (b, i, 0)),
        out_shape=jax.ShapeDtypeStruct((bn, S, d), F32),
        compiler_params=_cparams(("parallel", "parallel")),
        name="ffn",
    )(x, mod3, g.reshape(1, d), wg, wu, wd)


def _rope_tables(S):
    t = np.arange(S)
    row, col = t // GRID_W, t % GRID_W
    half = DK_A // 4
    inv = ROPE_THETA ** (-np.arange(half, dtype=np.float64) / half)
    ang_r = row[:, None] * inv[None, :]
    ang_c = col[:, None] * inv[None, :]
    cos = np.concatenate([np.cos(ang_r)] * 2 + [np.cos(ang_c)] * 2, axis=1)
    sin = np.concatenate([-np.sin(ang_r), np.sin(ang_r), -np.sin(ang_c), np.sin(ang_c)], axis=1)
    reps = LANES // DK_A
    return (jnp.asarray(np.tile(cos, (1, reps)), F32), jnp.asarray(np.tile(sin, (1, reps)), F32))


def kernel(x, c, ctx, c_ctx, w_mod, b_mod, norm1_g, w_in, b_gates, mlstm_norm_g, qn_g, kn_g, rpb,
           w_branch_a, w_branch_b, w_out, norm2_g, w_ffn_gate, w_ffn_up, w_ffn_down):
    bn, S, d = x.shape
    lctx = ctx.shape[1]
    rows = S // GRID_W
    depth = w_mod.shape[0]
    d_a, d_b = NH_A * DV_A, NH_B * DH_B
    dqk = NH_A * DK_A
    sizes = (dqk, dqk, d_a, d_a, 4 * NH_A, d_b, d_b, d_b, 2 * d)
    offs = np.cumsum((0,) + sizes)
    cos_t, sin_t = _rope_tables(S)
    gsum = jnp.asarray(np.kron(np.eye(MXU_TILE // DH_B), np.ones((DH_B, DH_B))), BF16)

    for l in range(depth):
        assert l == depth - 1, "context-stream update for non-final layers is not implemented"
        n_rows = -(-(bn + 1) // 8) * 8
        cc = jnp.concatenate([c, c_ctx[None, :], jnp.zeros((n_rows - bn - 1, d), F32)], axis=0)
        mod3 = _modulation(cc, w_mod[l], b_mod[l]).reshape(n_rows, N_MOD, d)

        w = w_in[l]
        seg = lambda i: w[:, int(offs[i]):int(offs[i + 1])]
        w_main = jnp.concatenate([seg(0), seg(1), seg(2), seg(3), seg(5), seg(6), seg(7), seg(8)],
                                 axis=1).astype(BF16)
        w_g = jnp.pad(seg(4), ((0, 0), (0, LANES - 4 * NH_A))).astype(BF16)
        b_g = jnp.pad(b_gates[l].reshape(1, 4 * NH_A), ((0, 0), (0, LANES - 4 * NH_A)))
        qk_gain = jnp.stack([jnp.tile(qn_g[l], PROJ_TN // DH_B) * (DH_B ** -0.5 * LOG2E),
                             jnp.tile(kn_g[l], PROJ_TN // DH_B)])

        px, ktx, gcx, grx = _mixer_in(x, mod3, norm1_g[l], w_main, w_g, b_g, qk_gain, cos_t, sin_t, gsum,
                                      rope=True)
        pc, ktc, _, grc = _mixer_in(ctx, mod3, norm1_g[l], w_main, w_g, b_g, qk_gain, cos_t, sin_t, gsum,
                                    rope=False, ctx_row=bn)

        zero_state = jnp.zeros((bn, 2, N_PAIR, 2 * DK_A, C_EXT), F32)
        _, _, st_ctx = _mlstm_states(pc, ktc, grc, zero_state, bn=bn, seq_len=lctx)
        cf, cb, _ = _mlstm_states(px, ktx, grx, st_ctx, bn=bn, seq_len=S)
        h_a = _mlstm_outputs(px, ktx, gcx, grx, cf, cb, bn=bn, seq_len=S)

        bias = _natten_bias(rpb[l], rows)
        na = _natten(px, pc, bias, bn=bn, seq_len=S, lctx=lctx)

        x_mid = _merge(h_a, px, na, x, mod3, mlstm_norm_g[l].reshape(1, d_a),
                       w_branch_a[l].astype(BF16), w_branch_b[l].astype(BF16), w_out[l].astype(BF16))
        x = _ffn(x_mid, mod3, norm2_g[l], w_ffn_gate[l].astype(BF16), w_ffn_up[l].astype(BF16),
                 w_ffn_down[l].astype(BF16))
    return x
```

```python
import functools

import numpy as np
import jax
import jax.numpy as jnp
from jax import lax
from jax.experimental import pallas as pl
from jax.experimental.pallas import tpu as pltpu

F32 = jnp.float32
BF16 = jnp.bfloat16

GRID_W = 64
NH_A, DK_A, DV_A = 8, 64, 128
NH_B, DH_B = 16, 64
CHUNK = 128
GATE_CAP = 15.0
WIN_R, WIN_C = 8, 16
ROPE_THETA = 10000.0
EPS = 1e-6
N_MOD = 6
LANES = 128
MXU_TILE = 256
VMEM_LIMIT = 56 * 1024 * 1024

QROWS = 4
KROWS = QROWS + WIN_R - 1

N_PAIR = NH_A // 2
MLSTM_CPS = 4
C_EXT = 2 * DV_A

G_A, G_WL, G_B, G_TOT = 0, 32, 72, 104


def _cparams(sem):
    return pltpu.CompilerParams(dimension_semantics=sem, vmem_limit_bytes=VMEM_LIMIT)


def _mod_kernel(c_ref, w_ref, b_ref, o_ref):
    c = c_ref[...]
    h = c * jax.nn.sigmoid(c)
    o_ref[...] = jnp.dot(h, w_ref[...], preferred_element_type=F32,
                         precision=lax.Precision.HIGHEST) + b_ref[...]


def _modulation(cc, w_mod, b_mod):
    rows, d = cc.shape
    n = w_mod.shape[1]
    tn = 1024
    return pl.pallas_call(
        _mod_kernel,
        grid=(n // tn,),
        in_specs=[pl.BlockSpec((rows, d), lambda j: (0, 0)),
                  pl.BlockSpec((d, tn), lambda j: (0, j)),
                  pl.BlockSpec((1, tn), lambda j: (0, j))],
        out_specs=pl.BlockSpec((rows, tn), lambda j: (0, j)),
        out_shape=jax.ShapeDtypeStruct((rows, n), F32),
        compiler_params=_cparams(("arbitrary",)),
        name="modulation",
    )(cc, w_mod, b_mod.reshape(1, n))


def _mod_norm(x, gain, scale, shift):
    ms = jnp.mean(x * x, axis=-1, keepdims=True)
    y = x * lax.rsqrt(ms + EPS) * gain
    return y * (1.0 + scale) + shift


def _gate_pack(g, gc_ref, gr_ref):
    tm = g.shape[0]
    g = GATE_CAP * jnp.tanh(g * (1.0 / GATE_CAP))
    lane_t = lax.broadcasted_iota(jnp.int32, g.shape, 1)
    is_forget = (lane_t & NH_A) != 0
    x = jnp.where(is_forget, jax.nn.log_sigmoid(g), g)
    x = jnp.where(lane_t < 4 * NH_A, x, 0.0)

    t_idx = lax.broadcasted_iota(jnp.int32, (CHUNK, CHUNK), 0)
    s_idx = lax.broadcasted_iota(jnp.int32, (CHUNK, CHUNK), 1)
    tri_f = (s_idx <= t_idx).astype(F32)
    tri_b = (s_idx >= t_idx).astype(F32)
    lane = s_idx
    fwd_half = lane < 2 * NH_A
    low = lane < 4 * NH_A
    hi = lax.Precision.HIGHEST
    for ci in range(tm // CHUNK):
        xc = x[ci * CHUNK:(ci + 1) * CHUNK]
        cf = jnp.dot(tri_f, xc, preferred_element_type=F32, precision=hi)
        cb = jnp.dot(tri_b, xc, preferred_element_type=F32, precision=hi)
        tot = jnp.broadcast_to(jnp.sum(xc, axis=0, keepdims=True), xc.shape)
        r_b = jnp.where(fwd_half, pltpu.roll(cf, LANES - NH_A, 1), pltpu.roll(cb, LANES - NH_A, 1))
        a = xc - r_b
        wl = xc + (pltpu.roll(tot, LANES - NH_A, 1) - r_b)
        bsel = jnp.where(fwd_half, cf, cb)
        z = lambda v: jnp.where(low, v, 0.0)
        packed = z(a) + pltpu.roll(z(wl), 32, 1) + pltpu.roll(z(bsel), 64, 1) + pltpu.roll(z(tot), 96, 1)
        gc_ref[ci * CHUNK:(ci + 1) * CHUNK, :] = packed
        gr_ref[:, ci * CHUNK:(ci + 1) * CHUNK] = packed.T


PROJ_TN = 512
BLK_QA, BLK_KA, BLK_VA, BLK_OA, BLK_QB, BLK_KB, BLK_VB, BLK_MG = 0, 1, 2, 4, 6, 8, 10, 12
N_PROJ_BLK = 16
LOG2E = 1.4426950408889634


def _swap16(x):
    n = x.shape[-1]
    lane = lax.broadcasted_iota(jnp.int32, x.shape, x.ndim - 1)
    first = (lane & 31) < 16
    return jnp.where(first, pltpu.roll(x, n - 16, x.ndim - 1), pltpu.roll(x, 16, x.ndim - 1))


def _mixer_in_kernel(x_ref, mod_ref, g_ref, w_ref, wg_ref, bg_ref, qk_ref, cos_ref, sin_ref, gsum_ref,
                     o_ref, kt_ref, gc_ref, gr_ref, *, rope):
    hx = _mod_norm(x_ref[...], g_ref[...], mod_ref[1:2, :], mod_ref[0:1, :]).astype(BF16)
    if rope:
        reps = PROJ_TN // LANES
        cos = jnp.concatenate([cos_ref[...]] * reps, axis=1)
        sin = jnp.concatenate([sin_ref[...]] * reps, axis=1)
    for blk in range(N_PROJ_BLK):
        cols = slice(blk * PROJ_TN, (blk + 1) * PROJ_TN)
        acc = jnp.dot(hx, w_ref[:, cols], preferred_element_type=F32)
        if blk < BLK_VA:
            if rope:
                acc = acc * cos + _swap16(acc) * sin
            if blk == BLK_QA:
                acc = acc * DK_A ** -0.5
        elif BLK_QB <= blk < BLK_VB:
            sq = (acc * acc).astype(BF16)
            gw = gsum_ref.shape[0]
            ss = jnp.concatenate([jnp.dot(sq[:, c0:c0 + gw], gsum_ref[...], preferred_element_type=F32)
                                  for c0 in range(0, PROJ_TN, gw)], axis=1)
            gain = qk_ref[0:1, :] if blk < BLK_KB else qk_ref[1:2, :]
            acc = acc * lax.rsqrt(ss * (1.0 / DH_B) + EPS) * gain
        if blk == BLK_KA:
            kt_ref[...] = acc.T.astype(kt_ref.dtype)
        o_ref[:, cols] = acc.astype(o_ref.dtype)
    _gate_pack(jnp.dot(hx, wg_ref[...], preferred_element_type=F32) + bg_ref[...], gc_ref, gr_ref)


def _mixer_in(x, mod3, g, w, w_g, b_g, qk_gain, cos, sin, gsum, *, rope, ctx_row=None, tm=512):
    bn, L, d = x.shape
    tm = min(tm, L)
    nblk = L // tm
    n = w.shape[1]
    mod_map = (lambda b, i: (b, 0, 0)) if ctx_row is None else (lambda b, i: (ctx_row, 0, 0))
    const = lambda b, i: (0, 0)
    tok = lambda b, i: (b * nblk + i, 0)
    resident = pl.Buffered(1)
    return pl.pallas_call(
        functools.partial(_mixer_in_kernel, rope=rope),
        grid=(bn, nblk),
        in_specs=[pl.BlockSpec((None, tm, d), lambda b, i: (b, i, 0)),
                  pl.BlockSpec((None, N_MOD, d), mod_map),
                  pl.BlockSpec((1, d), const),
                  pl.BlockSpec((d, n), const, pipeline_mode=resident),
                  pl.BlockSpec((d, LANES), const, pipeline_mode=resident),
                  pl.BlockSpec((1, LANES), const),
                  pl.BlockSpec((2, PROJ_TN), const),
                  pl.BlockSpec((tm, LANES), lambda b, i: (i, 0)),
                  pl.BlockSpec((tm, LANES), lambda b, i: (i, 0)),
                  pl.BlockSpec(gsum.shape, const, pipeline_mode=resident)],
        out_specs=[pl.BlockSpec((tm, n), tok),
                   pl.BlockSpec((PROJ_TN, tm), lambda b, i: (0, b * nblk + i)),
                   pl.BlockSpec((tm, LANES), tok),
                   pl.BlockSpec((LANES, tm), lambda b, i: (0, b * nblk + i))],
        out_shape=[jax.ShapeDtypeStruct((bn * L, n), BF16),
                   jax.ShapeDtypeStruct((PROJ_TN, bn * L), BF16),
                   jax.ShapeDtypeStruct((bn * L, LANES), F32),
                   jax.ShapeDtypeStruct((LANES, bn * L), F32)],
        compiler_params=_cparams(("parallel", "parallel")),
        name="mixer_in",
    )(x, mod3, g.reshape(1, d), w, w_g, b_g, qk_gain, cos, sin, gsum)


def _mlstm_state_kernel(kf_ref, vf_ref, gf_ref, kb_ref, vb_ref, gb_ref, c0_ref,
                        cf_ref, cb_ref, cfin_ref, st, *, nsteps, cps):
    s = pl.program_id(1)

    @pl.when(s == 0)
    def _():
        st[...] = c0_ref[...]

    lc = CHUNK
    top_k = lax.broadcasted_iota(jnp.int32, (2 * DK_A, lc), 0) < DK_A
    top = lax.broadcasted_iota(jnp.int32, (2 * DK_A, C_EXT), 0) < DK_A
    ones_blk = jnp.ones((lc, DV_A), BF16)
    for c in range(cps):
        for d, (k_ref, v_ref, g_ref, out_ref) in enumerate(((kf_ref, vf_ref, gf_ref, cf_ref),
                                                            (kb_ref, vb_ref, gb_ref, cb_ref))):
            cc = c if d == 0 else cps - 1 - c
            tok = slice(cc * lc, (cc + 1) * lc)
            g = g_ref[:, tok]
            for p in range(N_PAIR):
                h0, h1 = 2 * p, 2 * p + 1
                wl = G_WL + 16 * d
                w = jnp.where(top_k, jnp.exp(g[wl + h0:wl + h0 + 1, :]), jnp.exp(g[wl + h1:wl + h1 + 1, :]))
                kw = k_ref[p * 2 * DK_A:(p + 1) * 2 * DK_A, tok].astype(F32) * w
                kw0 = jnp.where(top_k, kw, 0.0).astype(BF16)
                kw1 = jnp.where(top_k, 0.0, kw).astype(BF16)
                vext0 = jnp.concatenate([v_ref[tok, h0 * DV_A:(h0 + 1) * DV_A], ones_blk], axis=1)
                vext1 = jnp.concatenate([v_ref[tok, h1 * DV_A:(h1 + 1) * DV_A], ones_blk], axis=1)
                upd = (jnp.dot(kw0, vext0, preferred_element_type=F32)
                       + jnp.dot(kw1, vext1, preferred_element_type=F32))
                c_old = st[d, p]
                out_ref[cc, p, 0] = c_old[:DK_A].astype(out_ref.dtype)
                out_ref[cc, p, 1] = c_old[DK_A:].astype(out_ref.dtype)
                tl = G_TOT + 16 * d
                dec = jnp.where(top, jnp.exp(g[tl + h0:tl + h0 + 1, 0:1]), jnp.exp(g[tl + h1:tl + h1 + 1, 0:1]))
                st[d, p] = dec * c_old + upd

    @pl.when(s == nsteps - 1)
    def _():
        cfin_ref[...] = st[...]


def _mlstm_states(proj, kt, gr, c0, *, bn, seq_len):
    nc = seq_len // CHUNK
    cps = min(MLSTM_CPS, nc)
    nsteps = nc // cps
    lb = cps * CHUNK
    vcol = BLK_VA * PROJ_TN // (NH_A * DV_A)
    fwd = lambda b, s: b * nsteps + s
    bwd = lambda b, s: b * nsteps + nsteps - 1 - s
    st_shape = (2, N_PAIR, 2 * DK_A, C_EXT)
    out_blk = (None, cps, N_PAIR, 2, DK_A, C_EXT)
    return pl.pallas_call(
        functools.partial(_mlstm_state_kernel, nsteps=nsteps, cps=cps),
        grid=(bn, nsteps),
        in_specs=[pl.BlockSpec((NH_A * DK_A, lb), lambda b, s: (0, fwd(b, s))),
                  pl.BlockSpec((lb, NH_A * DV_A), lambda b, s: (fwd(b, s), vcol)),
                  pl.BlockSpec((LANES, lb), lambda b, s: (0, fwd(b, s))),
                  pl.BlockSpec((NH_A * DK_A, lb), lambda b, s: (0, bwd(b, s))),
                  pl.BlockSpec((lb, NH_A * DV_A), lambda b, s: (bwd(b, s), vcol)),
                  pl.BlockSpec((LANES, lb), lambda b, s: (0, bwd(b, s))),
                  pl.BlockSpec((None,) + st_shape, lambda b, s: (b, 0, 0, 0, 0))],
        out_specs=[pl.BlockSpec(out_blk, lambda b, s: (b, s, 0, 0, 0, 0)),
                   pl.BlockSpec(out_blk, lambda b, s: (b, nsteps - 1 - s, 0, 0, 0, 0)),
                   pl.BlockSpec((None,) + st_shape, lambda b, s: (b, 0, 0, 0, 0))],
        out_shape=[jax.ShapeDtypeStruct((bn, nc, N_PAIR, 2, DK_A, C_EXT), BF16),
                   jax.ShapeDtypeStruct((bn, nc, N_PAIR, 2, DK_A, C_EXT), BF16),
                   jax.ShapeDtypeStruct((bn,) + st_shape, F32)],
        scratch_shapes=[pltpu.VMEM(st_shape, F32)],
        compiler_params=_cparams(("parallel", "arbitrary")),
        name="mlstm_states",
    )(kt, proj, gr, kt, proj, gr, c0)


def _mlstm_out_kernel(q_ref, kt_ref, v_ref, gc_ref, gr_ref, cf_ref, cb_ref, h_ref):
    lc = CHUNK
    t_idx = lax.broadcasted_iota(jnp.int32, (lc, lc), 0)
    s_idx = lax.broadcasted_iota(jnp.int32, (lc, lc), 1)
    visible = (s_idx <= t_idx, s_idx >= t_idx)
    lo = lax.broadcasted_iota(jnp.int32, (lc, 2 * DK_A), 1) < DK_A
    ones_blk = jnp.ones((lc, DV_A), BF16)
    zero_c = jnp.zeros((DK_A, C_EXT), BF16)
    for c in range(q_ref.shape[0] // lc):
        tok = slice(c * lc, (c + 1) * lc)
        gc = gc_ref[tok, :]
        gr = gr_ref[:, tok]
        for p in range(N_PAIR):
            qp = q_ref[tok, p * 2 * DK_A:(p + 1) * 2 * DK_A]
            ktp = kt_ref[p * 2 * DK_A:(p + 1) * 2 * DK_A, tok]
            zq = jnp.zeros_like(qp)
            q_stack = jnp.concatenate([jnp.where(lo, qp, zq), jnp.where(lo, zq, qp)], axis=0)
            s_both = jnp.dot(q_stack, ktp, preferred_element_type=F32)
            b_rep = [[jnp.broadcast_to(gc[:, G_B + 16 * d + 2 * p + hh:G_B + 16 * d + 2 * p + hh + 1], (lc, lc))
                      for hh in range(2)] for d in range(2)]
            inter = []
            for d, c_ref in enumerate((cf_ref, cb_ref)):
                e = jnp.where(lo, jnp.exp(b_rep[d][0]), jnp.exp(b_rep[d][1]))
                qs = (qp.astype(F32) * e).astype(BF16)
                c_pair = jnp.concatenate([jnp.concatenate([c_ref[c, p, 0], zero_c], axis=1),
                                          jnp.concatenate([zero_c, c_ref[c, p, 1]], axis=1)], axis=0)
                inter.append(jnp.dot(qs, c_pair, preferred_element_type=F32))
            for hh in range(2):
                h = 2 * p + hh
                s_h = s_both[hh * lc:(hh + 1) * lc]
                vext = jnp.concatenate([v_ref[tok, h * DV_A:(h + 1) * DV_A], ones_blk], axis=1)
                acc = None
                for d in range(2):
                    a_row = gr[G_A + 16 * d + h:G_A + 16 * d + h + 1, :]
                    log_d = jnp.where(visible[d], b_rep[d][hh] + a_row, -jnp.inf)
                    pm = (s_h * jnp.exp(log_d)).astype(BF16)
                    nd = (jnp.dot(pm, vext, preferred_element_type=F32)
                          + inter[d][:, hh * C_EXT:(hh + 1) * C_EXT])
                    hd = nd[:, :DV_A] / jnp.maximum(jnp.abs(nd[:, DV_A:]), 1.0)
                    acc = hd if acc is None else acc + hd
                h_ref[tok, h * DV_A:(h + 1) * DV_A] = acc.astype(h_ref.dtype)


def _mlstm_outputs(proj, kt, gc, gr, cf, cb, *, bn, seq_len):
    cps = min(MLSTM_CPS, seq_len // CHUNK)
    lc = cps * CHUNK
    nc = seq_len // lc
    qcol = BLK_QA * PROJ_TN // (NH_A * DK_A)
    vcol = BLK_VA * PROJ_TN // (NH_A * DV_A)
    tok = lambda b, c: b * nc + c
    st_blk = (None, cps, N_PAIR, 2, DK_A, C_EXT)
    st_map = lambda b, c: (b, c, 0, 0, 0, 0)
    return pl.pallas_call(
        _mlstm_out_kernel,
        grid=(bn, nc),
        in_specs=[pl.BlockSpec((lc, NH_A * DK_A), lambda b, c: (tok(b, c), qcol)),
                  pl.BlockSpec((NH_A * DK_A, lc), lambda b, c: (0, tok(b, c))),
                  pl.BlockSpec((lc, NH_A * DV_A), lambda b, c: (tok(b, c), vcol)),
                  pl.BlockSpec((lc, LANES), lambda b, c: (tok(b, c), 0)),
                  pl.BlockSpec((LANES, lc), lambda b, c: (0, tok(b, c))),
                  pl.BlockSpec(st_blk, st_map),
                  pl.BlockSpec(st_blk, st_map)],
        out_specs=pl.BlockSpec((lc, NH_A * DV_A), lambda b, c: (tok(b, c), 0)),
        out_shape=jax.ShapeDtypeStruct((bn * seq_len, NH_A * DV_A), F32),
        compiler_params=_cparams(("parallel", "parallel")),
        name="mlstm_outputs",
    )(proj, kt, proj, gc, gr, cf, cb)


NAT_QB = 4
NAT_SAFE_RANGE = 100.0


def _natten_kernel(shift_ref, q_ref, k_ref, v_ref, kc_ref, vc_ref, ba_ref, bm_ref, bb_ref, o_ref, *,
                   seq_len, bounded):
    i2 = pl.program_id(2)
    nq = QROWS * GRID_W
    nk = KROWS * GRID_W
    lctx = kc_ref.shape[0]
    pw = 2 * DH_B
    nt = (((1,), (1,)), ((), ()))
    lane_q = lax.broadcasted_iota(jnp.int32, (nq, pw), 1)
    lane_c = lax.broadcasted_iota(jnp.int32, (lctx, pw), 1)
    top_o = lax.broadcasted_iota(jnp.int32, (pw, nq), 0) < DH_B
    spare = (DH_B, 0)

    def with_ones_row(vt, h):
        row = lax.broadcasted_iota(jnp.int32, vt.shape, 0)
        own = (row < DH_B) if h == 0 else (row >= DH_B)
        return jnp.where(own, vt, (row == spare[h]).astype(vt.dtype))

    kc = kc_ref[...]
    neg_shift = jnp.broadcast_to(-shift_ref[...], (lctx, pw)).astype(BF16)
    kc_pair = [jnp.where(lane_c == spare[h], neg_shift, kc) for h in range(2)]
    vct = vc_ref[...].T
    vct_pair = [with_ones_row(vct, h) for h in range(2)]
    zq = jnp.zeros((nq, pw), BF16)
    bias_refs = (ba_ref,) + (bm_ref,) * (NAT_QB - 2) + (bb_ref,)
    for j, bias_ref in enumerate(bias_refs):
        blk = NAT_QB * i2 + j
        start = jnp.clip(blk * nq - (WIN_R // 2) * GRID_W, 0, seq_len - nk)
        start = pl.multiple_of(start, GRID_W)
        q = q_ref[j * nq:(j + 1) * nq, :]
        kb = k_ref[pl.ds(start, nk), :]
        vbt = v_ref[pl.ds(start, nk), :].T
        outs = []
        for h in range(2):
            own_q = (lane_q < DH_B) if h == 0 else (lane_q >= DH_B)
            qh = jnp.where(own_q, q, zq)
            qh_ctx = jnp.where(own_q, q, (lane_q == spare[h]).astype(BF16))
            s_win = lax.dot_general(kb, qh, nt, preferred_element_type=F32) + bias_ref[h]
            s_ctx = lax.dot_general(kc_pair[h], qh_ctx, nt, preferred_element_type=F32)
            if not bounded:
                m = jnp.maximum(jnp.max(s_win, axis=0, keepdims=True), jnp.max(s_ctx, axis=0, keepdims=True))
                s_win, s_ctx = s_win - m, s_ctx - m
            p_win = jnp.exp2(s_win).astype(BF16)
            p_ctx = jnp.exp2(s_ctx).astype(BF16)
            o = (jnp.dot(with_ones_row(vbt, h), p_win, preferred_element_type=F32)
                 + jnp.dot(vct_pair[h], p_ctx, preferred_element_type=F32))
            outs.append(o / o[spare[h]:spare[h] + 1, :])
        o_ref[j * nq:(j + 1) * nq, :] = jnp.where(top_o, outs[0], outs[1]).T.astype(o_ref.dtype)


def _natten_patterns(rows):
    nblk = rows // QROWS
    pats = []
    for blk in (0, 1 if nblk > 2 else 0, nblk - 1):
        r0 = blk * QROWS
        k0 = int(np.clip(r0 - WIN_R // 2, 0, rows - KROWS))
        qr = r0 + np.arange(QROWS)
        kr = k0 + np.arange(KROWS)
        wr0 = np.clip(qr - WIN_R // 2, 0, rows - WIN_R)
        valid = (kr[None, :] >= wr0[:, None]) & (kr[None, :] < wr0[:, None] + WIN_R)
        dr = np.clip(kr[None, :] - qr[:, None] + WIN_R - 1, 0, 2 * WIN_R - 2)
        pats.append((valid, dr))
    return pats


def _bias_kernel(t_ref, o_ref, *, patterns):
    pat = pl.program_id(0)
    for ps, (valid, dr) in enumerate(patterns):
        @pl.when(pat == ps)
        def _():
            for xq in range(QROWS):
                for yk in range(KROWS):
                    if valid[xq, yk]:
                        blk = t_ref[int(dr[xq, yk])]
                    else:
                        blk = jnp.full((GRID_W, GRID_W), -jnp.inf, F32)
                    o_ref[yk * GRID_W:(yk + 1) * GRID_W, xq * GRID_W:(xq + 1) * GRID_W] = blk


def _natten_shift(rpb, qn_g, kn_g):
    qmax = jnp.max(jnp.abs(qn_g)) * (DH_B ** -0.5 * LOG2E) * DH_B ** 0.5
    kmax = jnp.max(jnp.abs(kn_g)) * DH_B ** 0.5
    bound = 1.02 * qmax * kmax
    bias_hi = jnp.maximum(jnp.max(rpb) * LOG2E, 0.0)
    bias_lo = jnp.minimum(jnp.min(rpb) * LOG2E, 0.0)
    shift = jnp.ceil(bound + bias_hi)
    spread = shift + bound - bias_lo
    bounded = jnp.logical_and(spread <= NAT_SAFE_RANGE, shift <= 256.0)
    return jnp.where(bounded, shift, 0.0).astype(F32), bounded


def _natten_bias(rpb, rows, shift):
    nh = rpb.shape[0]
    qc = np.arange(GRID_W)
    kc = np.arange(GRID_W)
    wc0 = np.clip(qc - WIN_C // 2, 0, GRID_W - WIN_C)
    in_c = (kc[None, :] >= wc0[:, None]) & (kc[None, :] < wc0[:, None] + WIN_C)
    dc = np.clip(kc[None, :] - qc[:, None] + WIN_C - 1, 0, 2 * WIN_C - 2)
    sel_c = jnp.asarray(dc[:, :, None] == np.arange(2 * WIN_C - 1), F32)
    tcol = jnp.einsum('hab,uvb->havu', rpb.astype(F32) * LOG2E, sel_c, precision=lax.Precision.HIGHEST)
    tcol = jnp.where(in_c.T[None, None], tcol - shift, -jnp.inf)
    nq, nk = QROWS * GRID_W, KROWS * GRID_W
    n_dr = 2 * WIN_R - 1
    return pl.pallas_call(
        functools.partial(_bias_kernel, patterns=_natten_patterns(rows)),
        grid=(3, nh),
        in_specs=[pl.BlockSpec((None, n_dr, GRID_W, GRID_W), lambda p, h: (h, 0, 0, 0))],
        out_specs=pl.BlockSpec((None, None, nk, nq), lambda p, h: (p, h, 0, 0)),
        out_shape=jax.ShapeDtypeStruct((3, nh, nk, nq), F32),
        compiler_params=_cparams(("arbitrary", "arbitrary")),
        name="natten_bias",
    )(tcol)


def _natten(px, pc, bias, shift, *, bn, seq_len, lctx, bounded):
    nq = QROWS * GRID_W
    nk = KROWS * GRID_W
    nstep = seq_len // (NAT_QB * nq)
    pw = 2 * DH_B
    n_pair = NH_B // 2
    qcol, kcol, vcol = (blk * PROJ_TN // pw for blk in (BLK_QB, BLK_KB, BLK_VB))
    last = nstep - 1
    first_pat = lambda i: jnp.where(i == 0, 0, 1)
    second_pat = lambda i: jnp.where(i == last, 2, 1)
    return pl.pallas_call(
        functools.partial(_natten_kernel, seq_len=seq_len, bounded=bounded),
        grid=(bn, n_pair, nstep),
        in_specs=[pl.BlockSpec((1, pw), lambda b, p, i: (0, 0)),
                  pl.BlockSpec((NAT_QB * nq, pw), lambda b, p, i: (b * nstep + i, qcol + p)),
                  pl.BlockSpec((seq_len, pw), lambda b, p, i: (b, kcol + p)),
                  pl.BlockSpec((seq_len, pw), lambda b, p, i: (b, vcol + p)),
                  pl.BlockSpec((lctx, pw), lambda b, p, i: (b, kcol + p)),
                  pl.BlockSpec((lctx, pw), lambda b, p, i: (b, vcol + p)),
                  pl.BlockSpec((None, 2, nk, nq), lambda b, p, i: (first_pat(i), p, 0, 0)),
                  pl.BlockSpec((None, 2, nk, nq), lambda b, p, i: (1, p, 0, 0)),
                  pl.BlockSpec((None, 2, nk, nq), lambda b, p, i: (second_pat(i), p, 0, 0))],
        out_specs=pl.BlockSpec((NAT_QB * nq, pw), lambda b, p, i: (b * nstep + i, p)),
        out_shape=jax.ShapeDtypeStruct((bn * seq_len, NH_B * DH_B), BF16),
        compiler_params=_cparams(("parallel", "parallel", "arbitrary")),
        name="natten_bounded" if bounded else "natten",
    )(jnp.full((1, pw), shift, F32), px, px, px, pc, pc, bias, bias, bias)


def _merge_kernel(h_ref, oa_ref, na_ref, mg_ref, x_ref, mod_ref, gn_ref, wa_ref, wb_ref, wo_ref, o_ref):
    h = h_ref[...]
    parts = []
    for hd in range(NH_A):
        hh = h[:, hd * DV_A:(hd + 1) * DV_A]
        ms = jnp.mean(hh * hh, axis=-1, keepdims=True)
        parts.append(hh * lax.rsqrt(ms + EPS))
    ha = jnp.concatenate(parts, axis=1) * gn_ref[...]
    ha = (ha * jax.nn.sigmoid(oa_ref[...].astype(F32))).astype(BF16)
    d = wa_ref.shape[1]
    gates = jax.nn.sigmoid(mg_ref[...].astype(F32))
    t = (gates[:, :d] * jnp.dot(ha, wa_ref[...], preferred_element_type=F32)
         + gates[:, d:] * jnp.dot(na_ref[...], wb_ref[...], preferred_element_type=F32))
    mix = jnp.dot(t.astype(BF16), wo_ref[...], preferred_element_type=F32)
    g1 = mod_ref[2:3, :]
    o_ref[...] = x_ref[...] + g1 * mix


def _merge(h, proj, na, x, mod3, gn, wa, wb, wo, tm=512):
    bn, S, d = x.shape
    tm = min(tm, S)
    nblk = S // tm
    d_a = h.shape[-1]
    d_b = na.shape[-1]
    oa_blk = BLK_OA * PROJ_TN // d_a
    mg_blk = BLK_MG * PROJ_TN // (2 * d)
    const = lambda b, i: (0, 0)
    return pl.pallas_call(
        _merge_kernel,
        grid=(bn, nblk),
        in_specs=[pl.BlockSpec((tm, d_a), lambda b, i: (b * nblk + i, 0)),
                  pl.BlockSpec((tm, d_a), lambda b, i: (b * nblk + i, oa_blk)),
                  pl.BlockSpec((tm, d_b), lambda b, i: (b * nblk + i, 0)),
                  pl.BlockSpec((tm, 2 * d), lambda b, i: (b * nblk + i, mg_blk)),
                  pl.BlockSpec((None, tm, d), lambda b, i: (b, i, 0)),
                  pl.BlockSpec((None, N_MOD, d), lambda b, i: (b, 0, 0)),
                  pl.BlockSpec((1, d_a), const),
                  pl.BlockSpec((d_a, d), const, pipeline_mode=pl.Buffered(1)),
                  pl.BlockSpec((d_b, d), const, pipeline_mode=pl.Buffered(1)),
                  pl.BlockSpec((d, d), const, pipeline_mode=pl.Buffered(1))],
        out_specs=pl.BlockSpec((None, tm, d), lambda b, i: (b, i, 0)),
        out_shape=jax.ShapeDtypeStruct((bn, S, d), F32),
        compiler_params=_cparams(("parallel", "parallel")),
        name="merge",
    )(h, proj, na, proj, x, mod3, gn, wa, wb, wo)


def _ffn_kernel(x_ref, mod_ref, g_ref, wg_ref, wu_ref, wd_ref, o_ref):
    x = x_ref[...]
    hx = _mod_norm(x, g_ref[...], mod_ref[4:5, :], mod_ref[3:4, :]).astype(BF16)
    a = jnp.dot(hx, wg_ref[...], preferred_element_type=F32)
    u = jnp.dot(hx, wu_ref[...], preferred_element_type=F32)
    act = (a * jax.nn.sigmoid(a) * u).astype(BF16)
    f = jnp.dot(act, wd_ref[...], preferred_element_type=F32)
    o_ref[...] = x + mod_ref[5:6, :] * f


def _ffn(x, mod3, g, wg, wu, wd, tm=512):
    bn, S, d = x.shape
    tm = min(tm, S)
    dff = wg.shape[1]
    const = lambda b, i: (0, 0)
    return pl.pallas_call(
        _ffn_kernel,
        grid=(bn, S // tm),
        in_specs=[pl.BlockSpec((None, tm, d), lambda b, i: (b, i, 0)),
                  pl.BlockSpec((None, N_MOD, d), lambda b, i: (b, 0, 0)),
                  pl.BlockSpec((1, d), const),
                  pl.BlockSpec((d, dff), const, pipeline_mode=pl.Buffered(1)),
                  pl.BlockSpec((d, dff), const, pipeline_mode=pl.Buffered(1)),
                  pl.BlockSpec((dff, d), const, pipeline_mode=pl.Buffered(1))],
        out_specs=pl.BlockSpec((None, tm, d), lambda b, i: (b, i, 0)),
        out_shape=jax.ShapeDtypeStruct((bn, S, d), F32),
        compiler_params=_cparams(("parallel", "parallel")),
        name="ffn",
    )(x, mod3, g.reshape(1, d), wg, wu, wd)


def _rope_tables(S):
    t = np.arange(S)
    row, col = t // GRID_W, t % GRID_W
    half = DK_A // 4
    inv = ROPE_THETA ** (-np.arange(half, dtype=np.float64) / half)
    ang_r = row[:, None] * inv[None, :]
    ang_c = col[:, None] * inv[None, :]
    cos = np.concatenate([np.cos(ang_r)] * 2 + [np.cos(ang_c)] * 2, axis=1)
    sin = np.concatenate([-np.sin(ang_r), np.sin(ang_r), -np.sin(ang_c), np.sin(ang_c)], axis=1)
    reps = LANES // DK_A
    return (jnp.asarray(np.tile(cos, (1, reps)), F32), jnp.asarray(np.tile(sin, (1, reps)), F32))


def kernel(x, c, ctx, c_ctx, w_mod, b_mod, norm1_g, w_in, b_gates, mlstm_norm_g, qn_g, kn_g, rpb,
           w_branch_a, w_branch_b, w_out, norm2_g, w_ffn_gate, w_ffn_up, w_ffn_down):
    bn, S, d = x.shape
    lctx = ctx.shape[1]
    rows = S // GRID_W
    depth = w_mod.shape[0]
    d_a, d_b = NH_A * DV_A, NH_B * DH_B
    dqk = NH_A * DK_A
    sizes = (dqk, dqk, d_a, d_a, 4 * NH_A, d_b, d_b, d_b, 2 * d)
    offs = np.cumsum((0,) + sizes)
    cos_t, sin_t = _rope_tables(S)
    gsum = jnp.asarray(np.kron(np.eye(MXU_TILE // DH_B), np.ones((DH_B, DH_B))), BF16)

    for l in range(depth):
        assert l == depth - 1, "context-stream update for non-final layers is not implemented"
        n_rows = -(-(bn + 1) // 8) * 8
        cc = jnp.concatenate([c, c_ctx[None, :], jnp.zeros((n_rows - bn - 1, d), F32)], axis=0)
        mod3 = _modulation(cc, w_mod[l], b_mod[l]).reshape(n_rows, N_MOD, d)

        w = w_in[l]
        seg = lambda i: w[:, int(offs[i]):int(offs[i + 1])]
        w_main = jnp.concatenate([seg(0), seg(1), seg(2), seg(3), seg(5), seg(6), seg(7), seg(8)],
                                 axis=1).astype(BF16)
        w_g = jnp.pad(seg(4), ((0, 0), (0, LANES - 4 * NH_A))).astype(BF16)
        b_g = jnp.pad(b_gates[l].reshape(1, 4 * NH_A), ((0, 0), (0, LANES - 4 * NH_A)))
        qk_gain = jnp.stack([jnp.tile(qn_g[l], PROJ_TN // DH_B) * (DH_B ** -0.5 * LOG2E),
                             jnp.tile(kn_g[l], PROJ_TN // DH_B)])

        px, ktx, gcx, grx = _mixer_in(x, mod3, norm1_g[l], w_main, w_g, b_g, qk_gain, cos_t, sin_t, gsum,
                                      rope=True)
        pc, ktc, _, grc = _mixer_in(ctx, mod3, norm1_g[l], w_main, w_g, b_g, qk_gain, cos_t, sin_t, gsum,
                                    rope=False, ctx_row=bn)

        zero_state = jnp.zeros((bn, 2, N_PAIR, 2 * DK_A, C_EXT), F32)
        _, _, st_ctx = _mlstm_states(pc, ktc, grc, zero_state, bn=bn, seq_len=lctx)
        cf, cb, _ = _mlstm_states(px, ktx, grx, st_ctx, bn=bn, seq_len=S)
        h_a = _mlstm_outputs(px, ktx, gcx, grx, cf, cb, bn=bn, seq_len=S)

        shift, bounded = _natten_shift(rpb[l], qn_g[l], kn_g[l])
        bias = _natten_bias(rpb[l], rows, shift)
        na = lax.cond(bounded,
                      functools.partial(_natten, bn=bn, seq_len=S, lctx=lctx, bounded=True),
                      functools.partial(_natten, bn=bn, seq_len=S, lctx=lctx, bounded=False),
                      px, pc, bias, shift)

        x_mid = _merge(h_a, px, na, x, mod3, mlstm_norm_g[l].reshape(1, d_a),
                       w_branch_a[l].astype(BF16), w_branch_b[l].astype(BF16), w_out[l].astype(BF16))
        x = _ffn(x_mid, mod3, norm2_g[l], w_ffn_gate[l].astype(BF16), w_ffn_up[l].astype(BF16),
                 w_ffn_down[l].astype(BF16))
    return x
```

```python
import functools

import numpy as np
import jax
import jax.numpy as jnp
from jax import lax
from jax.experimental import pallas as pl
from jax.experimental.pallas import tpu as pltpu

F32 = jnp.float32
BF16 = jnp.bfloat16

GRID_W = 64
NH_A, DK_A, DV_A = 8, 64, 128
NH_B, DH_B = 16, 64
CHUNK = 128
GATE_CAP = 15.0
WIN_R, WIN_C = 8, 16
ROPE_THETA = 10000.0
EPS = 1e-6
N_MOD = 6
LANES = 128
MXU_TILE = 256
VMEM_LIMIT = 56 * 1024 * 1024

QROWS = 4
KROWS = QROWS + WIN_R - 1

N_PAIR = NH_A // 2
MLSTM_CPS = 4
C_EXT = 2 * DV_A

G_A, G_WL, G_B, G_TOT = 0, 32, 72, 104


def _cparams(sem):
    return pltpu.CompilerParams(dimension_semantics=sem, vmem_limit_bytes=VMEM_LIMIT)


def _mod_kernel(c_ref, w_ref, b_ref, o_ref):
    c = c_ref[...]
    h = c * jax.nn.sigmoid(c)
    o_ref[...] = jnp.dot(h, w_ref[...], preferred_element_type=F32,
                         precision=lax.Precision.HIGHEST) + b_ref[...]


def _modulation(cc, w_mod, b_mod):
    rows, d = cc.shape
    n = w_mod.shape[1]
    tn = 1024
    return pl.pallas_call(
        _mod_kernel,
        grid=(n // tn,),
        in_specs=[pl.BlockSpec((rows, d), lambda j: (0, 0)),
                  pl.BlockSpec((d, tn), lambda j: (0, j)),
                  pl.BlockSpec((1, tn), lambda j: (0, j))],
        out_specs=pl.BlockSpec((rows, tn), lambda j: (0, j)),
        out_shape=jax.ShapeDtypeStruct((rows, n), F32),
        compiler_params=_cparams(("arbitrary",)),
        name="modulation",
    )(cc, w_mod, b_mod.reshape(1, n))


def _mod_norm(x, gain, scale, shift):
    ms = jnp.mean(x * x, axis=-1, keepdims=True)
    y = x * lax.rsqrt(ms + EPS) * gain
    return y * (1.0 + scale) + shift


def _gate_pack(g, gc_ref, gr_ref):
    tm = g.shape[0]
    g = GATE_CAP * jnp.tanh(g * (1.0 / GATE_CAP))
    lane_t = lax.broadcasted_iota(jnp.int32, g.shape, 1)
    is_forget = (lane_t & NH_A) != 0
    x = jnp.where(is_forget, jax.nn.log_sigmoid(g), g)
    x = jnp.where(lane_t < 4 * NH_A, x, 0.0)

    t_idx = lax.broadcasted_iota(jnp.int32, (CHUNK, CHUNK), 0)
    lane = lax.broadcasted_iota(jnp.int32, (CHUNK, CHUNK), 1)
    fwd_half = lane < 2 * NH_A
    low = lane < 4 * NH_A
    for ci in range(tm // CHUNK):
        xc = x[ci * CHUNK:(ci + 1) * CHUNK]
        cf = xc
        step = 1
        while step < CHUNK:
            cf = cf + jnp.where(t_idx >= step, pltpu.roll(cf, step, 0), 0.0)
            step *= 2
        tot = jnp.broadcast_to(cf[CHUNK - 1:CHUNK, :], xc.shape)
        cb = tot - cf + xc
        r_b = jnp.where(fwd_half, pltpu.roll(cf, LANES - NH_A, 1), pltpu.roll(cb, LANES - NH_A, 1))
        a = xc - r_b
        wl = xc + (pltpu.roll(tot, LANES - NH_A, 1) - r_b)
        bsel = jnp.where(fwd_half, cf, cb)
        z = lambda v: jnp.where(low, v, 0.0)
        packed = z(a) + pltpu.roll(z(wl), 32, 1) + pltpu.roll(z(bsel), 64, 1) + pltpu.roll(z(tot), 96, 1)
        gc_ref[ci * CHUNK:(ci + 1) * CHUNK, :] = packed
        gr_ref[:, ci * CHUNK:(ci + 1) * CHUNK] = packed.T


PROJ_TN = 512
BLK_QA, BLK_KA, BLK_VA, BLK_OA, BLK_QB, BLK_KB, BLK_VB, BLK_MG = 0, 1, 2, 4, 6, 8, 10, 12
N_PROJ_BLK = 16
PROJ_DOT_BLKS = 1
GATES_AFTER_BLK = 0
LOG2E = 1.4426950408889634


def _swap16(x):
    n = x.shape[-1]
    lane = lax.broadcasted_iota(jnp.int32, x.shape, x.ndim - 1)
    first = (lane & 31) < 16
    return jnp.where(first, pltpu.roll(x, n - 16, x.ndim - 1), pltpu.roll(x, 16, x.ndim - 1))


def _mixer_in_kernel(x_ref, mod_ref, g_ref, w_ref, wg_ref, bg_ref, qk_ref, cos_ref, sin_ref, gsum_ref,
                     o_ref, kt_ref, gc_ref, gr_ref, *, rope):
    hx = _mod_norm(x_ref[...], g_ref[...], mod_ref[1:2, :], mod_ref[0:1, :]).astype(BF16)
    if rope:
        reps = PROJ_TN // LANES
        cos = jnp.concatenate([cos_ref[...]] * reps, axis=1)
        sin = jnp.concatenate([sin_ref[...]] * reps, axis=1)
    for d0 in range(0, N_PROJ_BLK, PROJ_DOT_BLKS):
        if d0 == GATES_AFTER_BLK:
            _gate_pack(jnp.dot(hx, wg_ref[...], preferred_element_type=F32) + bg_ref[...], gc_ref, gr_ref)
        wide = jnp.dot(hx, w_ref[:, d0 * PROJ_TN:(d0 + PROJ_DOT_BLKS) * PROJ_TN], preferred_element_type=F32)
        for blk in range(d0, d0 + PROJ_DOT_BLKS):
            cols = slice(blk * PROJ_TN, (blk + 1) * PROJ_TN)
            acc = wide[:, (blk - d0) * PROJ_TN:(blk - d0 + 1) * PROJ_TN]
            if blk < BLK_VA:
                if rope:
                    acc = acc * cos + _swap16(acc) * sin
                if blk == BLK_QA:
                    acc = acc * DK_A ** -0.5
            elif BLK_QB <= blk < BLK_VB:
                sq = (acc * acc).astype(BF16)
                gw = gsum_ref.shape[0]
                ss = jnp.concatenate([jnp.dot(sq[:, c0:c0 + gw], gsum_ref[...], preferred_element_type=F32)
                                      for c0 in range(0, PROJ_TN, gw)], axis=1)
                gain = qk_ref[0:1, :] if blk < BLK_KB else qk_ref[1:2, :]
                acc = acc * lax.rsqrt(ss * (1.0 / DH_B) + EPS) * gain
            if blk == BLK_KA:
                kt_ref[...] = acc.T.astype(kt_ref.dtype)
            o_ref[:, cols] = acc.astype(o_ref.dtype)


def _mixer_in(x, mod3, g, w, w_g, b_g, qk_gain, cos, sin, gsum, *, rope, ctx_row=None, tm=512):
    bn, L, d = x.shape
    tm = min(tm, L)
    nblk = L // tm
    n = w.shape[1]
    mod_map = (lambda b, i: (b, 0, 0)) if ctx_row is None else (lambda b, i: (ctx_row, 0, 0))
    const = lambda b, i: (0, 0)
    tok = lambda b, i: (b * nblk + i, 0)
    resident = pl.Buffered(1)
    return pl.pallas_call(
        functools.partial(_mixer_in_kernel, rope=rope),
        grid=(bn, nblk),
        in_specs=[pl.BlockSpec((None, tm, d), lambda b, i: (b, i, 0)),
                  pl.BlockSpec((None, N_MOD, d), mod_map),
                  pl.BlockSpec((1, d), const),
                  pl.BlockSpec((d, n), const, pipeline_mode=resident),
                  pl.BlockSpec((d, LANES), const, pipeline_mode=resident),
                  pl.BlockSpec((1, LANES), const),
                  pl.BlockSpec((2, PROJ_TN), const),
                  pl.BlockSpec((tm, LANES), lambda b, i: (i, 0)),
                  pl.BlockSpec((tm, LANES), lambda b, i: (i, 0)),
                  pl.BlockSpec(gsum.shape, const, pipeline_mode=resident)],
        out_specs=[pl.BlockSpec((tm, n), tok),
                   pl.BlockSpec((PROJ_TN, tm), lambda b, i: (0, b * nblk + i)),
                   pl.BlockSpec((tm, LANES), tok),
                   pl.BlockSpec((LANES, tm), lambda b, i: (0, b * nblk + i))],
        out_shape=[jax.ShapeDtypeStruct((bn * L, n), BF16),
                   jax.ShapeDtypeStruct((PROJ_TN, bn * L), BF16),
                   jax.ShapeDtypeStruct((bn * L, LANES), F32),
                   jax.ShapeDtypeStruct((LANES, bn * L), F32)],
        compiler_params=_cparams(("parallel", "parallel")),
        name="mixer_in",
    )(x, mod3, g.reshape(1, d), w, w_g, b_g, qk_gain, cos, sin, gsum)


def _mlstm_state_kernel(kf_ref, vf_ref, gf_ref, kb_ref, vb_ref, gb_ref, c0_ref,
                        cf_ref, cb_ref, cfin_ref, st, *, nsteps, cps):
    s = pl.program_id(1)

    @pl.when(s == 0)
    def _():
        st[...] = c0_ref[...]

    lc = CHUNK
    top_k = lax.broadcasted_iota(jnp.int32, (2 * DK_A, lc), 0) < DK_A
    top = lax.broadcasted_iota(jnp.int32, (2 * DK_A, C_EXT), 0) < DK_A
    ones_blk = jnp.ones((lc, DV_A), BF16)
    for c in range(cps):
        for d, (k_ref, v_ref, g_ref, out_ref) in enumerate(((kf_ref, vf_ref, gf_ref, cf_ref),
                                                            (kb_ref, vb_ref, gb_ref, cb_ref))):
            cc = c if d == 0 else cps - 1 - c
            tok = slice(cc * lc, (cc + 1) * lc)
            g = g_ref[:, tok]
            for p in range(N_PAIR):
                h0, h1 = 2 * p, 2 * p + 1
                wl = G_WL + 16 * d
                w = jnp.where(top_k, jnp.exp(g[wl + h0:wl + h0 + 1, :]), jnp.exp(g[wl + h1:wl + h1 + 1, :]))
                kw = k_ref[p * 2 * DK_A:(p + 1) * 2 * DK_A, tok].astype(F32) * w
                kw0 = jnp.where(top_k, kw, 0.0).astype(BF16)
                kw1 = jnp.where(top_k, 0.0, kw).astype(BF16)
                vext0 = jnp.concatenate([v_ref[tok, h0 * DV_A:(h0 + 1) * DV_A], ones_blk], axis=1)
                vext1 = jnp.concatenate([v_ref[tok, h1 * DV_A:(h1 + 1) * DV_A], ones_blk], axis=1)
                upd = (jnp.dot(kw0, vext0, preferred_element_type=F32)
                       + jnp.dot(kw1, vext1, preferred_element_type=F32))
                c_old = st[d, p]
                out_ref[cc, p, 0] = c_old[:DK_A].astype(out_ref.dtype)
                out_ref[cc, p, 1] = c_old[DK_A:].astype(out_ref.dtype)
                tl = G_TOT + 16 * d
                dec = jnp.where(top, jnp.exp(g[tl + h0:tl + h0 + 1, 0:1]), jnp.exp(g[tl + h1:tl + h1 + 1, 0:1]))
                st[d, p] = dec * c_old + upd

    @pl.when(s == nsteps - 1)
    def _():
        cfin_ref[...] = st[...]


def _mlstm_states(proj, kt, gr, c0, *, bn, seq_len):
    nc = seq_len // CHUNK
    cps = min(MLSTM_CPS, nc)
    nsteps = nc // cps
    lb = cps * CHUNK
    vcol = BLK_VA * PROJ_TN // (NH_A * DV_A)
    fwd = lambda b, s: b * nsteps + s
    bwd = lambda b, s: b * nsteps + nsteps - 1 - s
    st_shape = (2, N_PAIR, 2 * DK_A, C_EXT)
    out_blk = (None, cps, N_PAIR, 2, DK_A, C_EXT)
    return pl.pallas_call(
        functools.partial(_mlstm_state_kernel, nsteps=nsteps, cps=cps),
        grid=(bn, nsteps),
        in_specs=[pl.BlockSpec((NH_A * DK_A, lb), lambda b, s: (0, fwd(b, s))),
                  pl.BlockSpec((lb, NH_A * DV_A), lambda b, s: (fwd(b, s), vcol)),
                  pl.BlockSpec((LANES, lb), lambda b, s: (0, fwd(b, s))),
                  pl.BlockSpec((NH_A * DK_A, lb), lambda b, s: (0, bwd(b, s))),
                  pl.BlockSpec((lb, NH_A * DV_A), lambda b, s: (bwd(b, s), vcol)),
                  pl.BlockSpec((LANES, lb), lambda b, s: (0, bwd(b, s))),
                  pl.BlockSpec((None,) + st_shape, lambda b, s: (b, 0, 0, 0, 0))],
        out_specs=[pl.BlockSpec(out_blk, lambda b, s: (b, s, 0, 0, 0, 0)),
                   pl.BlockSpec(out_blk, lambda b, s: (b, nsteps - 1 - s, 0, 0, 0, 0)),
                   pl.BlockSpec((None,) + st_shape, lambda b, s: (b, 0, 0, 0, 0))],
        out_shape=[jax.ShapeDtypeStruct((bn, nc, N_PAIR, 2, DK_A, C_EXT), BF16),
                   jax.ShapeDtypeStruct((bn, nc, N_PAIR, 2, DK_A, C_EXT), BF16),
                   jax.ShapeDtypeStruct((bn,) + st_shape, F32)],
        scratch_shapes=[pltpu.VMEM(st_shape, F32)],
        compiler_params=_cparams(("parallel", "arbitrary")),
        name="mlstm_states",
    )(kt, proj, gr, kt, proj, gr, c0)


def _mlstm_out_kernel(q_ref, kt_ref, v_ref, gc_ref, gr_ref, cf_ref, cb_ref, h_ref):
    lc = CHUNK
    t_idx = lax.broadcasted_iota(jnp.int32, (lc, lc), 0)
    s_idx = lax.broadcasted_iota(jnp.int32, (lc, lc), 1)
    visible = (s_idx <= t_idx, s_idx >= t_idx)
    lo = lax.broadcasted_iota(jnp.int32, (lc, 2 * DK_A), 1) < DK_A
    ones_blk = jnp.ones((lc, DV_A), BF16)
    zero_c = jnp.zeros((DK_A, C_EXT), BF16)
    for c in range(q_ref.shape[0] // lc):
        tok = slice(c * lc, (c + 1) * lc)
        gc = gc_ref[tok, :]
        gr = gr_ref[:, tok]
        for p in range(N_PAIR):
            qp = q_ref[tok, p * 2 * DK_A:(p + 1) * 2 * DK_A]
            ktp = kt_ref[p * 2 * DK_A:(p + 1) * 2 * DK_A, tok]
            zq = jnp.zeros_like(qp)
            q_stack = jnp.concatenate([jnp.where(lo, qp, zq), jnp.where(lo, zq, qp)], axis=0)
            s_both = jnp.dot(q_stack, ktp, preferred_element_type=F32)
            b_rep = [[jnp.broadcast_to(gc[:, G_B + 16 * d + 2 * p + hh:G_B + 16 * d + 2 * p + hh + 1], (lc, lc))
                      for hh in range(2)] for d in range(2)]
            inter = []
            for d, c_ref in enumerate((cf_ref, cb_ref)):
                e = jnp.where(lo, jnp.exp(b_rep[d][0]), jnp.exp(b_rep[d][1]))
                qs = (qp.astype(F32) * e).astype(BF16)
                c_pair = jnp.concatenate([jnp.concatenate([c_ref[c, p, 0], zero_c], axis=1),
                                          jnp.concatenate([zero_c, c_ref[c, p, 1]], axis=1)], axis=0)
                inter.append(jnp.dot(qs, c_pair, preferred_element_type=F32))
            for hh in range(2):
                h = 2 * p + hh
                s_h = s_both[hh * lc:(hh + 1) * lc]
                vext = jnp.concatenate([v_ref[tok, h * DV_A:(h + 1) * DV_A], ones_blk], axis=1)
                acc = None
                for d in range(2):
                    a_row = gr[G_A + 16 * d + h:G_A + 16 * d + h + 1, :]
                    log_d = jnp.where(visible[d], b_rep[d][hh] + a_row, -jnp.inf)
                    pm = (s_h * jnp.exp(log_d)).astype(BF16)
                    nd = (jnp.dot(pm, vext, preferred_element_type=F32)
                          + inter[d][:, hh * C_EXT:(hh + 1) * C_EXT])
                    hd = nd[:, :DV_A] / jnp.maximum(jnp.abs(nd[:, DV_A:]), 1.0)
                    acc = hd if acc is None else acc + hd
                h_ref[tok, h * DV_A:(h + 1) * DV_A] = acc.astype(h_ref.dtype)


def _mlstm_outputs(proj, kt, gc, gr, cf, cb, *, bn, seq_len):
    cps = min(MLSTM_CPS, seq_len // CHUNK)
    lc = cps * CHUNK
    nc = seq_len // lc
    qcol = BLK_QA * PROJ_TN // (NH_A * DK_A)
    vcol = BLK_VA * PROJ_TN // (NH_A * DV_A)
    tok = lambda b, c: b * nc + c
    st_blk = (None, cps, N_PAIR, 2, DK_A, C_EXT)
    st_map = lambda b, c: (b, c, 0, 0, 0, 0)
    return pl.pallas_call(
        _mlstm_out_kernel,
        grid=(bn, nc),
        in_specs=[pl.BlockSpec((lc, NH_A * DK_A), lambda b, c: (tok(b, c), qcol)),
                  pl.BlockSpec((NH_A * DK_A, lc), lambda b, c: (0, tok(b, c))),
                  pl.BlockSpec((lc, NH_A * DV_A), lambda b, c: (tok(b, c), vcol)),
                  pl.BlockSpec((lc, LANES), lambda b, c: (tok(b, c), 0)),
                  pl.BlockSpec((LANES, lc), lambda b, c: (0, tok(b, c))),
                  pl.BlockSpec(st_blk, st_map),
                  pl.BlockSpec(st_blk, st_map)],
        out_specs=pl.BlockSpec((lc, NH_A * DV_A), lambda b, c: (tok(b, c), 0)),
        out_shape=jax.ShapeDtypeStruct((bn * seq_len, NH_A * DV_A), F32),
        compiler_params=_cparams(("parallel", "parallel")),
        name="mlstm_outputs",
    )(proj, kt, proj, gc, gr, cf, cb)


NAT_QB = 8
NAT_SAFE_RANGE = 100.0


def _natten_kernel(shift_ref, q_ref, k_ref, v_ref, kc_ref, vc_ref, ba_ref, bm_ref, bb_ref, o_ref, *,
                   seq_len, bounded):
    i2 = pl.program_id(2)
    nq = QROWS * GRID_W
    nk = KROWS * GRID_W
    lctx = kc_ref.shape[0]
    pw = 2 * DH_B
    nt = (((1,), (1,)), ((), ()))
    lane_q = lax.broadcasted_iota(jnp.int32, (nq, pw), 1)
    lane_c = lax.broadcasted_iota(jnp.int32, (lctx, pw), 1)
    top_o = lax.broadcasted_iota(jnp.int32, (pw, nq), 0) < DH_B
    spare = (DH_B, 0)

    def with_ones_row(vt, h):
        row = lax.broadcasted_iota(jnp.int32, vt.shape, 0)
        own = (row < DH_B) if h == 0 else (row >= DH_B)
        return jnp.where(own, vt, (row == spare[h]).astype(vt.dtype))

    kc = kc_ref[...]
    neg_shift = jnp.broadcast_to(-shift_ref[...], (lctx, pw)).astype(BF16)
    kc_pair = [jnp.where(lane_c == spare[h], neg_shift, kc) for h in range(2)]
    vct = vc_ref[...].T
    vct_pair = [with_ones_row(vct, h) for h in range(2)]
    zq = jnp.zeros((nq, pw), BF16)
    qb = q_ref.shape[0] // nq
    bias_refs = (ba_ref,) + (bm_ref,) * (qb - 2) + (bb_ref,)
    for j, bias_ref in enumerate(bias_refs):
        blk = qb * i2 + j
        start = jnp.clip(blk * nq - (WIN_R // 2) * GRID_W, 0, seq_len - nk)
        start = pl.multiple_of(start, GRID_W)
        q = q_ref[j * nq:(j + 1) * nq, :]
        kb = k_ref[pl.ds(start, nk), :]
        vbt = v_ref[pl.ds(start, nk), :].T
        outs = []
        for h in range(2):
            own_q = (lane_q < DH_B) if h == 0 else (lane_q >= DH_B)
            qh = jnp.where(own_q, q, zq)
            qh_ctx = jnp.where(own_q, q, (lane_q == spare[h]).astype(BF16))
            s_win = lax.dot_general(kb, qh, nt, preferred_element_type=F32) + bias_ref[h]
            s_ctx = lax.dot_general(kc_pair[h], qh_ctx, nt, preferred_element_type=F32)
            if not bounded:
                m = jnp.maximum(jnp.max(s_win, axis=0, keepdims=True), jnp.max(s_ctx, axis=0, keepdims=True))
                s_win, s_ctx = s_win - m, s_ctx - m
            p_win = jnp.exp2(s_win).astype(BF16)
            p_ctx = jnp.exp2(s_ctx).astype(BF16)
            o = (jnp.dot(with_ones_row(vbt, h), p_win, preferred_element_type=F32)
                 + jnp.dot(vct_pair[h], p_ctx, preferred_element_type=F32))
            outs.append(o / o[spare[h]:spare[h] + 1, :])
        o_ref[j * nq:(j + 1) * nq, :] = jnp.where(top_o, outs[0], outs[1]).T.astype(o_ref.dtype)


def _natten_patterns(rows):
    nblk = rows // QROWS
    pats = []
    for blk in (0, 1 if nblk > 2 else 0, nblk - 1):
        r0 = blk * QROWS
        k0 = int(np.clip(r0 - WIN_R // 2, 0, rows - KROWS))
        qr = r0 + np.arange(QROWS)
        kr = k0 + np.arange(KROWS)
        wr0 = np.clip(qr - WIN_R // 2, 0, rows - WIN_R)
        valid = (kr[None, :] >= wr0[:, None]) & (kr[None, :] < wr0[:, None] + WIN_R)
        dr = np.clip(kr[None, :] - qr[:, None] + WIN_R - 1, 0, 2 * WIN_R - 2)
        pats.append((valid, dr))
    return pats


def _bias_kernel(t_ref, o_ref, *, patterns):
    pat = pl.program_id(0)
    for ps, (valid, dr) in enumerate(patterns):
        @pl.when(pat == ps)
        def _():
            for xq in range(QROWS):
                for yk in range(KROWS):
                    if valid[xq, yk]:
                        blk = t_ref[int(dr[xq, yk])]
                    else:
                        blk = jnp.full((GRID_W, GRID_W), -jnp.inf, F32)
                    o_ref[yk * GRID_W:(yk + 1) * GRID_W, xq * GRID_W:(xq + 1) * GRID_W] = blk


def _natten_shift(rpb, qn_g, kn_g):
    qmax = jnp.max(jnp.abs(qn_g)) * (DH_B ** -0.5 * LOG2E) * DH_B ** 0.5
    kmax = jnp.max(jnp.abs(kn_g)) * DH_B ** 0.5
    bound = 1.02 * qmax * kmax
    bias_hi = jnp.maximum(jnp.max(rpb) * LOG2E, 0.0)
    bias_lo = jnp.minimum(jnp.min(rpb) * LOG2E, 0.0)
    shift = jnp.ceil(bound + bias_hi)
    spread = shift + bound - bias_lo
    bounded = jnp.logical_and(spread <= NAT_SAFE_RANGE, shift <= 256.0)
    return jnp.where(bounded, shift, 0.0).astype(F32), bounded


def _natten_bias(rpb, rows, shift):
    nh = rpb.shape[0]
    qc = np.arange(GRID_W)
    kc = np.arange(GRID_W)
    wc0 = np.clip(qc - WIN_C // 2, 0, GRID_W - WIN_C)
    in_c = (kc[None, :] >= wc0[:, None]) & (kc[None, :] < wc0[:, None] + WIN_C)
    dc = np.clip(kc[None, :] - qc[:, None] + WIN_C - 1, 0, 2 * WIN_C - 2)
    sel_c = jnp.asarray(dc[:, :, None] == np.arange(2 * WIN_C - 1), F32)
    tcol = jnp.einsum('hab,uvb->havu', rpb.astype(F32) * LOG2E, sel_c, precision=lax.Precision.HIGHEST)
    tcol = jnp.where(in_c.T[None, None], tcol - shift, -jnp.inf)
    nq, nk = QROWS * GRID_W, KROWS * GRID_W
    n_dr = 2 * WIN_R - 1
    return pl.pallas_call(
        functools.partial(_bias_kernel, patterns=_natten_patterns(rows)),
        grid=(3, nh),
        in_specs=[pl.BlockSpec((None, n_dr, GRID_W, GRID_W), lambda p, h: (h, 0, 0, 0))],
        out_specs=pl.BlockSpec((None, None, nk, nq), lambda p, h: (p, h, 0, 0)),
        out_shape=jax.ShapeDtypeStruct((3, nh, nk, nq), F32),
        compiler_params=_cparams(("arbitrary", "arbitrary")),
        name="natten_bias",
    )(tcol)


def _natten(px, pc, bias, shift, *, bn, seq_len, lctx, bounded):
    nq = QROWS * GRID_W
    nk = KROWS * GRID_W
    qb = min(NAT_QB, seq_len // nq)
    nstep = seq_len // (qb * nq)
    pw = 2 * DH_B
    n_pair = NH_B // 2
    qcol, kcol, vcol = (blk * PROJ_TN // pw for blk in (BLK_QB, BLK_KB, BLK_VB))
    last = nstep - 1
    first_pat = lambda i: jnp.where(i == 0, 0, 1)
    second_pat = lambda i: jnp.where(i == last, 2, 1)
    return pl.pallas_call(
        functools.partial(_natten_kernel, seq_len=seq_len, bounded=bounded),
        grid=(bn, n_pair, nstep),
        in_specs=[pl.BlockSpec((1, pw), lambda b, p, i: (0, 0)),
                  pl.BlockSpec((qb * nq, pw), lambda b, p, i: (b * nstep + i, qcol + p)),
                  pl.BlockSpec((seq_len, pw), lambda b, p, i: (b, kcol + p)),
                  pl.BlockSpec((seq_len, pw), lambda b, p, i: (b, vcol + p)),
                  pl.BlockSpec((lctx, pw), lambda b, p, i: (b, kcol + p)),
                  pl.BlockSpec((lctx, pw), lambda b, p, i: (b, vcol + p)),
                  pl.BlockSpec((None, 2, nk, nq), lambda b, p, i: (first_pat(i), p, 0, 0)),
                  pl.BlockSpec((None, 2, nk, nq), lambda b, p, i: (1, p, 0, 0)),
                  pl.BlockSpec((None, 2, nk, nq), lambda b, p, i: (second_pat(i), p, 0, 0))],
        out_specs=pl.BlockSpec((qb * nq, pw), lambda b, p, i: (b * nstep + i, p)),
        out_shape=jax.ShapeDtypeStruct((bn * seq_len, NH_B * DH_B), BF16),
        compiler_params=_cparams(("parallel", "parallel", "arbitrary")),
        name="natten_bounded" if bounded else "natten",
    )(jnp.full((1, pw), shift, F32), px, px, px, pc, pc, bias, bias, bias)


def _merge_kernel(h_ref, oa_ref, na_ref, mg_ref, x_ref, mod_ref, gn_ref, wa_ref, wb_ref, wo_ref, o_ref):
    h = h_ref[...]
    parts = []
    for hd in range(NH_A):
        hh = h[:, hd * DV_A:(hd + 1) * DV_A]
        ms = jnp.mean(hh * hh, axis=-1, keepdims=True)
        parts.append(hh * lax.rsqrt(ms + EPS))
    ha = jnp.concatenate(parts, axis=1) * gn_ref[...]
    ha = (ha * jax.nn.sigmoid(oa_ref[...].astype(F32))).astype(BF16)
    d = wa_ref.shape[1]
    gates = jax.nn.sigmoid(mg_ref[...].astype(F32))
    t = (gates[:, :d] * jnp.dot(ha, wa_ref[...], preferred_element_type=F32)
         + gates[:, d:] * jnp.dot(na_ref[...], wb_ref[...], preferred_element_type=F32))
    mix = jnp.dot(t.astype(BF16), wo_ref[...], preferred_element_type=F32)
    g1 = mod_ref[2:3, :]
    o_ref[...] = x_ref[...] + g1 * mix


def _merge(h, proj, na, x, mod3, gn, wa, wb, wo, tm=512):
    bn, S, d = x.shape
    tm = min(tm, S)
    nblk = S // tm
    d_a = h.shape[-1]
    d_b = na.shape[-1]
    oa_blk = BLK_OA * PROJ_TN // d_a
    mg_blk = BLK_MG * PROJ_TN // (2 * d)
    const = lambda b, i: (0, 0)
    return pl.pallas_call(
        _merge_kernel,
        grid=(bn, nblk),
        in_specs=[pl.BlockSpec((tm, d_a), lambda b, i: (b * nblk + i, 0)),
                  pl.BlockSpec((tm, d_a), lambda b, i: (b * nblk + i, oa_blk)),
                  pl.BlockSpec((tm, d_b), lambda b, i: (b * nblk + i, 0)),
                  pl.BlockSpec((tm, 2 * d), lambda b, i: (b * nblk + i, mg_blk)),
                  pl.BlockSpec((None, tm, d), lambda b, i: (b, i, 0)),
                  pl.BlockSpec((None, N_MOD, d), lambda b, i: (b, 0, 0)),
                  pl.BlockSpec((1, d_a), const),
                  pl.BlockSpec((d_a, d), const, pipeline_mode=pl.Buffered(1)),
                  pl.BlockSpec((d_b, d), const, pipeline_mode=pl.Buffered(1)),
                  pl.BlockSpec((d, d), const, pipeline_mode=pl.Buffered(1))],
        out_specs=pl.BlockSpec((None, tm, d), lambda b, i: (b, i, 0)),
        out_shape=jax.ShapeDtypeStruct((bn, S, d), F32),
        compiler_params=_cparams(("parallel", "parallel")),
        name="merge",
    )(h, proj, na, proj, x, mod3, gn, wa, wb, wo)


def _ffn_kernel(x_ref, mod_ref, g_ref, wg_ref, wu_ref, wd_ref, o_ref):
    x = x_ref[...]
    hx = _mod_norm(x, g_ref[...], mod_ref[4:5, :], mod_ref[3:4, :]).astype(BF16)
    a = jnp.dot(hx, wg_ref[...], preferred_element_type=F32)
    u = jnp.dot(hx, wu_ref[...], preferred_element_type=F32)
    act = (a * jax.nn.sigmoid(a) * u).astype(BF16)
    f = jnp.dot(act, wd_ref[...], preferred_element_type=F32)
    o_ref[...] = x + mod_ref[5:6, :] * f


def _ffn(x, mod3, g, wg, wu, wd, tm=512):
    bn, S, d = x.shape
    tm = min(tm, S)
    dff = wg.shape[1]
    const = lambda b, i: (0, 0)
    return pl.pallas_call(
        _ffn_kernel,
        grid=(bn, S // tm),
        in_specs=[pl.BlockSpec((None, tm, d), lambda b, i: (b, i, 0)),
                  pl.BlockSpec((None, N_MOD, d), lambda b, i: (b, 0, 0)),
                  pl.BlockSpec((1, d), const),
                  pl.BlockSpec((d, dff), const, pipeline_mode=pl.Buffered(1)),
                  pl.BlockSpec((d, dff), const, pipeline_mode=pl.Buffered(1)),
                  pl.BlockSpec((dff, d), const, pipeline_mode=pl.Buffered(1))],
        out_specs=pl.BlockSpec((None, tm, d), lambda b, i: (b, i, 0)),
        out_shape=jax.ShapeDtypeStruct((bn, S, d), F32),
        compiler_params=_cparams(("parallel", "parallel")),
        name="ffn",
    )(x, mod3, g.reshape(1, d), wg, wu, wd)


def _rope_tables(S):
    t = np.arange(S)
    row, col = t // GRID_W, t % GRID_W
    half = DK_A // 4
    inv = ROPE_THETA ** (-np.arange(half, dtype=np.float64) / half)
    ang_r = row[:, None] * inv[None, :]
    ang_c = col[:, None] * inv[None, :]
    cos = np.concatenate([np.cos(ang_r)] * 2 + [np.cos(ang_c)] * 2, axis=1)
    sin = np.concatenate([-np.sin(ang_r), np.sin(ang_r), -np.sin(ang_c), np.sin(ang_c)], axis=1)
    reps = LANES // DK_A
    return (jnp.asarray(np.tile(cos, (1, reps)), F32), jnp.asarray(np.tile(sin, (1, reps)), F32))


def kernel(x, c, ctx, c_ctx, w_mod, b_mod, norm1_g, w_in, b_gates, mlstm_norm_g, qn_g, kn_g, rpb,
           w_branch_a, w_branch_b, w_out, norm2_g, w_ffn_gate, w_ffn_up, w_ffn_down):
    bn, S, d = x.shape
    lctx = ctx.shape[1]
    rows = S // GRID_W
    depth = w_mod.shape[0]
    d_a, d_b = NH_A * DV_A, NH_B * DH_B
    dqk = NH_A * DK_A
    sizes = (dqk, dqk, d_a, d_a, 4 * NH_A, d_b, d_b, d_b, 2 * d)
    offs = np.cumsum((0,) + sizes)
    cos_t, sin_t = _rope_tables(S)
    gsum = jnp.asarray(np.kron(np.eye(MXU_TILE // DH_B), np.ones((DH_B, DH_B))), BF16)

    for l in range(depth):
        assert l == depth - 1, "context-stream update for non-final layers is not implemented"
        n_rows = -(-(bn + 1) // 8) * 8
        cc = jnp.concatenate([c, c_ctx[None, :], jnp.zeros((n_rows - bn - 1, d), F32)], axis=0)
        mod3 = _modulation(cc, w_mod[l], b_mod[l]).reshape(n_rows, N_MOD, d)

        w = w_in[l]
        seg = lambda i: w[:, int(offs[i]):int(offs[i + 1])]
        w_main = jnp.concatenate([seg(0), seg(1), seg(2), seg(3), seg(5), seg(6), seg(7), seg(8)],
                                 axis=1).astype(BF16)
        w_g = jnp.pad(seg(4), ((0, 0), (0, LANES - 4 * NH_A))).astype(BF16)
        b_g = jnp.pad(b_gates[l].reshape(1, 4 * NH_A), ((0, 0), (0, LANES - 4 * NH_A)))
        qk_gain = jnp.stack([jnp.tile(qn_g[l], PROJ_TN // DH_B) * (DH_B ** -0.5 * LOG2E),
                             jnp.tile(kn_g[l], PROJ_TN // DH_B)])

        px, ktx, gcx, grx = _mixer_in(x, mod3, norm1_g[l], w_main, w_g, b_g, qk_gain, cos_t, sin_t, gsum,
                                      rope=True)
        pc, ktc, _, grc = _mixer_in(ctx, mod3, norm1_g[l], w_main, w_g, b_g, qk_gain, cos_t, sin_t, gsum,
                                    rope=False, ctx_row=bn)

        zero_state = jnp.zeros((bn, 2, N_PAIR, 2 * DK_A, C_EXT), F32)
        _, _, st_ctx = _mlstm_states(pc, ktc, grc, zero_state, bn=bn, seq_len=lctx)
        cf, cb, _ = _mlstm_states(px, ktx, grx, st_ctx, bn=bn, seq_len=S)
        h_a = _mlstm_outputs(px, ktx, gcx, grx, cf, cb, bn=bn, seq_len=S)

        shift, bounded = _natten_shift(rpb[l], qn_g[l], kn_g[l])
        bias = _natten_bias(rpb[l], rows, shift)
        na = lax.cond(bounded,
                      functools.partial(_natten, bn=bn, seq_len=S, lctx=lctx, bounded=True),
                      functools.partial(_natten, bn=bn, seq_len=S, lctx=lctx, bounded=False),
                      px, pc, bias, shift)

        x_mid = _merge(h_a, px, na, x, mod3, mlstm_norm_g[l].reshape(1, d_a),
                       w_branch_a[l].astype(BF16), w_branch_b[l].astype(BF16), w_out[l].astype(BF16))
        x = _ffn(x_mid, mod3, norm2_g[l], w_ffn_gate[l].astype(BF16), w_ffn_up[l].astype(BF16),
                 w_ffn_down[l].astype(BF16))
    return x
```

```python
import functools

import numpy as np
import jax
import jax.numpy as jnp
from jax import lax
from jax.experimental import pallas as pl
from jax.experimental.pallas import tpu as pltpu

F32 = jnp.float32
BF16 = jnp.bfloat16

GRID_W = 64
NH_A, DK_A, DV_A = 8, 64, 128
NH_B, DH_B = 16, 64
CHUNK = 128
GATE_CAP = 15.0
WIN_R, WIN_C = 8, 16
ROPE_THETA = 10000.0
EPS = 1e-6
N_MOD = 6
LANES = 128
MXU_TILE = 256
VMEM_LIMIT = 56 * 1024 * 1024

QROWS = 4
KROWS = QROWS + WIN_R - 1

N_PAIR = NH_A // 2
MLSTM_CPS = 4
C_EXT = 2 * DV_A

G_A, G_WL, G_B, G_TOT = 0, 32, 72, 104


def _cparams(sem):
    return pltpu.CompilerParams(dimension_semantics=sem, vmem_limit_bytes=VMEM_LIMIT)


def _mod_kernel(c_ref, w_ref, b_ref, o_ref):
    c = c_ref[...]
    h = c * jax.nn.sigmoid(c)
    o_ref[...] = jnp.dot(h, w_ref[...], preferred_element_type=F32,
                         precision=lax.Precision.HIGHEST) + b_ref[...]


def _modulation(cc, w_mod, b_mod):
    rows, d = cc.shape
    n = w_mod.shape[1]
    tn = 1024
    return pl.pallas_call(
        _mod_kernel,
        grid=(n // tn,),
        in_specs=[pl.BlockSpec((rows, d), lambda j: (0, 0)),
                  pl.BlockSpec((d, tn), lambda j: (0, j)),
                  pl.BlockSpec((1, tn), lambda j: (0, j))],
        out_specs=pl.BlockSpec((rows, tn), lambda j: (0, j)),
        out_shape=jax.ShapeDtypeStruct((rows, n), F32),
        compiler_params=_cparams(("arbitrary",)),
        name="modulation",
    )(cc, w_mod, b_mod.reshape(1, n))


def _mod_norm(x, gain, scale, shift):
    ms = jnp.mean(x * x, axis=-1, keepdims=True)
    y = x * lax.rsqrt(ms + EPS) * gain
    return y * (1.0 + scale) + shift


def _gate_pack(g, gc_ref, gr_ref, r0):
    tm = g.shape[0]
    g = GATE_CAP * jnp.tanh(g * (1.0 / GATE_CAP))
    lane_t = lax.broadcasted_iota(jnp.int32, g.shape, 1)
    is_forget = (lane_t & NH_A) != 0
    x = jnp.where(is_forget, jax.nn.log_sigmoid(g), g)
    x = jnp.where(lane_t < 4 * NH_A, x, 0.0)

    t_idx = lax.broadcasted_iota(jnp.int32, (CHUNK, CHUNK), 0)
    lane = lax.broadcasted_iota(jnp.int32, (CHUNK, CHUNK), 1)
    fwd_half = lane < 2 * NH_A
    low = lane < 4 * NH_A
    for ci in range(tm // CHUNK):
        xc = x[ci * CHUNK:(ci + 1) * CHUNK]
        cf = xc
        step = 1
        while step < CHUNK:
            cf = cf + jnp.where(t_idx >= step, pltpu.roll(cf, step, 0), 0.0)
            step *= 2
        tot = jnp.broadcast_to(cf[CHUNK - 1:CHUNK, :], xc.shape)
        cb = tot - cf + xc
        r_b = jnp.where(fwd_half, pltpu.roll(cf, LANES - NH_A, 1), pltpu.roll(cb, LANES - NH_A, 1))
        a = xc - r_b
        wl = xc + (pltpu.roll(tot, LANES - NH_A, 1) - r_b)
        bsel = jnp.where(fwd_half, cf, cb)
        z = lambda v: jnp.where(low, v, 0.0)
        packed = z(a) + pltpu.roll(z(wl), 32, 1) + pltpu.roll(z(bsel), 64, 1) + pltpu.roll(z(tot), 96, 1)
        gc_ref[r0 + ci * CHUNK:r0 + (ci + 1) * CHUNK, :] = packed
        gr_ref[:, r0 + ci * CHUNK:r0 + (ci + 1) * CHUNK] = packed.T


PROJ_TN = 512
BLK_QA, BLK_KA, BLK_VA, BLK_OA, BLK_QB, BLK_KB, BLK_VB, BLK_MG = 0, 1, 2, 4, 6, 8, 10, 12
N_PROJ_BLK = 16
MIX_ROW_PARTS = 1
CTX_BLOCKS = (BLK_KA, BLK_VA, BLK_VA + 1, BLK_KB, BLK_KB + 1, BLK_VB, BLK_VB + 1)
LOG2E = 1.4426950408889634


def _swap16(x):
    n = x.shape[-1]
    lane = lax.broadcasted_iota(jnp.int32, x.shape, x.ndim - 1)
    first = (lane & 31) < 16
    return jnp.where(first, pltpu.roll(x, n - 16, x.ndim - 1), pltpu.roll(x, 16, x.ndim - 1))


def _mixer_in_kernel(x_ref, mod_ref, g_ref, w_ref, wg_ref, bg_ref, qk_ref, cos_ref, sin_ref, gsum_ref,
                     o_ref, kt_ref, gc_ref, gr_ref, *, rope, blocks):
    tm = x_ref.shape[0]
    part = tm // MIX_ROW_PARTS
    for r0 in range(0, tm, part):
        rows = slice(r0, r0 + part)
        hx = _mod_norm(x_ref[rows, :], g_ref[...], mod_ref[1:2, :], mod_ref[0:1, :]).astype(BF16)
        _gate_pack(jnp.dot(hx, wg_ref[...], preferred_element_type=F32) + bg_ref[...], gc_ref, gr_ref, r0)
        if rope:
            reps = PROJ_TN // LANES
            cos = jnp.concatenate([cos_ref[rows, :]] * reps, axis=1)
            sin = jnp.concatenate([sin_ref[rows, :]] * reps, axis=1)
        for blk in blocks:
            cols = slice(blk * PROJ_TN, (blk + 1) * PROJ_TN)
            acc = jnp.dot(hx, w_ref[:, cols], preferred_element_type=F32)
            if blk < BLK_VA:
                if rope:
                    acc = acc * cos + _swap16(acc) * sin
                if blk == BLK_QA:
                    acc = acc * DK_A ** -0.5
            elif BLK_QB <= blk < BLK_VB:
                sq = (acc * acc).astype(BF16)
                gw = gsum_ref.shape[0]
                ss = jnp.concatenate([jnp.dot(sq[:, c0:c0 + gw], gsum_ref[...], preferred_element_type=F32)
                                      for c0 in range(0, PROJ_TN, gw)], axis=1)
                gain = qk_ref[0:1, :] if blk < BLK_KB else qk_ref[1:2, :]
                acc = acc * lax.rsqrt(ss * (1.0 / DH_B) + EPS) * gain
            if blk == BLK_KA:
                kt_ref[:, rows] = acc.T.astype(kt_ref.dtype)
            o_ref[rows, cols] = acc.astype(o_ref.dtype)


def _mixer_in(x, mod3, g, w, w_g, b_g, qk_gain, cos, sin, gsum, *, rope, blocks, ctx_row=None, tm=512):
    bn, L, d = x.shape
    tm = min(tm, L)
    nblk = L // tm
    n = w.shape[1]
    mod_map = (lambda b, i: (b, 0, 0)) if ctx_row is None else (lambda b, i: (ctx_row, 0, 0))
    const = lambda b, i: (0, 0)
    tok = lambda b, i: (b * nblk + i, 0)
    resident = pl.Buffered(1)
    return pl.pallas_call(
        functools.partial(_mixer_in_kernel, rope=rope, blocks=blocks),
        grid=(bn, nblk),
        in_specs=[pl.BlockSpec((None, tm, d), lambda b, i: (b, i, 0)),
                  pl.BlockSpec((None, N_MOD, d), mod_map),
                  pl.BlockSpec((1, d), const),
                  pl.BlockSpec((d, n), const, pipeline_mode=resident),
                  pl.BlockSpec((d, LANES), const, pipeline_mode=resident),
                  pl.BlockSpec((1, LANES), const),
                  pl.BlockSpec((2, PROJ_TN), const),
                  pl.BlockSpec((tm, LANES), lambda b, i: (i, 0)),
                  pl.BlockSpec((tm, LANES), lambda b, i: (i, 0)),
                  pl.BlockSpec(gsum.shape, const, pipeline_mode=resident)],
        out_specs=[pl.BlockSpec((tm, n), tok),
                   pl.BlockSpec((PROJ_TN, tm), lambda b, i: (0, b * nblk + i)),
                   pl.BlockSpec((tm, LANES), tok),
                   pl.BlockSpec((LANES, tm), lambda b, i: (0, b * nblk + i))],
        out_shape=[jax.ShapeDtypeStruct((bn * L, n), BF16),
                   jax.ShapeDtypeStruct((PROJ_TN, bn * L), BF16),
                   jax.ShapeDtypeStruct((bn * L, LANES), F32),
                   jax.ShapeDtypeStruct((LANES, bn * L), F32)],
        compiler_params=_cparams(("parallel", "parallel")),
        name="mixer_in",
    )(x, mod3, g.reshape(1, d), w, w_g, b_g, qk_gain, cos, sin, gsum)


def _mlstm_state_kernel(kf_ref, vf_ref, gf_ref, kb_ref, vb_ref, gb_ref, c0_ref,
                        cf_ref, cb_ref, cfin_ref, st, *, nsteps, cps):
    s = pl.program_id(1)

    @pl.when(s == 0)
    def _():
        st[...] = c0_ref[...]

    lc = CHUNK
    top_k = lax.broadcasted_iota(jnp.int32, (2 * DK_A, lc), 0) < DK_A
    top = lax.broadcasted_iota(jnp.int32, (2 * DK_A, C_EXT), 0) < DK_A
    ones_blk = jnp.ones((lc, DV_A), BF16)
    for c in range(cps):
        for d, (k_ref, v_ref, g_ref, out_ref) in enumerate(((kf_ref, vf_ref, gf_ref, cf_ref),
                                                            (kb_ref, vb_ref, gb_ref, cb_ref))):
            cc = c if d == 0 else cps - 1 - c
            tok = slice(cc * lc, (cc + 1) * lc)
            g = g_ref[:, tok]
            for p in range(N_PAIR):
                h0, h1 = 2 * p, 2 * p + 1
                wl = G_WL + 16 * d
                w = jnp.where(top_k, jnp.exp(g[wl + h0:wl + h0 + 1, :]), jnp.exp(g[wl + h1:wl + h1 + 1, :]))
                kw = k_ref[p * 2 * DK_A:(p + 1) * 2 * DK_A, tok].astype(F32) * w
                kw0 = jnp.where(top_k, kw, 0.0).astype(BF16)
                kw1 = jnp.where(top_k, 0.0, kw).astype(BF16)
                vext0 = jnp.concatenate([v_ref[tok, h0 * DV_A:(h0 + 1) * DV_A], ones_blk], axis=1)
                vext1 = jnp.concatenate([v_ref[tok, h1 * DV_A:(h1 + 1) * DV_A], ones_blk], axis=1)
                upd = (jnp.dot(kw0, vext0, preferred_element_type=F32)
                       + jnp.dot(kw1, vext1, preferred_element_type=F32))
                c_old = st[d, p]
                out_ref[cc, p, 0] = c_old[:DK_A].astype(out_ref.dtype)
                out_ref[cc, p, 1] = c_old[DK_A:].astype(out_ref.dtype)
                tl = G_TOT + 16 * d
                dec = jnp.where(top, jnp.exp(g[tl + h0:tl + h0 + 1, 0:1]), jnp.exp(g[tl + h1:tl + h1 + 1, 0:1]))
                st[d, p] = dec * c_old + upd

    @pl.when(s == nsteps - 1)
    def _():
        cfin_ref[...] = st[...]


def _mlstm_states(proj, kt, gr, c0, *, bn, seq_len):
    nc = seq_len // CHUNK
    cps = min(MLSTM_CPS, nc)
    nsteps = nc // cps
    lb = cps * CHUNK
    vcol = BLK_VA * PROJ_TN // (NH_A * DV_A)
    fwd = lambda b, s: b * nsteps + s
    bwd = lambda b, s: b * nsteps + nsteps - 1 - s
    st_shape = (2, N_PAIR, 2 * DK_A, C_EXT)
    out_blk = (None, cps, N_PAIR, 2, DK_A, C_EXT)
    return pl.pallas_call(
        functools.partial(_mlstm_state_kernel, nsteps=nsteps, cps=cps),
        grid=(bn, nsteps),
        in_specs=[pl.BlockSpec((NH_A * DK_A, lb), lambda b, s: (0, fwd(b, s))),
                  pl.BlockSpec((lb, NH_A * DV_A), lambda b, s: (fwd(b, s), vcol)),
                  pl.BlockSpec((LANES, lb), lambda b, s: (0, fwd(b, s))),
                  pl.BlockSpec((NH_A * DK_A, lb), lambda b, s: (0, bwd(b, s))),
                  pl.BlockSpec((lb, NH_A * DV_A), lambda b, s: (bwd(b, s), vcol)),
                  pl.BlockSpec((LANES, lb), lambda b, s: (0, bwd(b, s))),
                  pl.BlockSpec((None,) + st_shape, lambda b, s: (b, 0, 0, 0, 0))],
        out_specs=[pl.BlockSpec(out_blk, lambda b, s: (b, s, 0, 0, 0, 0)),
                   pl.BlockSpec(out_blk, lambda b, s: (b, nsteps - 1 - s, 0, 0, 0, 0)),
                   pl.BlockSpec((None,) + st_shape, lambda b, s: (b, 0, 0, 0, 0))],
        out_shape=[jax.ShapeDtypeStruct((bn, nc, N_PAIR, 2, DK_A, C_EXT), BF16),
                   jax.ShapeDtypeStruct((bn, nc, N_PAIR, 2, DK_A, C_EXT), BF16),
                   jax.ShapeDtypeStruct((bn,) + st_shape, F32)],
        scratch_shapes=[pltpu.VMEM(st_shape, F32)],
        compiler_params=_cparams(("parallel", "arbitrary")),
        name="mlstm_states",
    )(kt, proj, gr, kt, proj, gr, c0)


def _mlstm_out_kernel(q_ref, kt_ref, v_ref, gc_ref, gr_ref, cf_ref, cb_ref, h_ref):
    lc = CHUNK
    t_idx = lax.broadcasted_iota(jnp.int32, (lc, lc), 0)
    s_idx = lax.broadcasted_iota(jnp.int32, (lc, lc), 1)
    visible = (s_idx <= t_idx, s_idx >= t_idx)
    lo = lax.broadcasted_iota(jnp.int32, (lc, 2 * DK_A), 1) < DK_A
    ones_blk = jnp.ones((lc, DV_A), BF16)
    zero_c = jnp.zeros((DK_A, C_EXT), BF16)
    for c in range(q_ref.shape[0] // lc):
        tok = slice(c * lc, (c + 1) * lc)
        gc = gc_ref[tok, :]
        gr = gr_ref[:, tok]
        for p in range(N_PAIR):
            qp = q_ref[tok, p * 2 * DK_A:(p + 1) * 2 * DK_A]
            ktp = kt_ref[p * 2 * DK_A:(p + 1) * 2 * DK_A, tok]
            zq = jnp.zeros_like(qp)
            q_stack = jnp.concatenate([jnp.where(lo, qp, zq), jnp.where(lo, zq, qp)], axis=0)
            s_both = jnp.dot(q_stack, ktp, preferred_element_type=F32)
            b_rep = [[jnp.broadcast_to(gc[:, G_B + 16 * d + 2 * p + hh:G_B + 16 * d + 2 * p + hh + 1], (lc, lc))
                      for hh in range(2)] for d in range(2)]
            inter = []
            for d, c_ref in enumerate((cf_ref, cb_ref)):
                e = jnp.where(lo, jnp.exp(b_rep[d][0]), jnp.exp(b_rep[d][1]))
                qs = (qp.astype(F32) * e).astype(BF16)
                c_pair = jnp.concatenate([jnp.concatenate([c_ref[c, p, 0], zero_c], axis=1),
                                          jnp.concatenate([zero_c, c_ref[c, p, 1]], axis=1)], axis=0)
                inter.append(jnp.dot(qs, c_pair, preferred_element_type=F32))
            for hh in range(2):
                h = 2 * p + hh
                s_h = s_both[hh * lc:(hh + 1) * lc]
                vext = jnp.concatenate([v_ref[tok, h * DV_A:(h + 1) * DV_A], ones_blk], axis=1)
                acc = None
                for d in range(2):
                    a_row = gr[G_A + 16 * d + h:G_A + 16 * d + h + 1, :]
                    log_d = jnp.where(visible[d], b_rep[d][hh] + a_row, -jnp.inf)
                    pm = (s_h * jnp.exp(log_d)).astype(BF16)
                    nd = (jnp.dot(pm, vext, preferred_element_type=F32)
                          + inter[d][:, hh * C_EXT:(hh + 1) * C_EXT])
                    hd = nd[:, :DV_A] / jnp.maximum(jnp.abs(nd[:, DV_A:]), 1.0)
                    acc = hd if acc is None else acc + hd
                h_ref[tok, h * DV_A:(h + 1) * DV_A] = acc.astype(h_ref.dtype)


def _mlstm_outputs(proj, kt, gc, gr, cf, cb, *, bn, seq_len):
    cps = min(MLSTM_CPS, seq_len // CHUNK)
    lc = cps * CHUNK
    nc = seq_len // lc
    qcol = BLK_QA * PROJ_TN // (NH_A * DK_A)
    vcol = BLK_VA * PROJ_TN // (NH_A * DV_A)
    tok = lambda b, c: b * nc + c
    st_blk = (None, cps, N_PAIR, 2, DK_A, C_EXT)
    st_map = lambda b, c: (b, c, 0, 0, 0, 0)
    return pl.pallas_call(
        _mlstm_out_kernel,
        grid=(bn, nc),
        in_specs=[pl.BlockSpec((lc, NH_A * DK_A), lambda b, c: (tok(b, c), qcol)),
                  pl.BlockSpec((NH_A * DK_A, lc), lambda b, c: (0, tok(b, c))),
                  pl.BlockSpec((lc, NH_A * DV_A), lambda b, c: (tok(b, c), vcol)),
                  pl.BlockSpec((lc, LANES), lambda b, c: (tok(b, c), 0)),
                  pl.BlockSpec((LANES, lc), lambda b, c: (0, tok(b, c))),
                  pl.BlockSpec(st_blk, st_map),
                  pl.BlockSpec(st_blk, st_map)],
        out_specs=pl.BlockSpec((lc, NH_A * DV_A), lambda b, c: (tok(b, c), 0)),
        out_shape=jax.ShapeDtypeStruct((bn * seq_len, NH_A * DV_A), F32),
        compiler_params=_cparams(("parallel", "parallel")),
        name="mlstm_outputs",
    )(proj, kt, proj, gc, gr, cf, cb)


NAT_QB = 8
NAT_SAFE_RANGE = 100.0


def _natten_kernel(shift_ref, q_ref, k_ref, v_ref, kc_ref, vc_ref, ba_ref, bm_ref, bb_ref, o_ref, *,
                   seq_len, bounded):
    i2 = pl.program_id(2)
    nq = QROWS * GRID_W
    nk = KROWS * GRID_W
    pw = 2 * DH_B
    shift = shift_ref[0:1, 0:1]
    spare = (DH_B, 0)
    nt = (((1,), (1,)), ((), ()))

    def with_ones_lane(v, h):
        lane = lax.broadcasted_iota(jnp.int32, v.shape, 1)
        own = (lane < DH_B) if h == 0 else (lane >= DH_B)
        return jnp.where(own, v, (lane == spare[h]).astype(v.dtype))

    kc = kc_ref[...]
    vc_pair = [with_ones_lane(vc_ref[...], h) for h in range(2)]
    lo_o = lax.broadcasted_iota(jnp.int32, (nq, pw), 1) < DH_B
    zq = jnp.zeros((nq, pw), BF16)
    qb = q_ref.shape[0] // nq
    bias_refs = (ba_ref,) + (bm_ref,) * (qb - 2) + (bb_ref,)
    for j, bias_ref in enumerate(bias_refs):
        blk = qb * i2 + j
        start = jnp.clip(blk * nq - (WIN_R // 2) * GRID_W, 0, seq_len - nk)
        start = pl.multiple_of(start, GRID_W)
        q = q_ref[j * nq:(j + 1) * nq, :]
        kb = k_ref[pl.ds(start, nk), :]
        vb = v_ref[pl.ds(start, nk), :]
        outs = []
        for h in range(2):
            qh = jnp.where(lo_o if h == 0 else ~lo_o, q, zq)
            s_win = lax.dot_general(qh, kb, nt, preferred_element_type=F32) + bias_ref[h]
            s_ctx = lax.dot_general(qh, kc, nt, preferred_element_type=F32)
            if bounded:
                m = shift
            else:
                m = jnp.maximum(jnp.max(s_win, axis=1, keepdims=True), jnp.max(s_ctx, axis=1, keepdims=True))
                s_win = s_win - m
            o = (jnp.dot(jnp.exp2(s_win).astype(BF16), with_ones_lane(vb, h), preferred_element_type=F32)
                 + jnp.dot(jnp.exp2(s_ctx - m).astype(BF16), vc_pair[h], preferred_element_type=F32))
            outs.append(o / o[:, spare[h]:spare[h] + 1])
        o_ref[j * nq:(j + 1) * nq, :] = jnp.where(lo_o, outs[0], outs[1]).astype(o_ref.dtype)


def _natten_patterns(rows):
    nblk = rows // QROWS
    pats = []
    for blk in (0, 1 if nblk > 2 else 0, nblk - 1):
        r0 = blk * QROWS
        k0 = int(np.clip(r0 - WIN_R // 2, 0, rows - KROWS))
        qr = r0 + np.arange(QROWS)
        kr = k0 + np.arange(KROWS)
        wr0 = np.clip(qr - WIN_R // 2, 0, rows - WIN_R)
        valid = (kr[None, :] >= wr0[:, None]) & (kr[None, :] < wr0[:, None] + WIN_R)
        dr = np.clip(kr[None, :] - qr[:, None] + WIN_R - 1, 0, 2 * WIN_R - 2)
        pats.append((valid, dr))
    return pats


def _bias_kernel(t_ref, o_ref, *, patterns):
    pat = pl.program_id(0)
    for ps, (valid, dr) in enumerate(patterns):
        @pl.when(pat == ps)
        def _():
            for xq in range(QROWS):
                for yk in range(KROWS):
                    if valid[xq, yk]:
                        blk = t_ref[int(dr[xq, yk])]
                    else:
                        blk = jnp.full((GRID_W, GRID_W), -jnp.inf, F32)
                    o_ref[xq * GRID_W:(xq + 1) * GRID_W, yk * GRID_W:(yk + 1) * GRID_W] = blk


def _natten_shift(rpb, qn_g, kn_g):
    qmax = jnp.max(jnp.abs(qn_g)) * (DH_B ** -0.5 * LOG2E) * DH_B ** 0.5
    kmax = jnp.max(jnp.abs(kn_g)) * DH_B ** 0.5
    bound = 1.02 * qmax * kmax
    bias_hi = jnp.maximum(jnp.max(rpb) * LOG2E, 0.0)
    bias_lo = jnp.minimum(jnp.min(rpb) * LOG2E, 0.0)
    shift = bound + bias_hi
    spread = shift + bound - bias_lo
    bounded = spread <= NAT_SAFE_RANGE
    return jnp.where(bounded, shift, 0.0).astype(F32), bounded


def _natten_bias(rpb, rows, shift):
    nh = rpb.shape[0]
    qc = np.arange(GRID_W)
    kc = np.arange(GRID_W)
    wc0 = np.clip(qc - WIN_C // 2, 0, GRID_W - WIN_C)
    in_c = (kc[None, :] >= wc0[:, None]) & (kc[None, :] < wc0[:, None] + WIN_C)
    dc = np.clip(kc[None, :] - qc[:, None] + WIN_C - 1, 0, 2 * WIN_C - 2)
    sel_c = jnp.asarray(dc[:, :, None] == np.arange(2 * WIN_C - 1), F32)
    tcol = jnp.einsum('hab,uvb->hauv', rpb.astype(F32) * LOG2E, sel_c, precision=lax.Precision.HIGHEST)
    tcol = jnp.where(in_c[None, None], tcol - shift, -jnp.inf)
    nq, nk = QROWS * GRID_W, KROWS * GRID_W
    n_dr = 2 * WIN_R - 1
    return pl.pallas_call(
        functools.partial(_bias_kernel, patterns=_natten_patterns(rows)),
        grid=(3, nh),
        in_specs=[pl.BlockSpec((None, n_dr, GRID_W, GRID_W), lambda p, h: (h, 0, 0, 0))],
        out_specs=pl.BlockSpec((None, None, nq, nk), lambda p, h: (p, h, 0, 0)),
        out_shape=jax.ShapeDtypeStruct((3, nh, nq, nk), F32),
        compiler_params=_cparams(("arbitrary", "arbitrary")),
        name="natten_bias",
    )(tcol)


def _natten(px, pc, bias, shift, *, bn, seq_len, lctx, bounded):
    nq = QROWS * GRID_W
    nk = KROWS * GRID_W
    qb = min(NAT_QB, seq_len // nq)
    nstep = seq_len // (qb * nq)
    pw = 2 * DH_B
    n_pair = NH_B // 2
    qcol, kcol, vcol = (blk * PROJ_TN // pw for blk in (BLK_QB, BLK_KB, BLK_VB))
    last = nstep - 1
    first_pat = lambda i: jnp.where(i == 0, 0, 1)
    second_pat = lambda i: jnp.where(i == last, 2, 1)
    return pl.pallas_call(
        functools.partial(_natten_kernel, seq_len=seq_len, bounded=bounded),
        grid=(bn, n_pair, nstep),
        in_specs=[pl.BlockSpec((1, pw), lambda b, p, i: (0, 0)),
                  pl.BlockSpec((qb * nq, pw), lambda b, p, i: (b * nstep + i, qcol + p)),
                  pl.BlockSpec((seq_len, pw), lambda b, p, i: (b, kcol + p)),
                  pl.BlockSpec((seq_len, pw), lambda b, p, i: (b, vcol + p)),
                  pl.BlockSpec((lctx, pw), lambda b, p, i: (b, kcol + p)),
                  pl.BlockSpec((lctx, pw), lambda b, p, i: (b, vcol + p)),
                  pl.BlockSpec((None, 2, nq, nk), lambda b, p, i: (first_pat(i), p, 0, 0)),
                  pl.BlockSpec((None, 2, nq, nk), lambda b, p, i: (1, p, 0, 0)),
                  pl.BlockSpec((None, 2, nq, nk), lambda b, p, i: (second_pat(i), p, 0, 0))],
        out_specs=pl.BlockSpec((qb * nq, pw), lambda b, p, i: (b * nstep + i, p)),
        out_shape=jax.ShapeDtypeStruct((bn * seq_len, NH_B * DH_B), BF16),
        compiler_params=_cparams(("parallel", "parallel", "arbitrary")),
        name="natten_bounded" if bounded else "natten",
    )(jnp.full((1, pw), shift, F32), px, px, px, pc, pc, bias, bias, bias)


def _merge_kernel(h_ref, oa_ref, na_ref, mg_ref, x_ref, mod_ref, gn_ref, wa_ref, wb_ref, wo_ref, o_ref):
    h = h_ref[...]
    parts = []
    for hd in range(NH_A):
        hh = h[:, hd * DV_A:(hd + 1) * DV_A]
        ms = jnp.mean(hh * hh, axis=-1, keepdims=True)
        parts.append(hh * lax.rsqrt(ms + EPS))
    ha = jnp.concatenate(parts, axis=1) * gn_ref[...]
    ha = (ha * jax.nn.sigmoid(oa_ref[...].astype(F32))).astype(BF16)
    d = wa_ref.shape[1]
    gates = jax.nn.sigmoid(mg_ref[...].astype(F32))
    t = (gates[:, :d] * jnp.dot(ha, wa_ref[...], preferred_element_type=F32)
         + gates[:, d:] * jnp.dot(na_ref[...], wb_ref[...], preferred_element_type=F32))
    mix = jnp.dot(t.astype(BF16), wo_ref[...], preferred_element_type=F32)
    g1 = mod_ref[2:3, :]
    o_ref[...] = x_ref[...] + g1 * mix


def _merge(h, proj, na, x, mod3, gn, wa, wb, wo, tm=512):
    bn, S, d = x.shape
    tm = min(tm, S)
    nblk = S // tm
    d_a = h.shape[-1]
    d_b = na.shape[-1]
    oa_blk = BLK_OA * PROJ_TN // d_a
    mg_blk = BLK_MG * PROJ_TN // (2 * d)
    const = lambda b, i: (0, 0)
    return pl.pallas_call(
        _merge_kernel,
        grid=(bn, nblk),
        in_specs=[pl.BlockSpec((tm, d_a), lambda b, i: (b * nblk + i, 0)),
                  pl.BlockSpec((tm, d_a), lambda b, i: (b * nblk + i, oa_blk)),
                  pl.BlockSpec((tm, d_b), lambda b, i: (b * nblk + i, 0)),
                  pl.BlockSpec((tm, 2 * d), lambda b, i: (b * nblk + i, mg_blk)),
                  pl.BlockSpec((None, tm, d), lambda b, i: (b, i, 0)),
                  pl.BlockSpec((None, N_MOD, d), lambda b, i: (b, 0, 0)),
                  pl.BlockSpec((1, d_a), const),
                  pl.BlockSpec((d_a, d), const, pipeline_mode=pl.Buffered(1)),
                  pl.BlockSpec((d_b, d), const, pipeline_mode=pl.Buffered(1)),
                  pl.BlockSpec((d, d), const, pipeline_mode=pl.Buffered(1))],
        out_specs=pl.BlockSpec((None, tm, d), lambda b, i: (b, i, 0)),
        out_shape=jax.ShapeDtypeStruct((bn, S, d), F32),
        compiler_params=_cparams(("parallel", "parallel")),
        name="merge",
    )(h, proj, na, proj, x, mod3, gn, wa, wb, wo)


def _ffn_kernel(x_ref, mod_ref, g_ref, wg_ref, wu_ref, wd_ref, o_ref):
    x = x_ref[...]
    hx = _mod_norm(x, g_ref[...], mod_ref[4:5, :], mod_ref[3:4, :]).astype(BF16)
    a = jnp.dot(hx, wg_ref[...], preferred_element_type=F32)
    u = jnp.dot(hx, wu_ref[...], preferred_element_type=F32)
    act = (a * jax.nn.sigmoid(a) * u).astype(BF16)
    f = jnp.dot(act, wd_ref[...], preferred_element_type=F32)
    o_ref[...] = x + mod_ref[5:6, :] * f


def _ffn(x, mod3, g, wg, wu, wd, tm=512):
    bn, S, d = x.shape
    tm = min(tm, S)
    dff = wg.shape[1]
    const = lambda b, i: (0, 0)
    return pl.pallas_call(
        _ffn_kernel,
        grid=(bn, S // tm),
        in_specs=[pl.BlockSpec((None, tm, d), lambda b, i: (b, i, 0)),
                  pl.BlockSpec((None, N_MOD, d), lambda b, i: (b, 0, 0)),
                  pl.BlockSpec((1, d), const),
                  pl.BlockSpec((d, dff), const, pipeline_mode=pl.Buffered(1)),
                  pl.BlockSpec((d, dff), const, pipeline_mode=pl.Buffered(1)),
                  pl.BlockSpec((dff, d), const, pipeline_mode=pl.Buffered(1))],
        out_specs=pl.BlockSpec((None, tm, d), lambda b, i: (b, i, 0)),
        out_shape=jax.ShapeDtypeStruct((bn, S, d), F32),
        compiler_params=_cparams(("parallel", "parallel")),
        name="ffn",
    )(x, mod3, g.reshape(1, d), wg, wu, wd)


def _rope_tables(S):
    t = np.arange(S)
    row, col = t // GRID_W, t % GRID_W
    half = DK_A // 4
    inv = ROPE_THETA ** (-np.arange(half, dtype=np.float64) / half)
    ang_r = row[:, None] * inv[None, :]
    ang_c = col[:, None] * inv[None, :]
    cos = np.concatenate([np.cos(ang_r)] * 2 + [np.cos(ang_c)] * 2, axis=1)
    sin = np.concatenate([-np.sin(ang_r), np.sin(ang_r), -np.sin(ang_c), np.sin(ang_c)], axis=1)
    reps = LANES // DK_A
    return (jnp.asarray(np.tile(cos, (1, reps)), F32), jnp.asarray(np.tile(sin, (1, reps)), F32))


def kernel(x, c, ctx, c_ctx, w_mod, b_mod, norm1_g, w_in, b_gates, mlstm_norm_g, qn_g, kn_g, rpb,
           w_branch_a, w_branch_b, w_out, norm2_g, w_ffn_gate, w_ffn_up, w_ffn_down):
    bn, S, d = x.shape
    lctx = ctx.shape[1]
    rows = S // GRID_W
    depth = w_mod.shape[0]
    d_a, d_b = NH_A * DV_A, NH_B * DH_B
    dqk = NH_A * DK_A
    sizes = (dqk, dqk, d_a, d_a, 4 * NH_A, d_b, d_b, d_b, 2 * d)
    offs = np.cumsum((0,) + sizes)
    cos_t, sin_t = _rope_tables(S)
    gsum = jnp.asarray(np.kron(np.eye(MXU_TILE // DH_B), np.ones((DH_B, DH_B))), BF16)

    for l in range(depth):
        assert l == depth - 1, "context-stream update for non-final layers is not implemented"
        n_rows = -(-(bn + 1) // 8) * 8
        cc = jnp.concatenate([c, c_ctx[None, :], jnp.zeros((n_rows - bn - 1, d), F32)], axis=0)
        mod3 = _modulation(cc, w_mod[l], b_mod[l]).reshape(n_rows, N_MOD, d)

        w = w_in[l]
        seg = lambda i: w[:, int(offs[i]):int(offs[i + 1])]
        w_main = jnp.concatenate([seg(0), seg(1), seg(2), seg(3), seg(5), seg(6), seg(7), seg(8)],
                                 axis=1).astype(BF16)
        w_g = jnp.pad(seg(4), ((0, 0), (0, LANES - 4 * NH_A))).astype(BF16)
        b_g = jnp.pad(b_gates[l].reshape(1, 4 * NH_A), ((0, 0), (0, LANES - 4 * NH_A)))
        qk_gain = jnp.stack([jnp.tile(qn_g[l], PROJ_TN // DH_B) * (DH_B ** -0.5 * LOG2E),
                             jnp.tile(kn_g[l], PROJ_TN // DH_B)])

        px, ktx, gcx, grx = _mixer_in(x, mod3, norm1_g[l], w_main, w_g, b_g, qk_gain, cos_t, sin_t, gsum,
                                      rope=True, blocks=tuple(range(N_PROJ_BLK)))
        pc, ktc, _, grc = _mixer_in(ctx, mod3, norm1_g[l], w_main, w_g, b_g, qk_gain, cos_t, sin_t, gsum,
                                    rope=False, blocks=CTX_BLOCKS, ctx_row=bn)

        zero_state = jnp.zeros((bn, 2, N_PAIR, 2 * DK_A, C_EXT), F32)
        _, _, st_ctx = _mlstm_states(pc, ktc, grc, zero_state, bn=bn, seq_len=lctx)
        cf, cb, _ = _mlstm_states(px, ktx, grx, st_ctx, bn=bn, seq_len=S)
        h_a = _mlstm_outputs(px, ktx, gcx, grx, cf, cb, bn=bn, seq_len=S)

        shift, bounded = _natten_shift(rpb[l], qn_g[l], kn_g[l])
        bias = _natten_bias(rpb[l], rows, shift)
        na = lax.cond(bounded,
                      functools.partial(_natten, bn=bn, seq_len=S, lctx=lctx, bounded=True),
                      functools.partial(_natten, bn=bn, seq_len=S, lctx=lctx, bounded=False),
                      px, pc, bias, shift)

        x_mid = _merge(h_a, px, na, x, mod3, mlstm_norm_g[l].reshape(1, d_a),
                       w_branch_a[l].astype(BF16), w_branch_b[l].astype(BF16), w_out[l].astype(BF16))
        x = _ffn(x_mid, mod3, norm2_g[l], w_ffn_gate[l].astype(BF16), w_ffn_up[l].astype(BF16),
                 w_ffn_down[l].astype(BF16))
    return x
```

```python
import functools

import numpy as np
import jax
import jax.numpy as jnp
from jax import lax
from jax.experimental import pallas as pl
from jax.experimental.pallas import tpu as pltpu

F32 = jnp.float32
BF16 = jnp.bfloat16

GRID_W = 64
NH_A, DK_A, DV_A = 8, 64, 128
NH_B, DH_B = 16, 64
CHUNK = 128
GATE_CAP = 15.0
WIN_R, WIN_C = 8, 16
ROPE_THETA = 10000.0
EPS = 1e-6
N_MOD = 6
LANES = 128
MXU_TILE = 256
VMEM_LIMIT = 56 * 1024 * 1024

QROWS = 4
KROWS = QROWS + WIN_R - 1

N_PAIR = NH_A // 2
MLSTM_CPS = 8
C_EXT = 2 * DV_A

G_A, G_WL, G_B, G_TOT = 0, 32, 72, 104


def _cparams(sem):
    return pltpu.CompilerParams(dimension_semantics=sem, vmem_limit_bytes=VMEM_LIMIT)


def _mod_kernel(c_ref, w_ref, b_ref, o_ref):
    c = c_ref[...]
    h = c * jax.nn.sigmoid(c)
    o_ref[...] = jnp.dot(h, w_ref[...], preferred_element_type=F32,
                         precision=lax.Precision.HIGHEST) + b_ref[...]


def _modulation(cc, w_mod, b_mod):
    rows, d = cc.shape
    n = w_mod.shape[1]
    tn = 1024
    return pl.pallas_call(
        _mod_kernel,
        grid=(n // tn,),
        in_specs=[pl.BlockSpec((rows, d), lambda j: (0, 0)),
                  pl.BlockSpec((d, tn), lambda j: (0, j)),
                  pl.BlockSpec((1, tn), lambda j: (0, j))],
        out_specs=pl.BlockSpec((rows, tn), lambda j: (0, j)),
        out_shape=jax.ShapeDtypeStruct((rows, n), F32),
        compiler_params=_cparams(("arbitrary",)),
        name="modulation",
    )(cc, w_mod, b_mod.reshape(1, n))


def _mod_norm(x, gain, scale, shift):
    ms = jnp.mean(x * x, axis=-1, keepdims=True)
    y = x * lax.rsqrt(ms + EPS) * gain
    return y * (1.0 + scale) + shift


def _gate_pack(g, gc_ref, gr_ref, r0):
    tm = g.shape[0]
    g = GATE_CAP * jnp.tanh(g * (1.0 / GATE_CAP))
    lane_t = lax.broadcasted_iota(jnp.int32, g.shape, 1)
    is_forget = (lane_t & NH_A) != 0
    x = jnp.where(is_forget, jax.nn.log_sigmoid(g), g)
    x = jnp.where(lane_t < 4 * NH_A, x, 0.0)

    t_idx = lax.broadcasted_iota(jnp.int32, (CHUNK, CHUNK), 0)
    lane = lax.broadcasted_iota(jnp.int32, (CHUNK, CHUNK), 1)
    fwd_half = lane < 2 * NH_A
    low = lane < 4 * NH_A
    for ci in range(tm // CHUNK):
        xc = x[ci * CHUNK:(ci + 1) * CHUNK]
        cf = xc
        step = 1
        while step < CHUNK:
            cf = cf + jnp.where(t_idx >= step, pltpu.roll(cf, step, 0), 0.0)
            step *= 2
        tot = jnp.broadcast_to(cf[CHUNK - 1:CHUNK, :], xc.shape)
        cb = tot - cf + xc
        r_b = jnp.where(fwd_half, pltpu.roll(cf, LANES - NH_A, 1), pltpu.roll(cb, LANES - NH_A, 1))
        a = xc - r_b
        wl = xc + (pltpu.roll(tot, LANES - NH_A, 1) - r_b)
        bsel = jnp.where(fwd_half, cf, cb)
        z = lambda v: jnp.where(low, v, 0.0)
        packed = z(a) + pltpu.roll(z(wl), 32, 1) + pltpu.roll(z(bsel), 64, 1) + pltpu.roll(z(tot), 96, 1)
        gc_ref[r0 + ci * CHUNK:r0 + (ci + 1) * CHUNK, :] = packed
        gr_ref[:, r0 + ci * CHUNK:r0 + (ci + 1) * CHUNK] = packed.T


PROJ_TN = 512
BLK_QA, BLK_KA, BLK_VA, BLK_OA, BLK_QB, BLK_KB, BLK_VB, BLK_MG = 0, 1, 2, 4, 6, 8, 10, 12
N_PROJ_BLK = 16
MIX_ROW_PARTS = 1
CTX_BLOCKS = (BLK_KA, BLK_VA, BLK_VA + 1, BLK_KB, BLK_KB + 1, BLK_VB, BLK_VB + 1)
LOG2E = 1.4426950408889634


def _swap16(x):
    n = x.shape[-1]
    lane = lax.broadcasted_iota(jnp.int32, x.shape, x.ndim - 1)
    first = (lane & 31) < 16
    return jnp.where(first, pltpu.roll(x, n - 16, x.ndim - 1), pltpu.roll(x, 16, x.ndim - 1))


def _mixer_in_kernel(x_ref, mod_ref, g_ref, w_ref, wg_ref, bg_ref, qk_ref, cos_ref, sin_ref, gsum_ref,
                     o_ref, kt_ref, gc_ref, gr_ref, *, rope, blocks):
    tm = x_ref.shape[0]
    part = tm // MIX_ROW_PARTS
    for r0 in range(0, tm, part):
        rows = slice(r0, r0 + part)
        hx = _mod_norm(x_ref[rows, :], g_ref[...], mod_ref[1:2, :], mod_ref[0:1, :]).astype(BF16)
        _gate_pack(jnp.dot(hx, wg_ref[...], preferred_element_type=F32) + bg_ref[...], gc_ref, gr_ref, r0)
        if rope:
            reps = PROJ_TN // LANES
            cos = jnp.concatenate([cos_ref[rows, :]] * reps, axis=1)
            sin = jnp.concatenate([sin_ref[rows, :]] * reps, axis=1)
        for blk in range(N_PROJ_BLK):
            cols = slice(blk * PROJ_TN, (blk + 1) * PROJ_TN)
            if blk not in blocks:
                o_ref[rows, cols] = jnp.zeros((part, PROJ_TN), o_ref.dtype)
                continue
            acc = jnp.dot(hx, w_ref[:, cols], preferred_element_type=F32)
            if blk < BLK_VA:
                if rope:
                    acc = acc * cos + _swap16(acc) * sin
                if blk == BLK_QA:
                    acc = acc * DK_A ** -0.5
            elif BLK_QB <= blk < BLK_VB:
                sq = (acc * acc).astype(BF16)
                gw = gsum_ref.shape[0]
                ss = jnp.concatenate([jnp.dot(sq[:, c0:c0 + gw], gsum_ref[...], preferred_element_type=F32)
                                      for c0 in range(0, PROJ_TN, gw)], axis=1)
                gain = qk_ref[0:1, :] if blk < BLK_KB else qk_ref[1:2, :]
                acc = acc * lax.rsqrt(ss * (1.0 / DH_B) + EPS) * gain
            if blk == BLK_KA:
                kt_ref[:, rows] = acc.T.astype(kt_ref.dtype)
            o_ref[rows, cols] = acc.astype(o_ref.dtype)


def _mixer_in(x, mod3, g, w, w_g, b_g, qk_gain, cos, sin, gsum, *, rope, blocks, ctx_row=None, tm=512):
    bn, L, d = x.shape
    tm = min(tm, L)
    nblk = L // tm
    n = w.shape[1]
    mod_map = (lambda b, i: (b, 0, 0)) if ctx_row is None else (lambda b, i: (ctx_row, 0, 0))
    const = lambda b, i: (0, 0)
    tok = lambda b, i: (b * nblk + i, 0)
    resident = pl.Buffered(1)
    return pl.pallas_call(
        functools.partial(_mixer_in_kernel, rope=rope, blocks=blocks),
        grid=(bn, nblk),
        in_specs=[pl.BlockSpec((None, tm, d), lambda b, i: (b, i, 0)),
                  pl.BlockSpec((None, N_MOD, d), mod_map),
                  pl.BlockSpec((1, d), const),
                  pl.BlockSpec((d, n), const, pipeline_mode=resident),
                  pl.BlockSpec((d, LANES), const, pipeline_mode=resident),
                  pl.BlockSpec((1, LANES), const),
                  pl.BlockSpec((2, PROJ_TN), const),
                  pl.BlockSpec((tm, LANES), lambda b, i: (i, 0)),
                  pl.BlockSpec((tm, LANES), lambda b, i: (i, 0)),
                  pl.BlockSpec(gsum.shape, const, pipeline_mode=resident)],
        out_specs=[pl.BlockSpec((tm, n), tok),
                   pl.BlockSpec((PROJ_TN, tm), lambda b, i: (0, b * nblk + i)),
                   pl.BlockSpec((tm, LANES), tok),
                   pl.BlockSpec((LANES, tm), lambda b, i: (0, b * nblk + i))],
        out_shape=[jax.ShapeDtypeStruct((bn * L, n), BF16),
                   jax.ShapeDtypeStruct((PROJ_TN, bn * L), BF16),
                   jax.ShapeDtypeStruct((bn * L, LANES), F32),
                   jax.ShapeDtypeStruct((LANES, bn * L), F32)],
        compiler_params=_cparams(("parallel", "parallel")),
        name="mixer_in",
    )(x, mod3, g.reshape(1, d), w, w_g, b_g, qk_gain, cos, sin, gsum)


def _mlstm_state_kernel(kf_ref, vf_ref, gf_ref, kb_ref, vb_ref, gb_ref, c0_ref,
                        cf_ref, cb_ref, cfin_ref, st, *, nsteps, cps):
    s = pl.program_id(1)

    @pl.when(s == 0)
    def _():
        st[...] = c0_ref[...]

    lc = CHUNK
    top_k = lax.broadcasted_iota(jnp.int32, (2 * DK_A, lc), 0) < DK_A
    top = lax.broadcasted_iota(jnp.int32, (2 * DK_A, C_EXT), 0) < DK_A
    ones_blk = jnp.ones((lc, DV_A), BF16)
    for c in range(cps):
        for d, (k_ref, v_ref, g_ref, out_ref) in enumerate(((kf_ref, vf_ref, gf_ref, cf_ref),
                                                            (kb_ref, vb_ref, gb_ref, cb_ref))):
            cc = c if d == 0 else cps - 1 - c
            tok = slice(cc * lc, (cc + 1) * lc)
            g = g_ref[:, tok]
            for p in range(N_PAIR):
                h0, h1 = 2 * p, 2 * p + 1
                wl = G_WL + 16 * d
                w = jnp.where(top_k, jnp.exp(g[wl + h0:wl + h0 + 1, :]), jnp.exp(g[wl + h1:wl + h1 + 1, :]))
                kw = k_ref[p * 2 * DK_A:(p + 1) * 2 * DK_A, tok].astype(F32) * w
                kw0 = jnp.where(top_k, kw, 0.0).astype(BF16)
                kw1 = jnp.where(top_k, 0.0, kw).astype(BF16)
                vext0 = jnp.concatenate([v_ref[tok, h0 * DV_A:(h0 + 1) * DV_A], ones_blk], axis=1)
                vext1 = jnp.concatenate([v_ref[tok, h1 * DV_A:(h1 + 1) * DV_A], ones_blk], axis=1)
                upd = (jnp.dot(kw0, vext0, preferred_element_type=F32)
                       + jnp.dot(kw1, vext1, preferred_element_type=F32))
                c_old = st[d, p]
                out_ref[cc, p, 0] = c_old[:DK_A].astype(out_ref.dtype)
                out_ref[cc, p, 1] = c_old[DK_A:].astype(out_ref.dtype)
                tl = G_TOT + 16 * d
                dec = jnp.where(top, jnp.exp(g[tl + h0:tl + h0 + 1, 0:1]), jnp.exp(g[tl + h1:tl + h1 + 1, 0:1]))
                st[d, p] = dec * c_old + upd

    @pl.when(s == nsteps - 1)
    def _():
        cfin_ref[...] = st[...]


def _mlstm_states(proj, kt, gr, c0, *, bn, seq_len):
    nc = seq_len // CHUNK
    cps = min(MLSTM_CPS, nc)
    nsteps = nc // cps
    lb = cps * CHUNK
    vcol = BLK_VA * PROJ_TN // (NH_A * DV_A)
    fwd = lambda b, s: b * nsteps + s
    bwd = lambda b, s: b * nsteps + nsteps - 1 - s
    st_shape = (2, N_PAIR, 2 * DK_A, C_EXT)
    out_blk = (None, cps, N_PAIR, 2, DK_A, C_EXT)
    return pl.pallas_call(
        functools.partial(_mlstm_state_kernel, nsteps=nsteps, cps=cps),
        grid=(bn, nsteps),
        in_specs=[pl.BlockSpec((NH_A * DK_A, lb), lambda b, s: (0, fwd(b, s))),
                  pl.BlockSpec((lb, NH_A * DV_A), lambda b, s: (fwd(b, s), vcol)),
                  pl.BlockSpec((LANES, lb), lambda b, s: (0, fwd(b, s))),
                  pl.BlockSpec((NH_A * DK_A, lb), lambda b, s: (0, bwd(b, s))),
                  pl.BlockSpec((lb, NH_A * DV_A), lambda b, s: (bwd(b, s), vcol)),
                  pl.BlockSpec((LANES, lb), lambda b, s: (0, bwd(b, s))),
                  pl.BlockSpec((None,) + st_shape, lambda b, s: (b, 0, 0, 0, 0))],
        out_specs=[pl.BlockSpec(out_blk, lambda b, s: (b, s, 0, 0, 0, 0)),
                   pl.BlockSpec(out_blk, lambda b, s: (b, nsteps - 1 - s, 0, 0, 0, 0)),
                   pl.BlockSpec((None,) + st_shape, lambda b, s: (b, 0, 0, 0, 0))],
        out_shape=[jax.ShapeDtypeStruct((bn, nc, N_PAIR, 2, DK_A, C_EXT), BF16),
                   jax.ShapeDtypeStruct((bn, nc, N_PAIR, 2, DK_A, C_EXT), BF16),
                   jax.ShapeDtypeStruct((bn,) + st_shape, F32)],
        scratch_shapes=[pltpu.VMEM(st_shape, F32)],
        compiler_params=_cparams(("parallel", "arbitrary")),
        name="mlstm_states",
    )(kt, proj, gr, kt, proj, gr, c0)


def _mlstm_out_kernel(q_ref, kt_ref, v_ref, gc_ref, gr_ref, cf_ref, cb_ref, h_ref):
    lc = CHUNK
    t_idx = lax.broadcasted_iota(jnp.int32, (lc, lc), 0)
    s_idx = lax.broadcasted_iota(jnp.int32, (lc, lc), 1)
    visible = (s_idx <= t_idx, s_idx >= t_idx)
    lo = lax.broadcasted_iota(jnp.int32, (lc, 2 * DK_A), 1) < DK_A
    ones_blk = jnp.ones((lc, DV_A), BF16)
    zero_c = jnp.zeros((DK_A, C_EXT), BF16)
    for c in range(q_ref.shape[0] // lc):
        tok = slice(c * lc, (c + 1) * lc)
        gc = gc_ref[tok, :]
        gr = gr_ref[:, tok]
        for p in range(N_PAIR):
            qp = q_ref[tok, p * 2 * DK_A:(p + 1) * 2 * DK_A]
            ktp = kt_ref[p * 2 * DK_A:(p + 1) * 2 * DK_A, tok]
            zq = jnp.zeros_like(qp)
            q_stack = jnp.concatenate([jnp.where(lo, qp, zq), jnp.where(lo, zq, qp)], axis=0)
            s_both = jnp.dot(q_stack, ktp, preferred_element_type=F32)
            b_rep = [[jnp.broadcast_to(gc[:, G_B + 16 * d + 2 * p + hh:G_B + 16 * d + 2 * p + hh + 1], (lc, lc))
                      for hh in range(2)] for d in range(2)]
            inter = []
            for d, c_ref in enumerate((cf_ref, cb_ref)):
                e = jnp.where(lo, jnp.exp(b_rep[d][0]), jnp.exp(b_rep[d][1]))
                qs = (qp.astype(F32) * e).astype(BF16)
                c_pair = jnp.concatenate([jnp.concatenate([c_ref[c, p, 0], zero_c], axis=1),
                                          jnp.concatenate([zero_c, c_ref[c, p, 1]], axis=1)], axis=0)
                inter.append(jnp.dot(qs, c_pair, preferred_element_type=F32))
            for hh in range(2):
                h = 2 * p + hh
                s_h = s_both[hh * lc:(hh + 1) * lc]
                vext = jnp.concatenate([v_ref[tok, h * DV_A:(h + 1) * DV_A], ones_blk], axis=1)
                acc = None
                for d in range(2):
                    a_row = gr[G_A + 16 * d + h:G_A + 16 * d + h + 1, :]
                    log_d = jnp.where(visible[d], b_rep[d][hh] + a_row, -jnp.inf)
                    pm = (s_h * jnp.exp(log_d)).astype(BF16)
                    nd = (jnp.dot(pm, vext, preferred_element_type=F32)
                          + inter[d][:, hh * C_EXT:(hh + 1) * C_EXT])
                    hd = nd[:, :DV_A] / jnp.maximum(jnp.abs(nd[:, DV_A:]), 1.0)
                    acc = hd if acc is None else acc + hd
                h_ref[tok, h * DV_A:(h + 1) * DV_A] = acc.astype(h_ref.dtype)


def _mlstm_outputs(proj, kt, gc, gr, cf, cb, *, bn, seq_len):
    cps = min(MLSTM_CPS, seq_len // CHUNK)
    lc = cps * CHUNK
    nc = seq_len // lc
    qcol = BLK_QA * PROJ_TN // (NH_A * DK_A)
    vcol = BLK_VA * PROJ_TN // (NH_A * DV_A)
    tok = lambda b, c: b * nc + c
    st_blk = (None, cps, N_PAIR, 2, DK_A, C_EXT)
    st_map = lambda b, c: (b, c, 0, 0, 0, 0)
    return pl.pallas_call(
        _mlstm_out_kernel,
        grid=(bn, nc),
        in_specs=[pl.BlockSpec((lc, NH_A * DK_A), lambda b, c: (tok(b, c), qcol)),
                  pl.BlockSpec((NH_A * DK_A, lc), lambda b, c: (0, tok(b, c))),
                  pl.BlockSpec((lc, NH_A * DV_A), lambda b, c: (tok(b, c), vcol)),
                  pl.BlockSpec((lc, LANES), lambda b, c: (tok(b, c), 0)),
                  pl.BlockSpec((LANES, lc), lambda b, c: (0, tok(b, c))),
                  pl.BlockSpec(st_blk, st_map),
                  pl.BlockSpec(st_blk, st_map)],
        out_specs=pl.BlockSpec((lc, NH_A * DV_A), lambda b, c: (tok(b, c), 0)),
        out_shape=jax.ShapeDtypeStruct((bn * seq_len, NH_A * DV_A), F32),
        compiler_params=_cparams(("parallel", "parallel")),
        name="mlstm_outputs",
    )(proj, kt, proj, gc, gr, cf, cb)


NAT_QB = 16
NAT_SAFE_RANGE = 100.0


def _natten_kernel(shift_ref, q_ref, k_ref, v_ref, kc_ref, vc_ref, ba_ref, bm_ref, bb_ref, o_ref, *,
                   seq_len, bounded):
    i2 = pl.program_id(2)
    nq = QROWS * GRID_W
    nk = KROWS * GRID_W
    pw = 2 * DH_B
    shift = shift_ref[0:1, 0:1]
    spare = (DH_B, 0)
    nt = (((1,), (1,)), ((), ()))

    def with_ones_lane(v, h):
        lane = lax.broadcasted_iota(jnp.int32, v.shape, 1)
        own = (lane < DH_B) if h == 0 else (lane >= DH_B)
        return jnp.where(own, v, (lane == spare[h]).astype(v.dtype))

    kc = kc_ref[...]
    vc_pair = [with_ones_lane(vc_ref[...], h) for h in range(2)]
    lo_o = lax.broadcasted_iota(jnp.int32, (nq, pw), 1) < DH_B
    zq = jnp.zeros((nq, pw), BF16)
    qb = q_ref.shape[0] // nq
    bias_refs = (ba_ref,) + (bm_ref,) * (qb - 2) + (bb_ref,)
    for j, bias_ref in enumerate(bias_refs):
        blk = qb * i2 + j
        start = jnp.clip(blk * nq - (WIN_R // 2) * GRID_W, 0, seq_len - nk)
        start = pl.multiple_of(start, GRID_W)
        q = q_ref[j * nq:(j + 1) * nq, :]
        kb = k_ref[pl.ds(start, nk), :]
        vb = v_ref[pl.ds(start, nk), :]
        outs = []
        for h in range(2):
            qh = jnp.where(lo_o if h == 0 else ~lo_o, q, zq)
            s_win = lax.dot_general(qh, kb, nt, preferred_element_type=F32) + bias_ref[h]
            s_ctx = lax.dot_general(qh, kc, nt, preferred_element_type=F32)
            if bounded:
                m = shift
            else:
                m = jnp.maximum(jnp.max(s_win, axis=1, keepdims=True), jnp.max(s_ctx, axis=1, keepdims=True))
                s_win = s_win - m
            o = (jnp.dot(jnp.exp2(s_win).astype(BF16), with_ones_lane(vb, h), preferred_element_type=F32)
                 + jnp.dot(jnp.exp2(s_ctx - m).astype(BF16), vc_pair[h], preferred_element_type=F32))
            outs.append(o / o[:, spare[h]:spare[h] + 1])
        o_ref[j * nq:(j + 1) * nq, :] = jnp.where(lo_o, outs[0], outs[1]).astype(o_ref.dtype)


def _natten_patterns(rows):
    nblk = rows // QROWS
    pats = []
    for blk in (0, 1 if nblk > 2 else 0, nblk - 1):
        r0 = blk * QROWS
        k0 = int(np.clip(r0 - WIN_R // 2, 0, rows - KROWS))
        qr = r0 + np.arange(QROWS)
        kr = k0 + np.arange(KROWS)
        wr0 = np.clip(qr - WIN_R // 2, 0, rows - WIN_R)
        valid = (kr[None, :] >= wr0[:, None]) & (kr[None, :] < wr0[:, None] + WIN_R)
        dr = np.clip(kr[None, :] - qr[:, None] + WIN_R - 1, 0, 2 * WIN_R - 2)
        pats.append((valid, dr))
    return pats


def _bias_kernel(t_ref, o_ref, *, patterns):
    pat = pl.program_id(0)
    for ps, (valid, dr) in enumerate(patterns):
        @pl.when(pat == ps)
        def _():
            for xq in range(QROWS):
                for yk in range(KROWS):
                    if valid[xq, yk]:
                        blk = t_ref[int(dr[xq, yk])]
                    else:
                        blk = jnp.full((GRID_W, GRID_W), -jnp.inf, F32)
                    o_ref[xq * GRID_W:(xq + 1) * GRID_W, yk * GRID_W:(yk + 1) * GRID_W] = blk


def _natten_shift(rpb, qn_g, kn_g):
    qmax = jnp.max(jnp.abs(qn_g)) * (DH_B ** -0.5 * LOG2E) * DH_B ** 0.5
    kmax = jnp.max(jnp.abs(kn_g)) * DH_B ** 0.5
    bound = 1.02 * qmax * kmax
    bias_hi = jnp.maximum(jnp.max(rpb) * LOG2E, 0.0)
    bias_lo = jnp.minimum(jnp.min(rpb) * LOG2E, 0.0)
    shift = bound + bias_hi
    spread = shift + bound - bias_lo
    bounded = spread <= NAT_SAFE_RANGE
    return jnp.where(bounded, shift, 0.0).astype(F32), bounded


def _natten_bias(rpb, rows, shift):
    nh = rpb.shape[0]
    qc = np.arange(GRID_W)
    kc = np.arange(GRID_W)
    wc0 = np.clip(qc - WIN_C // 2, 0, GRID_W - WIN_C)
    in_c = (kc[None, :] >= wc0[:, None]) & (kc[None, :] < wc0[:, None] + WIN_C)
    dc = np.clip(kc[None, :] - qc[:, None] + WIN_C - 1, 0, 2 * WIN_C - 2)
    sel_c = jnp.asarray(dc[:, :, None] == np.arange(2 * WIN_C - 1), F32)
    tcol = jnp.einsum('hab,uvb->hauv', rpb.astype(F32) * LOG2E, sel_c, precision=lax.Precision.HIGHEST)
    tcol = jnp.where(in_c[None, None], tcol - shift, -jnp.inf)
    nq, nk = QROWS * GRID_W, KROWS * GRID_W
    n_dr = 2 * WIN_R - 1
    return pl.pallas_call(
        functools.partial(_bias_kernel, patterns=_natten_patterns(rows)),
        grid=(3, nh),
        in_specs=[pl.BlockSpec((None, n_dr, GRID_W, GRID_W), lambda p, h: (h, 0, 0, 0))],
        out_specs=pl.BlockSpec((None, None, nq, nk), lambda p, h: (p, h, 0, 0)),
        out_shape=jax.ShapeDtypeStruct((3, nh, nq, nk), F32),
        compiler_params=_cparams(("arbitrary", "arbitrary")),
        name="natten_bias",
    )(tcol)


def _natten(px, pc, bias, shift, *, bn, seq_len, lctx, bounded):
    nq = QROWS * GRID_W
    nk = KROWS * GRID_W
    qb = min(NAT_QB, seq_len // nq)
    nstep = seq_len // (qb * nq)
    pw = 2 * DH_B
    n_pair = NH_B // 2
    qcol, kcol, vcol = (blk * PROJ_TN // pw for blk in (BLK_QB, BLK_KB, BLK_VB))
    last = nstep - 1
    first_pat = lambda i: jnp.where(i == 0, 0, 1)
    second_pat = lambda i: jnp.where(i == last, 2, 1)
    return pl.pallas_call(
        functools.partial(_natten_kernel, seq_len=seq_len, bounded=bounded),
        grid=(bn, n_pair, nstep),
        in_specs=[pl.BlockSpec((1, pw), lambda b, p, i: (0, 0)),
                  pl.BlockSpec((qb * nq, pw), lambda b, p, i: (b * nstep + i, qcol + p)),
                  pl.BlockSpec((seq_len, pw), lambda b, p, i: (b, kcol + p)),
                  pl.BlockSpec((seq_len, pw), lambda b, p, i: (b, vcol + p)),
                  pl.BlockSpec((lctx, pw), lambda b, p, i: (b, kcol + p)),
                  pl.BlockSpec((lctx, pw), lambda b, p, i: (b, vcol + p)),
                  pl.BlockSpec((None, 2, nq, nk), lambda b, p, i: (first_pat(i), p, 0, 0)),
                  pl.BlockSpec((None, 2, nq, nk), lambda b, p, i: (1, p, 0, 0)),
                  pl.BlockSpec((None, 2, nq, nk), lambda b, p, i: (second_pat(i), p, 0, 0))],
        out_specs=pl.BlockSpec((qb * nq, pw), lambda b, p, i: (b * nstep + i, p)),
        out_shape=jax.ShapeDtypeStruct((bn * seq_len, NH_B * DH_B), BF16),
        compiler_params=_cparams(("parallel", "parallel", "arbitrary")),
        name="natten_bounded" if bounded else "natten",
    )(jnp.full((1, pw), shift, F32), px, px, px, pc, pc, bias, bias, bias)


def _merge_kernel(h_ref, oa_ref, na_ref, mg_ref, x_ref, mod_ref, gn_ref, wa_ref, wb_ref, wo_ref, o_ref):
    h = h_ref[...]
    parts = []
    for hd in range(NH_A):
        hh = h[:, hd * DV_A:(hd + 1) * DV_A]
        ms = jnp.mean(hh * hh, axis=-1, keepdims=True)
        parts.append(hh * lax.rsqrt(ms + EPS))
    ha = jnp.concatenate(parts, axis=1) * gn_ref[...]
    ha = (ha * jax.nn.sigmoid(oa_ref[...].astype(F32))).astype(BF16)
    d = wa_ref.shape[1]
    gates = jax.nn.sigmoid(mg_ref[...].astype(F32))
    t = (gates[:, :d] * jnp.dot(ha, wa_ref[...], preferred_element_type=F32)
         + gates[:, d:] * jnp.dot(na_ref[...], wb_ref[...], preferred_element_type=F32))
    mix = jnp.dot(t.astype(BF16), wo_ref[...], preferred_element_type=F32)
    g1 = mod_ref[2:3, :]
    o_ref[...] = x_ref[...] + g1 * mix


def _merge(h, proj, na, x, mod3, gn, wa, wb, wo, tm=512):
    bn, S, d = x.shape
    tm = min(tm, S)
    nblk = S // tm
    d_a = h.shape[-1]
    d_b = na.shape[-1]
    oa_blk = BLK_OA * PROJ_TN // d_a
    mg_blk = BLK_MG * PROJ_TN // (2 * d)
    const = lambda b, i: (0, 0)
    return pl.pallas_call(
        _merge_kernel,
        grid=(bn, nblk),
        in_specs=[pl.BlockSpec((tm, d_a), lambda b, i: (b * nblk + i, 0)),
                  pl.BlockSpec((tm, d_a), lambda b, i: (b * nblk + i, oa_blk)),
                  pl.BlockSpec((tm, d_b), lambda b, i: (b * nblk + i, 0)),
                  pl.BlockSpec((tm, 2 * d), lambda b, i: (b * nblk + i, mg_blk)),
                  pl.BlockSpec((None, tm, d), lambda b, i: (b, i, 0)),
                  pl.BlockSpec((None, N_MOD, d), lambda b, i: (b, 0, 0)),
                  pl.BlockSpec((1, d_a), const),
                  pl.BlockSpec((d_a, d), const, pipeline_mode=pl.Buffered(1)),
                  pl.BlockSpec((d_b, d), const, pipeline_mode=pl.Buffered(1)),
                  pl.BlockSpec((d, d), const, pipeline_mode=pl.Buffered(1))],
        out_specs=pl.BlockSpec((None, tm, d), lambda b, i: (b, i, 0)),
        out_shape=jax.ShapeDtypeStruct((bn, S, d), F32),
        compiler_params=_cparams(("parallel", "parallel")),
        name="merge",
    )(h, proj, na, proj, x, mod3, gn, wa, wb, wo)


def _ffn_kernel(x_ref, mod_ref, g_ref, wg_ref, wu_ref, wd_ref, o_ref):
    x = x_ref[...]
    hx = _mod_norm(x, g_ref[...], mod_ref[4:5, :], mod_ref[3:4, :]).astype(BF16)
    a = jnp.dot(hx, wg_ref[...], preferred_element_type=F32)
    u = jnp.dot(hx, wu_ref[...], preferred_element_type=F32)
    act = (a * jax.nn.sigmoid(a) * u).astype(BF16)
    f = jnp.dot(act, wd_ref[...], preferred_element_type=F32)
    o_ref[...] = x + mod_ref[5:6, :] * f


def _ffn(x, mod3, g, wg, wu, wd, tm=512):
    bn, S, d = x.shape
    tm = min(tm, S)
    dff = wg.shape[1]
    const = lambda b, i: (0, 0)
    return pl.pallas_call(
        _ffn_kernel,
        grid=(bn, S // tm),
        in_specs=[pl.BlockSpec((None, tm, d), lambda b, i: (b, i, 0)),
                  pl.BlockSpec((None, N_MOD, d), lambda b, i: (b, 0, 0)),
                  pl.BlockSpec((1, d), const),
                  pl.BlockSpec((d, dff), const, pipeline_mode=pl.Buffered(1)),
                  pl.BlockSpec((d, dff), const, pipeline_mode=pl.Buffered(1)),
                  pl.BlockSpec((dff, d), const, pipeline_mode=pl.Buffered(1))],
        out_specs=pl.BlockSpec((None, tm, d), lambda b, i: (b, i, 0)),
        out_shape=jax.ShapeDtypeStruct((bn, S, d), F32),
        compiler_params=_cparams(("parallel", "parallel")),
        name="ffn",
    )(x, mod3, g.reshape(1, d), wg, wu, wd)


def _rope_tables(S):
    t = np.arange(S)
    row, col = t // GRID_W, t % GRID_W
    half = DK_A // 4
    inv = ROPE_THETA ** (-np.arange(half, dtype=np.float64) / half)
    ang_r = row[:, None] * inv[None, :]
    ang_c = col[:, None] * inv[None, :]
    cos = np.concatenate([np.cos(ang_r)] * 2 + [np.cos(ang_c)] * 2, axis=1)
    sin = np.concatenate([-np.sin(ang_r), np.sin(ang_r), -np.sin(ang_c), np.sin(ang_c)], axis=1)
    reps = LANES // DK_A
    return (jnp.asarray(np.tile(cos, (1, reps)), F32), jnp.asarray(np.tile(sin, (1, reps)), F32))


def kernel(x, c, ctx, c_ctx, w_mod, b_mod, norm1_g, w_in, b_gates, mlstm_norm_g, qn_g, kn_g, rpb,
           w_branch_a, w_branch_b, w_out, norm2_g, w_ffn_gate, w_ffn_up, w_ffn_down):
    bn, S, d = x.shape
    lctx = ctx.shape[1]
    rows = S // GRID_W
    depth = w_mod.shape[0]
    d_a, d_b = NH_A * DV_A, NH_B * DH_B
    dqk = NH_A * DK_A
    sizes = (dqk, dqk, d_a, d_a, 4 * NH_A, d_b, d_b, d_b, 2 * d)
    offs = np.cumsum((0,) + sizes)
    cos_t, sin_t = _rope_tables(S)
    gsum = jnp.asarray(np.kron(np.eye(MXU_TILE // DH_B), np.ones((DH_B, DH_B))), BF16)

    for l in range(depth):
        assert l == depth - 1, "context-stream update for non-final layers is not implemented"
        n_rows = -(-(bn + 1) // 8) * 8
        cc = jnp.concatenate([c, c_ctx[None, :], jnp.zeros((n_rows - bn - 1, d), F32)], axis=0)
        mod3 = _modulation(cc, w_mod[l], b_mod[l]).reshape(n_rows, N_MOD, d)

        w = w_in[l]
        seg = lambda i: w[:, int(offs[i]):int(offs[i + 1])]
        w_main = jnp.concatenate([seg(0), seg(1), seg(2), seg(3), seg(5), seg(6), seg(7), seg(8)],
                                 axis=1).astype(BF16)
        w_g = jnp.pad(seg(4), ((0, 0), (0, LANES - 4 * NH_A))).astype(BF16)
        b_g = jnp.pad(b_gates[l].reshape(1, 4 * NH_A), ((0, 0), (0, LANES - 4 * NH_A)))
        qk_gain = jnp.stack([jnp.tile(qn_g[l], PROJ_TN // DH_B) * (DH_B ** -0.5 * LOG2E),
                             jnp.tile(kn_g[l], PROJ_TN // DH_B)])

        px, ktx, gcx, grx = _mixer_in(x, mod3, norm1_g[l], w_main, w_g, b_g, qk_gain, cos_t, sin_t, gsum,
                                      rope=True, blocks=tuple(range(N_PROJ_BLK)))
        pc, ktc, _, grc = _mixer_in(ctx, mod3, norm1_g[l], w_main, w_g, b_g, qk_gain, cos_t, sin_t, gsum,
                                    rope=False, blocks=CTX_BLOCKS, ctx_row=bn)

        zero_state = jnp.zeros((bn, 2, N_PAIR, 2 * DK_A, C_EXT), F32)
        _, _, st_ctx = _mlstm_states(pc, ktc, grc, zero_state, bn=bn, seq_len=lctx)
        cf, cb, _ = _mlstm_states(px, ktx, grx, st_ctx, bn=bn, seq_len=S)
        h_a = _mlstm_outputs(px, ktx, gcx, grx, cf, cb, bn=bn, seq_len=S)

        shift, bounded = _natten_shift(rpb[l], qn_g[l], kn_g[l])
        bias = _natten_bias(rpb[l], rows, shift)
        na = lax.cond(bounded,
                      functools.partial(_natten, bn=bn, seq_len=S, lctx=lctx, bounded=True),
                      functools.partial(_natten, bn=bn, seq_len=S, lctx=lctx, bounded=False),
                      px, pc, bias, shift)

        x_mid = _merge(h_a, px, na, x, mod3, mlstm_norm_g[l].reshape(1, d_a),
                       w_branch_a[l].astype(BF16), w_branch_b[l].astype(BF16), w_out[l].astype(BF16))
        x = _ffn(x_mid, mod3, norm2_g[l], w_ffn_gate[l].astype(BF16), w_ffn_up[l].astype(BF16),
                 w_ffn_down[l].astype(BF16))
    return x
```

```python
import functools

import numpy as np
import jax
import jax.numpy as jnp
from jax import lax
from jax.experimental import pallas as pl
from jax.experimental.pallas import tpu as pltpu

F32 = jnp.float32
BF16 = jnp.bfloat16

GRID_W = 64
NH_A, DK_A, DV_A = 8, 64, 128
NH_B, DH_B = 16, 64
CHUNK = 128
GATE_CAP = 15.0
WIN_R, WIN_C = 8, 16
ROPE_THETA = 10000.0
EPS = 1e-6
N_MOD = 6
LANES = 128
MXU_TILE = 256
VMEM_LIMIT = 56 * 1024 * 1024

QROWS = 4
KROWS = QROWS + WIN_R - 1

N_PAIR = NH_A // 2
MLSTM_CPS = 8
C_EXT = 2 * DV_A

G_A, G_WL, G_B, G_TOT = 0, 32, 72, 104


def _cparams(sem):
    return pltpu.CompilerParams(dimension_semantics=sem, vmem_limit_bytes=VMEM_LIMIT)


def _mod_kernel(c_ref, w_ref, b_ref, o_ref):
    c = c_ref[...]
    h = c * jax.nn.sigmoid(c)
    o_ref[...] = jnp.dot(h, w_ref[...], preferred_element_type=F32,
                         precision=lax.Precision.HIGHEST) + b_ref[...]


def _modulation(cc, w_mod, b_mod):
    rows, d = cc.shape
    n = w_mod.shape[1]
    tn = 1024
    return pl.pallas_call(
        _mod_kernel,
        grid=(n // tn,),
        in_specs=[pl.BlockSpec((rows, d), lambda j: (0, 0)),
                  pl.BlockSpec((d, tn), lambda j: (0, j)),
                  pl.BlockSpec((1, tn), lambda j: (0, j))],
        out_specs=pl.BlockSpec((rows, tn), lambda j: (0, j)),
        out_shape=jax.ShapeDtypeStruct((rows, n), F32),
        compiler_params=_cparams(("arbitrary",)),
        name="modulation",
    )(cc, w_mod, b_mod.reshape(1, n))


def _mod_norm(x, gain, scale, shift):
    ms = jnp.mean(x * x, axis=-1, keepdims=True)
    y = x * lax.rsqrt(ms + EPS) * gain
    return y * (1.0 + scale) + shift


def _gate_pack(g, gc_ref, gr_ref):
    tm = g.shape[0]
    g = GATE_CAP * jnp.tanh(g * (1.0 / GATE_CAP))
    lane_t = lax.broadcasted_iota(jnp.int32, g.shape, 1)
    is_forget = (lane_t & NH_A) != 0
    x = jnp.where(is_forget, jax.nn.log_sigmoid(g), g)
    x = jnp.where(lane_t < 4 * NH_A, x, 0.0)

    t_idx = lax.broadcasted_iota(jnp.int32, (CHUNK, CHUNK), 0)
    lane = lax.broadcasted_iota(jnp.int32, (CHUNK, CHUNK), 1)
    fwd_half = lane < 2 * NH_A
    low = lane < 4 * NH_A
    for ci in range(tm // CHUNK):
        xc = x[ci * CHUNK:(ci + 1) * CHUNK]
        cf = xc
        step = 1
        while step < CHUNK:
            cf = cf + jnp.where(t_idx >= step, pltpu.roll(cf, step, 0), 0.0)
            step *= 2
        tot = jnp.broadcast_to(cf[CHUNK - 1:CHUNK, :], xc.shape)
        cb = tot - cf + xc
        r_b = jnp.where(fwd_half, pltpu.roll(cf, LANES - NH_A, 1), pltpu.roll(cb, LANES - NH_A, 1))
        a = xc - r_b
        wl = xc + (pltpu.roll(tot, LANES - NH_A, 1) - r_b)
        bsel = jnp.where(fwd_half, cf, cb)
        z = lambda v: jnp.where(low, v, 0.0)
        packed = z(a) + pltpu.roll(z(wl), 32, 1) + pltpu.roll(z(bsel), 64, 1) + pltpu.roll(z(tot), 96, 1)
        gc_ref[ci * CHUNK:(ci + 1) * CHUNK, :] = packed
        gr_ref[:, ci * CHUNK:(ci + 1) * CHUNK] = packed.T


PROJ_TN = 512
BLK_QA, BLK_KA, BLK_VA, BLK_OA, BLK_QB, BLK_KB, BLK_VB, BLK_MG = 0, 1, 2, 4, 6, 8, 10, 12
N_PROJ_BLK = 16
CTX_BLOCKS = (BLK_KA, BLK_VA, BLK_VA + 1, BLK_KB, BLK_KB + 1, BLK_VB, BLK_VB + 1)
LOG2E = 1.4426950408889634


def _swap16(x):
    n = x.shape[-1]
    lane = lax.broadcasted_iota(jnp.int32, x.shape, x.ndim - 1)
    first = (lane & 31) < 16
    return jnp.where(first, pltpu.roll(x, n - 16, x.ndim - 1), pltpu.roll(x, 16, x.ndim - 1))


def _mixer_in_kernel(x_ref, mod_ref, g_ref, w_ref, wg_ref, bg_ref, qk_ref, cos_ref, sin_ref, gsum_ref,
                     o_ref, kt_ref, gc_ref, gr_ref, *, rope, blocks):
    tm = x_ref.shape[0]
    hx = _mod_norm(x_ref[...], g_ref[...], mod_ref[1:2, :], mod_ref[0:1, :]).astype(BF16)
    _gate_pack(jnp.dot(hx, wg_ref[...], preferred_element_type=F32) + bg_ref[...], gc_ref, gr_ref)
    if rope:
        reps = PROJ_TN // LANES
        cos = jnp.concatenate([cos_ref[...]] * reps, axis=1)
        sin = jnp.concatenate([sin_ref[...]] * reps, axis=1)
    for blk in range(N_PROJ_BLK):
        cols = slice(blk * PROJ_TN, (blk + 1) * PROJ_TN)
        if blk not in blocks:
            o_ref[:, cols] = jnp.zeros((tm, PROJ_TN), o_ref.dtype)
            continue
        acc = jnp.dot(hx, w_ref[:, cols], preferred_element_type=F32)
        if blk < BLK_VA:
            if rope:
                acc = acc * cos + _swap16(acc) * sin
            if blk == BLK_QA:
                acc = acc * DK_A ** -0.5
        elif BLK_QB <= blk < BLK_VB:
            sq = (acc * acc).astype(BF16)
            gw = gsum_ref.shape[0]
            ss = jnp.concatenate([jnp.dot(sq[:, c0:c0 + gw], gsum_ref[...], preferred_element_type=F32)
                                  for c0 in range(0, PROJ_TN, gw)], axis=1)
            gain = qk_ref[0:1, :] if blk < BLK_KB else qk_ref[1:2, :]
            acc = acc * lax.rsqrt(ss * (1.0 / DH_B) + EPS) * gain
        if blk == BLK_KA:
            kt_ref[...] = acc.T.astype(kt_ref.dtype)
        o_ref[:, cols] = acc.astype(o_ref.dtype)


def _mixer_in(x, mod3, g, w, w_g, b_g, qk_gain, cos, sin, gsum, *, rope, blocks, ctx_row=None, tm=512):
    bn, L, d = x.shape
    tm = min(tm, L)
    nblk = L // tm
    n = w.shape[1]
    mod_map = (lambda b, i: (b, 0, 0)) if ctx_row is None else (lambda b, i: (ctx_row, 0, 0))
    const = lambda b, i: (0, 0)
    tok = lambda b, i: (b * nblk + i, 0)
    resident = pl.Buffered(1)
    return pl.pallas_call(
        functools.partial(_mixer_in_kernel, rope=rope, blocks=blocks),
        grid=(bn, nblk),
        in_specs=[pl.BlockSpec((None, tm, d), lambda b, i: (b, i, 0)),
                  pl.BlockSpec((None, N_MOD, d), mod_map),
                  pl.BlockSpec((1, d), const),
                  pl.BlockSpec((d, n), const, pipeline_mode=resident),
                  pl.BlockSpec((d, LANES), const, pipeline_mode=resident),
                  pl.BlockSpec((1, LANES), const),
                  pl.BlockSpec((2, PROJ_TN), const),
                  pl.BlockSpec((tm, LANES), lambda b, i: (i, 0)),
                  pl.BlockSpec((tm, LANES), lambda b, i: (i, 0)),
                  pl.BlockSpec(gsum.shape, const, pipeline_mode=resident)],
        out_specs=[pl.BlockSpec((tm, n), tok),
                   pl.BlockSpec((PROJ_TN, tm), lambda b, i: (0, b * nblk + i)),
                   pl.BlockSpec((tm, LANES), tok),
                   pl.BlockSpec((LANES, tm), lambda b, i: (0, b * nblk + i))],
        out_shape=[jax.ShapeDtypeStruct((bn * L, n), BF16),
                   jax.ShapeDtypeStruct((PROJ_TN, bn * L), BF16),
                   jax.ShapeDtypeStruct((bn * L, LANES), F32),
                   jax.ShapeDtypeStruct((LANES, bn * L), F32)],
        compiler_params=_cparams(("parallel", "parallel")),
        name="mixer_in",
    )(x, mod3, g.reshape(1, d), w, w_g, b_g, qk_gain, cos, sin, gsum)


def _mlstm_state_kernel(kf_ref, vf_ref, gf_ref, kb_ref, vb_ref, gb_ref, c0_ref,
                        cf_ref, cb_ref, cfin_ref, st, *, nsteps, cps):
    s = pl.program_id(1)

    @pl.when(s == 0)
    def _():
        st[...] = c0_ref[...]

    lc = CHUNK
    ones_blk = jnp.ones((lc, DV_A), BF16)
    for c in range(cps):
        for d, (k_ref, v_ref, g_ref, out_ref) in enumerate(((kf_ref, vf_ref, gf_ref, cf_ref),
                                                            (kb_ref, vb_ref, gb_ref, cb_ref))):
            cc = c if d == 0 else cps - 1 - c
            tok = slice(cc * lc, (cc + 1) * lc)
            g = g_ref[:, tok]
            for h in range(NH_A):
                p, hh = divmod(h, 2)
                rows = slice(hh * DK_A, (hh + 1) * DK_A)
                wl = G_WL + 16 * d + h
                tl = G_TOT + 16 * d + h
                kw = (k_ref[h * DK_A:(h + 1) * DK_A, tok].astype(F32) * jnp.exp(g[wl:wl + 1, :])).astype(BF16)
                vext = jnp.concatenate([v_ref[tok, h * DV_A:(h + 1) * DV_A], ones_blk], axis=1)
                upd = jnp.dot(kw, vext, preferred_element_type=F32)
                c_old = st[d, p, rows, :]
                out_ref[cc, p, hh] = c_old.astype(out_ref.dtype)
                st[d, p, rows, :] = jnp.exp(g[tl:tl + 1, 0:1]) * c_old + upd

    @pl.when(s == nsteps - 1)
    def _():
        cfin_ref[...] = st[...]


def _mlstm_states(proj, kt, gr, c0, *, bn, seq_len):
    nc = seq_len // CHUNK
    cps = min(MLSTM_CPS, nc)
    nsteps = nc // cps
    lb = cps * CHUNK
    vcol = BLK_VA * PROJ_TN // (NH_A * DV_A)
    fwd = lambda b, s: b * nsteps + s
    bwd = lambda b, s: b * nsteps + nsteps - 1 - s
    st_shape = (2, N_PAIR, 2 * DK_A, C_EXT)
    out_blk = (None, cps, N_PAIR, 2, DK_A, C_EXT)
    return pl.pallas_call(
        functools.partial(_mlstm_state_kernel, nsteps=nsteps, cps=cps),
        grid=(bn, nsteps),
        in_specs=[pl.BlockSpec((NH_A * DK_A, lb), lambda b, s: (0, fwd(b, s))),
                  pl.BlockSpec((lb, NH_A * DV_A), lambda b, s: (fwd(b, s), vcol)),
                  pl.BlockSpec((LANES, lb), lambda b, s: (0, fwd(b, s))),
                  pl.BlockSpec((NH_A * DK_A, lb), lambda b, s: (0, bwd(b, s))),
                  pl.BlockSpec((lb, NH_A * DV_A), lambda b, s: (bwd(b, s), vcol)),
                  pl.BlockSpec((LANES, lb), lambda b, s: (0, bwd(b, s))),
                  pl.BlockSpec((None,) + st_shape, lambda b, s: (b, 0, 0, 0, 0))],
        out_specs=[pl.BlockSpec(out_blk, lambda b, s: (b, s, 0, 0, 0, 0)),
                   pl.BlockSpec(out_blk, lambda b, s: (b, nsteps - 1 - s, 0, 0, 0, 0)),
                   pl.BlockSpec((None,) + st_shape, lambda b, s: (b, 0, 0, 0, 0))],
        out_shape=[jax.ShapeDtypeStruct((bn, nc, N_PAIR, 2, DK_A, C_EXT), BF16),
                   jax.ShapeDtypeStruct((bn, nc, N_PAIR, 2, DK_A, C_EXT), BF16),
                   jax.ShapeDtypeStruct((bn,) + st_shape, F32)],
        scratch_shapes=[pltpu.VMEM(st_shape, F32)],
        compiler_params=_cparams(("parallel", "arbitrary")),
        name="mlstm_states",
    )(kt, proj, gr, kt, proj, gr, c0)


def _mlstm_out_kernel(q_ref, kt_ref, v_ref, gc_ref, gr_ref, cf_ref, cb_ref, h_ref):
    lc = CHUNK
    t_idx = lax.broadcasted_iota(jnp.int32, (lc, lc), 0)
    s_idx = lax.broadcasted_iota(jnp.int32, (lc, lc), 1)
    visible = (s_idx <= t_idx, s_idx >= t_idx)
    lo = lax.broadcasted_iota(jnp.int32, (lc, 2 * DK_A), 1) < DK_A
    ones_blk = jnp.ones((lc, DV_A), BF16)
    zero_c = jnp.zeros((DK_A, C_EXT), BF16)
    for c in range(q_ref.shape[0] // lc):
        tok = slice(c * lc, (c + 1) * lc)
        gc = gc_ref[tok, :]
        gr = gr_ref[:, tok]
        for p in range(N_PAIR):
            qp = q_ref[tok, p * 2 * DK_A:(p + 1) * 2 * DK_A]
            ktp = kt_ref[p * 2 * DK_A:(p + 1) * 2 * DK_A, tok]
            zq = jnp.zeros_like(qp)
            q_stack = jnp.concatenate([jnp.where(lo, qp, zq), jnp.where(lo, zq, qp)], axis=0)
            s_both = jnp.dot(q_stack, ktp, preferred_element_type=F32)
            b_rep = [[jnp.broadcast_to(gc[:, G_B + 16 * d + 2 * p + hh:G_B + 16 * d + 2 * p + hh + 1], (lc, lc))
                      for hh in range(2)] for d in range(2)]
            inter = []
            for d, c_ref in enumerate((cf_ref, cb_ref)):
                e = jnp.where(lo, jnp.exp(b_rep[d][0]), jnp.exp(b_rep[d][1]))
                qs = (qp.astype(F32) * e).astype(BF16)
                c_pair = jnp.concatenate([jnp.concatenate([c_ref[c, p, 0], zero_c], axis=1),
                                          jnp.concatenate([zero_c, c_ref[c, p, 1]], axis=1)], axis=0)
                inter.append(jnp.dot(qs, c_pair, preferred_element_type=F32))
            for hh in range(2):
                h = 2 * p + hh
                s_h = s_both[hh * lc:(hh + 1) * lc]
                vext = jnp.concatenate([v_ref[tok, h * DV_A:(h + 1) * DV_A], ones_blk], axis=1)
                acc = None
                for d in range(2):
                    a_row = gr[G_A + 16 * d + h:G_A + 16 * d + h + 1, :]
                    log_d = jnp.where(visible[d], b_rep[d][hh] + a_row, -jnp.inf)
                    pm = (s_h * jnp.exp(log_d)).astype(BF16)
                    nd = (jnp.dot(pm, vext, preferred_element_type=F32)
                          + inter[d][:, hh * C_EXT:(hh + 1) * C_EXT])
                    hd = nd[:, :DV_A] / jnp.maximum(jnp.abs(nd[:, DV_A:]), 1.0)
                    acc = hd if acc is None else acc + hd
                h_ref[tok, h * DV_A:(h + 1) * DV_A] = acc.astype(h_ref.dtype)


def _mlstm_outputs(proj, kt, gc, gr, cf, cb, *, bn, seq_len):
    cps = min(MLSTM_CPS, seq_len // CHUNK)
    lc = cps * CHUNK
    nc = seq_len // lc
    qcol = BLK_QA * PROJ_TN // (NH_A * DK_A)
    vcol = BLK_VA * PROJ_TN // (NH_A * DV_A)
    tok = lambda b, c: b * nc + c
    st_blk = (None, cps, N_PAIR, 2, DK_A, C_EXT)
    st_map = lambda b, c: (b, c, 0, 0, 0, 0)
    return pl.pallas_call(
        _mlstm_out_kernel,
        grid=(bn, nc),
        in_specs=[pl.BlockSpec((lc, NH_A * DK_A), lambda b, c: (tok(b, c), qcol)),
                  pl.BlockSpec((NH_A * DK_A, lc), lambda b, c: (0, tok(b, c))),
                  pl.BlockSpec((lc, NH_A * DV_A), lambda b, c: (tok(b, c), vcol)),
                  pl.BlockSpec((lc, LANES), lambda b, c: (tok(b, c), 0)),
                  pl.BlockSpec((LANES, lc), lambda b, c: (0, tok(b, c))),
                  pl.BlockSpec(st_blk, st_map),
                  pl.BlockSpec(st_blk, st_map)],
        out_specs=pl.BlockSpec((lc, NH_A * DV_A), lambda b, c: (tok(b, c), 0)),
        out_shape=jax.ShapeDtypeStruct((bn * seq_len, NH_A * DV_A), F32),
        compiler_params=_cparams(("parallel", "parallel")),
        name="mlstm_outputs",
    )(proj, kt, proj, gc, gr, cf, cb)


NAT_QB = 16
NAT_SAFE_RANGE = 100.0


def _natten_kernel(shift_ref, q_ref, k_ref, v_ref, kc_ref, vc_ref, ba_ref, bm_ref, bb_ref, o_ref, *,
                   seq_len, bounded):
    i2 = pl.program_id(2)
    nq = QROWS * GRID_W
    nk = KROWS * GRID_W
    pw = 2 * DH_B
    shift = shift_ref[0:1, 0:1]
    spare = (DH_B, 0)
    nt = (((1,), (1,)), ((), ()))

    def with_ones_lane(v, h):
        lane = lax.broadcasted_iota(jnp.int32, v.shape, 1)
        own = (lane < DH_B) if h == 0 else (lane >= DH_B)
        return jnp.where(own, v, (lane == spare[h]).astype(v.dtype))

    kc = kc_ref[...]
    vc_pair = [with_ones_lane(vc_ref[...], h) for h in range(2)]
    lo_o = lax.broadcasted_iota(jnp.int32, (nq, pw), 1) < DH_B
    zq = jnp.zeros((nq, pw), BF16)
    qb = q_ref.shape[0] // nq
    bias_refs = (ba_ref,) + (bm_ref,) * (qb - 2) + (bb_ref,)
    for j, bias_ref in enumerate(bias_refs):
        blk = qb * i2 + j
        start = jnp.clip(blk * nq - (WIN_R // 2) * GRID_W, 0, seq_len - nk)
        start = pl.multiple_of(start, GRID_W)
        q = q_ref[j * nq:(j + 1) * nq, :]
        kb = k_ref[pl.ds(start, nk), :]
        vb = v_ref[pl.ds(start, nk), :]
        outs = []
        for h in range(2):
            qh = jnp.where(lo_o if h == 0 else ~lo_o, q, zq)
            s_win = lax.dot_general(qh, kb, nt, preferred_element_type=F32) + bias_ref[h]
            s_ctx = lax.dot_general(qh, kc, nt, preferred_element_type=F32)
            if bounded:
                m = shift
            else:
                m = jnp.maximum(jnp.max(s_win, axis=1, keepdims=True), jnp.max(s_ctx, axis=1, keepdims=True))
                s_win = s_win - m
            o = (jnp.dot(jnp.exp2(s_win).astype(BF16), with_ones_lane(vb, h), preferred_element_type=F32)
                 + jnp.dot(jnp.exp2(s_ctx - m).astype(BF16), vc_pair[h], preferred_element_type=F32))
            outs.append(o / o[:, spare[h]:spare[h] + 1])
        o_ref[j * nq:(j + 1) * nq, :] = jnp.where(lo_o, outs[0], outs[1]).astype(o_ref.dtype)


def _natten_patterns(rows):
    nblk = rows // QROWS
    pats = []
    for blk in (0, 1 if nblk > 2 else 0, nblk - 1):
        r0 = blk * QROWS
        k0 = int(np.clip(r0 - WIN_R // 2, 0, rows - KROWS))
        qr = r0 + np.arange(QROWS)
        kr = k0 + np.arange(KROWS)
        wr0 = np.clip(qr - WIN_R // 2, 0, rows - WIN_R)
        valid = (kr[None, :] >= wr0[:, None]) & (kr[None, :] < wr0[:, None] + WIN_R)
        dr = np.clip(kr[None, :] - qr[:, None] + WIN_R - 1, 0, 2 * WIN_R - 2)
        pats.append((valid, dr))
    return pats


def _bias_kernel(t_ref, o_ref, *, patterns):
    pat = pl.program_id(0)
    for ps, (valid, dr) in enumerate(patterns):
        @pl.when(pat == ps)
        def _():
            for xq in range(QROWS):
                for yk in range(KROWS):
                    if valid[xq, yk]:
                        blk = t_ref[int(dr[xq, yk])]
                    else:
                        blk = jnp.full((GRID_W, GRID_W), -jnp.inf, F32)
                    o_ref[xq * GRID_W:(xq + 1) * GRID_W, yk * GRID_W:(yk + 1) * GRID_W] = blk


def _natten_shift(rpb, qn_g, kn_g):
    qmax = jnp.max(jnp.abs(qn_g)) * (DH_B ** -0.5 * LOG2E) * DH_B ** 0.5
    kmax = jnp.max(jnp.abs(kn_g)) * DH_B ** 0.5
    bound = 1.02 * qmax * kmax
    bias_hi = jnp.maximum(jnp.max(rpb) * LOG2E, 0.0)
    bias_lo = jnp.minimum(jnp.min(rpb) * LOG2E, 0.0)
    shift = bound + bias_hi
    spread = shift + bound - bias_lo
    bounded = spread <= NAT_SAFE_RANGE
    return jnp.where(bounded, shift, 0.0).astype(F32), bounded


def _natten_bias(rpb, rows, shift):
    nh = rpb.shape[0]
    qc = np.arange(GRID_W)
    kc = np.arange(GRID_W)
    wc0 = np.clip(qc - WIN_C // 2, 0, GRID_W - WIN_C)
    in_c = (kc[None, :] >= wc0[:, None]) & (kc[None, :] < wc0[:, None] + WIN_C)
    dc = np.clip(kc[None, :] - qc[:, None] + WIN_C - 1, 0, 2 * WIN_C - 2)
    sel_c = jnp.asarray(dc[:, :, None] == np.arange(2 * WIN_C - 1), F32)
    tcol = jnp.einsum('hab,uvb->hauv', rpb.astype(F32) * LOG2E, sel_c, precision=lax.Precision.HIGHEST)
    tcol = jnp.where(in_c[None, None], tcol - shift, -jnp.inf)
    nq, nk = QROWS * GRID_W, KROWS * GRID_W
    n_dr = 2 * WIN_R - 1
    return pl.pallas_call(
        functools.partial(_bias_kernel, patterns=_natten_patterns(rows)),
        grid=(3, nh),
        in_specs=[pl.BlockSpec((None, n_dr, GRID_W, GRID_W), lambda p, h: (h, 0, 0, 0))],
        out_specs=pl.BlockSpec((None, None, nq, nk), lambda p, h: (p, h, 0, 0)),
        out_shape=jax.ShapeDtypeStruct((3, nh, nq, nk), F32),
        compiler_params=_cparams(("arbitrary", "arbitrary")),
        name="natten_bias",
    )(tcol)


def _natten(px, pc, bias, shift, *, bn, seq_len, lctx, bounded):
    nq = QROWS * GRID_W
    nk = KROWS * GRID_W
    qb = min(NAT_QB, seq_len // nq)
    nstep = seq_len // (qb * nq)
    pw = 2 * DH_B
    n_pair = NH_B // 2
    qcol, kcol, vcol = (blk * PROJ_TN // pw for blk in (BLK_QB, BLK_KB, BLK_VB))
    last = nstep - 1
    first_pat = lambda i: jnp.where(i == 0, 0, 1)
    second_pat = lambda i: jnp.where(i == last, 2, 1)
    return pl.pallas_call(
        functools.partial(_natten_kernel, seq_len=seq_len, bounded=bounded),
        grid=(bn, n_pair, nstep),
        in_specs=[pl.BlockSpec((1, pw), lambda b, p, i: (0, 0)),
                  pl.BlockSpec((qb * nq, pw), lambda b, p, i: (b * nstep + i, qcol + p)),
                  pl.BlockSpec((seq_len, pw), lambda b, p, i: (b, kcol + p)),
                  pl.BlockSpec((seq_len, pw), lambda b, p, i: (b, vcol + p)),
                  pl.BlockSpec((lctx, pw), lambda b, p, i: (b, kcol + p)),
                  pl.BlockSpec((lctx, pw), lambda b, p, i: (b, vcol + p)),
                  pl.BlockSpec((None, 2, nq, nk), lambda b, p, i: (first_pat(i), p, 0, 0)),
                  pl.BlockSpec((None, 2, nq, nk), lambda b, p, i: (1, p, 0, 0)),
                  pl.BlockSpec((None, 2, nq, nk), lambda b, p, i: (second_pat(i), p, 0, 0))],
        out_specs=pl.BlockSpec((qb * nq, pw), lambda b, p, i: (b * nstep + i, p)),
        out_shape=jax.ShapeDtypeStruct((bn * seq_len, NH_B * DH_B), BF16),
        compiler_params=_cparams(("parallel", "parallel", "arbitrary")),
        name="natten_bounded" if bounded else "natten",
    )(jnp.full((1, pw), shift, F32), px, px, px, pc, pc, bias, bias, bias)


def _merge_kernel(h_ref, oa_ref, na_ref, mg_ref, x_ref, mod_ref, gn_ref, wa_ref, wb_ref, wo_ref, o_ref):
    h = h_ref[...]
    parts = []
    for hd in range(NH_A):
        hh = h[:, hd * DV_A:(hd + 1) * DV_A]
        ms = jnp.mean(hh * hh, axis=-1, keepdims=True)
        parts.append(hh * lax.rsqrt(ms + EPS))
    ha = jnp.concatenate(parts, axis=1) * gn_ref[...]
    ha = (ha * jax.nn.sigmoid(oa_ref[...].astype(F32))).astype(BF16)
    d = wa_ref.shape[1]
    gates = jax.nn.sigmoid(mg_ref[...].astype(F32))
    t = (gates[:, :d] * jnp.dot(ha, wa_ref[...], preferred_element_type=F32)
         + gates[:, d:] * jnp.dot(na_ref[...], wb_ref[...], preferred_element_type=F32))
    mix = jnp.dot(t.astype(BF16), wo_ref[...], preferred_element_type=F32)
    g1 = mod_ref[2:3, :]
    o_ref[...] = x_ref[...] + g1 * mix


def _merge(h, proj, na, x, mod3, gn, wa, wb, wo, tm=512):
    bn, S, d = x.shape
    tm = min(tm, S)
    nblk = S // tm
    d_a = h.shape[-1]
    d_b = na.shape[-1]
    oa_blk = BLK_OA * PROJ_TN // d_a
    mg_blk = BLK_MG * PROJ_TN // (2 * d)
    const = lambda b, i: (0, 0)
    return pl.pallas_call(
        _merge_kernel,
        grid=(bn, nblk),
        in_specs=[pl.BlockSpec((tm, d_a), lambda b, i: (b * nblk + i, 0)),
                  pl.BlockSpec((tm, d_a), lambda b, i: (b * nblk + i, oa_blk)),
                  pl.BlockSpec((tm, d_b), lambda b, i: (b * nblk + i, 0)),
                  pl.BlockSpec((tm, 2 * d), lambda b, i: (b * nblk + i, mg_blk)),
                  pl.BlockSpec((None, tm, d), lambda b, i: (b, i, 0)),
                  pl.BlockSpec((None, N_MOD, d), lambda b, i: (b, 0, 0)),
                  pl.BlockSpec((1, d_a), const),
                  pl.BlockSpec((d_a, d), const, pipeline_mode=pl.Buffered(1)),
                  pl.BlockSpec((d_b, d), const, pipeline_mode=pl.Buffered(1)),
                  pl.BlockSpec((d, d), const, pipeline_mode=pl.Buffered(1))],
        out_specs=pl.BlockSpec((None, tm, d), lambda b, i: (b, i, 0)),
        out_shape=jax.ShapeDtypeStruct((bn, S, d), F32),
        compiler_params=_cparams(("parallel", "parallel")),
        name="merge",
    )(h, proj, na, proj, x, mod3, gn, wa, wb, wo)


def _ffn_kernel(x_ref, mod_ref, g_ref, wg_ref, wu_ref, wd_ref, o_ref):
    x = x_ref[...]
    hx = _mod_norm(x, g_ref[...], mod_ref[4:5, :], mod_ref[3:4, :]).astype(BF16)
    a = jnp.dot(hx, wg_ref[...], preferred_element_type=F32)
    u = jnp.dot(hx, wu_ref[...], preferred_element_type=F32)
    act = (a * jax.nn.sigmoid(a) * u).astype(BF16)
    f = jnp.dot(act, wd_ref[...], preferred_element_type=F32)
    o_ref[...] = x + mod_ref[5:6, :] * f


def _ffn(x, mod3, g, wg, wu, wd, tm=512):
    bn, S, d = x.shape
    tm = min(tm, S)
    dff = wg.shape[1]
    const = lambda b, i: (0, 0)
    return pl.pallas_call(
        _ffn_kernel,
        grid=(bn, S // tm),
        in_specs=[pl.BlockSpec((None, tm, d), lambda b, i: (b, i, 0)),
                  pl.BlockSpec((None, N_MOD, d), lambda b, i: (b, 0, 0)),
                  pl.BlockSpec((1, d), const),
                  pl.BlockSpec((d, dff), const, pipeline_mode=pl.Buffered(1)),
                  pl.BlockSpec((d, dff), const, pipeline_mode=pl.Buffered(1)),
                  pl.BlockSpec((dff, d), const, pipeline_mode=pl.Buffered(1))],
        out_specs=pl.BlockSpec((None, tm, d), lambda b, i: (b, i, 0)),
        out_shape=jax.ShapeDtypeStruct((bn, S, d), F32),
        compiler_params=_cparams(("parallel", "parallel")),
        name="ffn",
    )(x, mod3, g.reshape(1, d), wg, wu, wd)


def _rope_tables(S):
    t = np.arange(S)
    row, col = t // GRID_W, t % GRID_W
    half = DK_A // 4
    inv = ROPE_THETA ** (-np.arange(half, dtype=np.float64) / half)
    ang_r = row[:, None] * inv[None, :]
    ang_c = col[:, None] * inv[None, :]
    cos = np.concatenate([np.cos(ang_r)] * 2 + [np.cos(ang_c)] * 2, axis=1)
    sin = np.concatenate([-np.sin(ang_r), np.sin(ang_r), -np.sin(ang_c), np.sin(ang_c)], axis=1)
    reps = LANES // DK_A
    return (jnp.asarray(np.tile(cos, (1, reps)), F32), jnp.asarray(np.tile(sin, (1, reps)), F32))


def kernel(x, c, ctx, c_ctx, w_mod, b_mod, norm1_g, w_in, b_gates, mlstm_norm_g, qn_g, kn_g, rpb,
           w_branch_a, w_branch_b, w_out, norm2_g, w_ffn_gate, w_ffn_up, w_ffn_down):
    bn, S, d = x.shape
    lctx = ctx.shape[1]
    rows = S // GRID_W
    depth = w_mod.shape[0]
    d_a, d_b = NH_A * DV_A, NH_B * DH_B
    dqk = NH_A * DK_A
    sizes = (dqk, dqk, d_a, d_a, 4 * NH_A, d_b, d_b, d_b, 2 * d)
    offs = np.cumsum((0,) + sizes)
    cos_t, sin_t = _rope_tables(S)
    gsum = jnp.asarray(np.kron(np.eye(MXU_TILE // DH_B), np.ones((DH_B, DH_B))), BF16)

    for l in range(depth):
        assert l == depth - 1, "context-stream update for non-final layers is not implemented"
        n_rows = -(-(bn + 1) // 8) * 8
        cc = jnp.concatenate([c, c_ctx[None, :], jnp.zeros((n_rows - bn - 1, d), F32)], axis=0)
        mod3 = _modulation(cc, w_mod[l], b_mod[l]).reshape(n_rows, N_MOD, d)

        w = w_in[l]
        seg = lambda i: w[:, int(offs[i]):int(offs[i + 1])]
        w_main = jnp.concatenate([seg(0), seg(1), seg(2), seg(3), seg(5), seg(6), seg(7), seg(8)],
                                 axis=1).astype(BF16)
        w_g = jnp.pad(seg(4), ((0, 0), (0, LANES - 4 * NH_A))).astype(BF16)
        b_g = jnp.pad(b_gates[l].reshape(1, 4 * NH_A), ((0, 0), (0, LANES - 4 * NH_A)))
        qk_gain = jnp.stack([jnp.tile(qn_g[l], PROJ_TN // DH_B) * (DH_B ** -0.5 * LOG2E),
                             jnp.tile(kn_g[l], PROJ_TN // DH_B)])

        px, ktx, gcx, grx = _mixer_in(x, mod3, norm1_g[l], w_main, w_g, b_g, qk_gain, cos_t, sin_t, gsum,
                                      rope=True, blocks=tuple(range(N_PROJ_BLK)))
        pc, ktc, _, grc = _mixer_in(ctx, mod3, norm1_g[l], w_main, w_g, b_g, qk_gain, cos_t, sin_t, gsum,
                                    rope=False, blocks=CTX_BLOCKS, ctx_row=bn)

        zero_state = jnp.zeros((bn, 2, N_PAIR, 2 * DK_A, C_EXT), F32)
        _, _, st_ctx = _mlstm_states(pc, ktc, grc, zero_state, bn=bn, seq_len=lctx)
        cf, cb, _ = _mlstm_states(px, ktx, grx, st_ctx, bn=bn, seq_len=S)
        h_a = _mlstm_outputs(px, ktx, gcx, grx, cf, cb, bn=bn, seq_len=S)

        shift, bounded = _natten_shift(rpb[l], qn_g[l], kn_g[l])
        bias = _natten_bias(rpb[l], rows, shift)
        na = lax.cond(bounded,
                      functools.partial(_natten, bn=bn, seq_len=S, lctx=lctx, bounded=True),
                      functools.partial(_natten, bn=bn, seq_len=S, lctx=lctx, bounded=False),
                      px, pc, bias, shift)

        x_mid = _merge(h_a, px, na, x, mod3, mlstm_norm_g[l].reshape(1, d_a),
                       w_branch_a[l].astype(BF16), w_branch_b[l].astype(BF16), w_out[l].astype(BF16))
        x = _ffn(x_mid, mod3, norm2_g[l], w_ffn_gate[l].astype(BF16), w_ffn_up[l].astype(BF16),
                 w_ffn_down[l].astype(BF16))
    return x
```

```python
import functools

import numpy as np
import jax
import jax.numpy as jnp
from jax import lax
from jax.experimental import pallas as pl
from jax.experimental.pallas import tpu as pltpu

F32 = jnp.float32
BF16 = jnp.bfloat16

GRID_W = 64
NH_A, DK_A, DV_A = 8, 64, 128
NH_B, DH_B = 16, 64
CHUNK = 128
GATE_CAP = 15.0
WIN_R, WIN_C = 8, 16
ROPE_THETA = 10000.0
EPS = 1e-6
N_MOD = 6
LANES = 128
MXU_TILE = 256
VMEM_LIMIT = 56 * 1024 * 1024

QROWS = 4
KROWS = QROWS + WIN_R - 1

N_PAIR = NH_A // 2
MLSTM_CPS = 8
C_EXT = 2 * DV_A

G_A, G_WL, G_B, G_TOT = 0, 32, 72, 104


def _cparams(sem):
    return pltpu.CompilerParams(dimension_semantics=sem, vmem_limit_bytes=VMEM_LIMIT)


def _mod_kernel(c_ref, w_ref, b_ref, o_ref):
    c = c_ref[...]
    h = c * jax.nn.sigmoid(c)
    o_ref[...] = jnp.dot(h, w_ref[...], preferred_element_type=F32,
                         precision=lax.Precision.HIGHEST) + b_ref[...]


def _modulation(cc, w_mod, b_mod, layer):
    rows, d = cc.shape
    n = w_mod.shape[2]
    tn = 1024
    return pl.pallas_call(
        _mod_kernel,
        grid=(n // tn,),
        in_specs=[pl.BlockSpec((rows, d), lambda j: (0, 0)),
                  pl.BlockSpec((None, d, tn), lambda j: (layer, 0, j)),
                  pl.BlockSpec((1, tn), lambda j: (0, j))],
        out_specs=pl.BlockSpec((rows, tn), lambda j: (0, j)),
        out_shape=jax.ShapeDtypeStruct((rows, n), F32),
        compiler_params=_cparams(("arbitrary",)),
        name="modulation",
    )(cc, w_mod, b_mod.reshape(1, n))


def _mod_norm(x, gain, scale, shift):
    ms = jnp.mean(x * x, axis=-1, keepdims=True)
    y = x * lax.rsqrt(ms + EPS) * gain
    return y * (1.0 + scale) + shift


def _gate_pack(g, gc_ref, gr_ref):
    tm = g.shape[0]
    g = GATE_CAP * jnp.tanh(g * (1.0 / GATE_CAP))
    lane_t = lax.broadcasted_iota(jnp.int32, g.shape, 1)
    is_forget = (lane_t & NH_A) != 0
    x = jnp.where(is_forget, jax.nn.log_sigmoid(g), g)
    x = jnp.where(lane_t < 4 * NH_A, x, 0.0)

    t_idx = lax.broadcasted_iota(jnp.int32, (CHUNK, CHUNK), 0)
    lane = lax.broadcasted_iota(jnp.int32, (CHUNK, CHUNK), 1)
    fwd_half = lane < 2 * NH_A
    low = lane < 4 * NH_A
    for ci in range(tm // CHUNK):
        xc = x[ci * CHUNK:(ci + 1) * CHUNK]
        cf = xc
        step = 1
        while step < CHUNK:
            cf = cf + jnp.where(t_idx >= step, pltpu.roll(cf, step, 0), 0.0)
            step *= 2
        tot = jnp.broadcast_to(cf[CHUNK - 1:CHUNK, :], xc.shape)
        cb = tot - cf + xc
        r_b = jnp.where(fwd_half, pltpu.roll(cf, LANES - NH_A, 1), pltpu.roll(cb, LANES - NH_A, 1))
        a = xc - r_b
        wl = xc + (pltpu.roll(tot, LANES - NH_A, 1) - r_b)
        bsel = jnp.where(fwd_half, cf, cb)
        z = lambda v: jnp.where(low, v, 0.0)
        packed = z(a) + pltpu.roll(z(wl), 32, 1) + pltpu.roll(z(bsel), 64, 1) + pltpu.roll(z(tot), 96, 1)
        gc_ref[ci * CHUNK:(ci + 1) * CHUNK, :] = packed
        gr_ref[:, ci * CHUNK:(ci + 1) * CHUNK] = packed.T


PROJ_TN = 512
BLK_QA, BLK_KA, BLK_VA, BLK_OA, BLK_QB, BLK_KB, BLK_VB, BLK_MG = 0, 1, 2, 4, 6, 8, 10, 12
N_PROJ_BLK = 16
CTX_BLOCKS = (BLK_KA, BLK_VA, BLK_VA + 1, BLK_KB, BLK_KB + 1, BLK_VB, BLK_VB + 1)
LOG2E = 1.4426950408889634


def _swap16(x):
    n = x.shape[-1]
    lane = lax.broadcasted_iota(jnp.int32, x.shape, x.ndim - 1)
    first = (lane & 31) < 16
    return jnp.where(first, pltpu.roll(x, n - 16, x.ndim - 1), pltpu.roll(x, 16, x.ndim - 1))


def _mixer_in_kernel(x_ref, mod_ref, g_ref, w_ref, wg_ref, bg_ref, qk_ref, cos_ref, sin_ref, gsum_ref,
                     o_ref, kt_ref, gc_ref, gr_ref, *, rope, blocks):
    tm = x_ref.shape[0]
    hx = _mod_norm(x_ref[...], g_ref[...], mod_ref[1:2, :], mod_ref[0:1, :]).astype(BF16)
    _gate_pack(jnp.dot(hx, wg_ref[...], preferred_element_type=F32) + bg_ref[...], gc_ref, gr_ref)
    if rope:
        reps = PROJ_TN // LANES
        cos = jnp.concatenate([cos_ref[...]] * reps, axis=1)
        sin = jnp.concatenate([sin_ref[...]] * reps, axis=1)
    for blk in range(N_PROJ_BLK):
        cols = slice(blk * PROJ_TN, (blk + 1) * PROJ_TN)
        if blk not in blocks:
            o_ref[:, cols] = jnp.zeros((tm, PROJ_TN), o_ref.dtype)
            continue
        acc = jnp.dot(hx, w_ref[:, cols], preferred_element_type=F32)
        if blk < BLK_VA:
            if rope:
                acc = acc * cos + _swap16(acc) * sin
            if blk == BLK_QA:
                acc = acc * DK_A ** -0.5
        elif BLK_QB <= blk < BLK_VB:
            sq = (acc * acc).astype(BF16)
            gw = gsum_ref.shape[0]
            ss = jnp.concatenate([jnp.dot(sq[:, c0:c0 + gw], gsum_ref[...], preferred_element_type=F32)
                                  for c0 in range(0, PROJ_TN, gw)], axis=1)
            gain = qk_ref[0:1, :] if blk < BLK_KB else qk_ref[1:2, :]
            acc = acc * lax.rsqrt(ss * (1.0 / DH_B) + EPS) * gain
        if blk == BLK_KA:
            kt_ref[...] = acc.T.astype(kt_ref.dtype)
        o_ref[:, cols] = acc.astype(o_ref.dtype)


def _mixer_in(x, mod3, g, w, w_g, b_g, qk_gain, cos, sin, gsum, *, rope, blocks, ctx_row=None, tm=512):
    bn, L, d = x.shape
    tm = min(tm, L)
    nblk = L // tm
    n = w.shape[1]
    mod_map = (lambda b, i: (b, 0, 0)) if ctx_row is None else (lambda b, i: (ctx_row, 0, 0))
    const = lambda b, i: (0, 0)
    tok = lambda b, i: (b * nblk + i, 0)
    resident = pl.Buffered(1)
    return pl.pallas_call(
        functools.partial(_mixer_in_kernel, rope=rope, blocks=blocks),
        grid=(bn, nblk),
        in_specs=[pl.BlockSpec((None, tm, d), lambda b, i: (b, i, 0)),
                  pl.BlockSpec((None, N_MOD, d), mod_map),
                  pl.BlockSpec((1, d), const),
                  pl.BlockSpec((d, n), const, pipeline_mode=resident),
                  pl.BlockSpec((d, LANES), const, pipeline_mode=resident),
                  pl.BlockSpec((1, LANES), const),
                  pl.BlockSpec((2, PROJ_TN), const),
                  pl.BlockSpec((tm, LANES), lambda b, i: (i, 0)),
                  pl.BlockSpec((tm, LANES), lambda b, i: (i, 0)),
                  pl.BlockSpec(gsum.shape, const, pipeline_mode=resident)],
        out_specs=[pl.BlockSpec((tm, n), tok),
                   pl.BlockSpec((PROJ_TN, tm), lambda b, i: (0, b * nblk + i)),
                   pl.BlockSpec((tm, LANES), tok),
                   pl.BlockSpec((LANES, tm), lambda b, i: (0, b * nblk + i))],
        out_shape=[jax.ShapeDtypeStruct((bn * L, n), BF16),
                   jax.ShapeDtypeStruct((PROJ_TN, bn * L), BF16),
                   jax.ShapeDtypeStruct((bn * L, LANES), F32),
                   jax.ShapeDtypeStruct((LANES, bn * L), F32)],
        compiler_params=_cparams(("parallel", "parallel")),
        name="mixer_in",
    )(x, mod3, g.reshape(1, d), w, w_g, b_g, qk_gain, cos, sin, gsum)


def _mlstm_state_kernel(kf_ref, vf_ref, gf_ref, kb_ref, vb_ref, gb_ref, c0_ref,
                        cf_ref, cb_ref, cfin_ref, st, *, nsteps, cps):
    s = pl.program_id(1)

    @pl.when(s == 0)
    def _():
        st[...] = c0_ref[...]

    lc = CHUNK
    ones_blk = jnp.ones((lc, DV_A), BF16)
    for c in range(cps):
        for d, (k_ref, v_ref, g_ref, out_ref) in enumerate(((kf_ref, vf_ref, gf_ref, cf_ref),
                                                            (kb_ref, vb_ref, gb_ref, cb_ref))):
            cc = c if d == 0 else cps - 1 - c
            tok = slice(cc * lc, (cc + 1) * lc)
            g = g_ref[:, tok]
            for h in range(NH_A):
                p, hh = divmod(h, 2)
                rows = slice(hh * DK_A, (hh + 1) * DK_A)
                wl = G_WL + 16 * d + h
                tl = G_TOT + 16 * d + h
                kw = (k_ref[h * DK_A:(h + 1) * DK_A, tok].astype(F32) * jnp.exp(g[wl:wl + 1, :])).astype(BF16)
                vext = jnp.concatenate([v_ref[tok, h * DV_A:(h + 1) * DV_A], ones_blk], axis=1)
                upd = jnp.dot(kw, vext, preferred_element_type=F32)
                c_old = st[d, p, rows, :]
                out_ref[cc, p, hh] = c_old.astype(out_ref.dtype)
                st[d, p, rows, :] = jnp.exp(g[tl:tl + 1, 0:1]) * c_old + upd

    @pl.when(s == nsteps - 1)
    def _():
        cfin_ref[...] = st[...]


def _mlstm_states(proj, kt, gr, c0, *, bn, seq_len):
    nc = seq_len // CHUNK
    cps = min(MLSTM_CPS, nc)
    nsteps = nc // cps
    lb = cps * CHUNK
    vcol = BLK_VA * PROJ_TN // (NH_A * DV_A)
    fwd = lambda b, s: b * nsteps + s
    bwd = lambda b, s: b * nsteps + nsteps - 1 - s
    st_shape = (2, N_PAIR, 2 * DK_A, C_EXT)
    out_blk = (None, cps, N_PAIR, 2, DK_A, C_EXT)
    return pl.pallas_call(
        functools.partial(_mlstm_state_kernel, nsteps=nsteps, cps=cps),
        grid=(bn, nsteps),
        in_specs=[pl.BlockSpec((NH_A * DK_A, lb), lambda b, s: (0, fwd(b, s))),
                  pl.BlockSpec((lb, NH_A * DV_A), lambda b, s: (fwd(b, s), vcol)),
                  pl.BlockSpec((LANES, lb), lambda b, s: (0, fwd(b, s))),
                  pl.BlockSpec((NH_A * DK_A, lb), lambda b, s: (0, bwd(b, s))),
                  pl.BlockSpec((lb, NH_A * DV_A), lambda b, s: (bwd(b, s), vcol)),
                  pl.BlockSpec((LANES, lb), lambda b, s: (0, bwd(b, s))),
                  pl.BlockSpec((None,) + st_shape, lambda b, s: (b, 0, 0, 0, 0))],
        out_specs=[pl.BlockSpec(out_blk, lambda b, s: (b, s, 0, 0, 0, 0)),
                   pl.BlockSpec(out_blk, lambda b, s: (b, nsteps - 1 - s, 0, 0, 0, 0)),
                   pl.BlockSpec((None,) + st_shape, lambda b, s: (b, 0, 0, 0, 0))],
        out_shape=[jax.ShapeDtypeStruct((bn, nc, N_PAIR, 2, DK_A, C_EXT), BF16),
                   jax.ShapeDtypeStruct((bn, nc, N_PAIR, 2, DK_A, C_EXT), BF16),
                   jax.ShapeDtypeStruct((bn,) + st_shape, F32)],
        scratch_shapes=[pltpu.VMEM(st_shape, F32)],
        compiler_params=_cparams(("parallel", "arbitrary")),
        name="mlstm_states",
    )(kt, proj, gr, kt, proj, gr, c0)


def _mlstm_out_kernel(q_ref, kt_ref, v_ref, gc_ref, gr_ref, cf_ref, cb_ref, h_ref):
    lc = CHUNK
    t_idx = lax.broadcasted_iota(jnp.int32, (lc, lc), 0)
    s_idx = lax.broadcasted_iota(jnp.int32, (lc, lc), 1)
    visible = (s_idx <= t_idx, s_idx >= t_idx)
    lo = lax.broadcasted_iota(jnp.int32, (lc, 2 * DK_A), 1) < DK_A
    ones_blk = jnp.ones((lc, DV_A), BF16)
    zero_c = jnp.zeros((DK_A, C_EXT), BF16)
    for c in range(q_ref.shape[0] // lc):
        tok = slice(c * lc, (c + 1) * lc)
        gc = gc_ref[tok, :]
        gr = gr_ref[:, tok]
        for p in range(N_PAIR):
            qp = q_ref[tok, p * 2 * DK_A:(p + 1) * 2 * DK_A]
            ktp = kt_ref[p * 2 * DK_A:(p + 1) * 2 * DK_A, tok]
            zq = jnp.zeros_like(qp)
            q_stack = jnp.concatenate([jnp.where(lo, qp, zq), jnp.where(lo, zq, qp)], axis=0)
            s_both = jnp.dot(q_stack, ktp, preferred_element_type=F32)
            b_rep = [[jnp.broadcast_to(gc[:, G_B + 16 * d + 2 * p + hh:G_B + 16 * d + 2 * p + hh + 1], (lc, lc))
                      for hh in range(2)] for d in range(2)]
            inter = []
            for d, c_ref in enumerate((cf_ref, cb_ref)):
                e = jnp.where(lo, jnp.exp(b_rep[d][0]), jnp.exp(b_rep[d][1]))
                qs = (qp.astype(F32) * e).astype(BF16)
                c_pair = jnp.concatenate([jnp.concatenate([c_ref[c, p, 0], zero_c], axis=1),
                                          jnp.concatenate([zero_c, c_ref[c, p, 1]], axis=1)], axis=0)
                inter.append(jnp.dot(qs, c_pair, preferred_element_type=F32))
            for hh in range(2):
                h = 2 * p + hh
                s_h = s_both[hh * lc:(hh + 1) * lc]
                vext = jnp.concatenate([v_ref[tok, h * DV_A:(h + 1) * DV_A], ones_blk], axis=1)
                acc = None
                for d in range(2):
                    a_row = gr[G_A + 16 * d + h:G_A + 16 * d + h + 1, :]
                    log_d = jnp.where(visible[d], b_rep[d][hh] + a_row, -jnp.inf)
                    pm = (s_h * jnp.exp(log_d)).astype(BF16)
                    nd = (jnp.dot(pm, vext, preferred_element_type=F32)
                          + inter[d][:, hh * C_EXT:(hh + 1) * C_EXT])
                    hd = nd[:, :DV_A] / jnp.maximum(jnp.abs(nd[:, DV_A:]), 1.0)
                    acc = hd if acc is None else acc + hd
                h_ref[tok, h * DV_A:(h + 1) * DV_A] = acc.astype(h_ref.dtype)


def _mlstm_outputs(proj, kt, gc, gr, cf, cb, *, bn, seq_len):
    cps = min(MLSTM_CPS, seq_len // CHUNK)
    lc = cps * CHUNK
    nc = seq_len // lc
    qcol = BLK_QA * PROJ_TN // (NH_A * DK_A)
    vcol = BLK_VA * PROJ_TN // (NH_A * DV_A)
    tok = lambda b, c: b * nc + c
    st_blk = (None, cps, N_PAIR, 2, DK_A, C_EXT)
    st_map = lambda b, c: (b, c, 0, 0, 0, 0)
    return pl.pallas_call(
        _mlstm_out_kernel,
        grid=(bn, nc),
        in_specs=[pl.BlockSpec((lc, NH_A * DK_A), lambda b, c: (tok(b, c), qcol)),
                  pl.BlockSpec((NH_A * DK_A, lc), lambda b, c: (0, tok(b, c))),
                  pl.BlockSpec((lc, NH_A * DV_A), lambda b, c: (tok(b, c), vcol)),
                  pl.BlockSpec((lc, LANES), lambda b, c: (tok(b, c), 0)),
                  pl.BlockSpec((LANES, lc), lambda b, c: (0, tok(b, c))),
                  pl.BlockSpec(st_blk, st_map),
                  pl.BlockSpec(st_blk, st_map)],
        out_specs=pl.BlockSpec((lc, NH_A * DV_A), lambda b, c: (tok(b, c), 0)),
        out_shape=jax.ShapeDtypeStruct((bn * seq_len, NH_A * DV_A), F32),
        compiler_params=_cparams(("parallel", "parallel")),
        name="mlstm_outputs",
    )(proj, kt, proj, gc, gr, cf, cb)


NAT_QB = 16
NAT_SAFE_RANGE = 100.0


def _natten_kernel(shift_ref, q_ref, k_ref, v_ref, kc_ref, vc_ref, ba_ref, bm_ref, bb_ref, o_ref, *,
                   seq_len, bounded):
    i2 = pl.program_id(2)
    nq = QROWS * GRID_W
    nk = KROWS * GRID_W
    pw = 2 * DH_B
    shift = shift_ref[0:1, 0:1]
    spare = (DH_B, 0)
    nt = (((1,), (1,)), ((), ()))

    def with_ones_lane(v, h):
        lane = lax.broadcasted_iota(jnp.int32, v.shape, 1)
        own = (lane < DH_B) if h == 0 else (lane >= DH_B)
        return jnp.where(own, v, (lane == spare[h]).astype(v.dtype))

    kc = kc_ref[...]
    vc_pair = [with_ones_lane(vc_ref[...], h) for h in range(2)]
    lo_o = lax.broadcasted_iota(jnp.int32, (nq, pw), 1) < DH_B
    zq = jnp.zeros((nq, pw), BF16)
    qb = q_ref.shape[0] // nq
    bias_refs = (ba_ref,) + (bm_ref,) * (qb - 2) + (bb_ref,)
    for j, bias_ref in enumerate(bias_refs):
        blk = qb * i2 + j
        start = jnp.clip(blk * nq - (WIN_R // 2) * GRID_W, 0, seq_len - nk)
        start = pl.multiple_of(start, GRID_W)
        q = q_ref[j * nq:(j + 1) * nq, :]
        kb = k_ref[pl.ds(start, nk), :]
        vb = v_ref[pl.ds(start, nk), :]
        outs = []
        for h in range(2):
            qh = jnp.where(lo_o if h == 0 else ~lo_o, q, zq)
            s_win = lax.dot_general(qh, kb, nt, preferred_element_type=F32) + bias_ref[h]
            s_ctx = lax.dot_general(qh, kc, nt, preferred_element_type=F32)
            if bounded:
                m = shift
            else:
                m = jnp.maximum(jnp.max(s_win, axis=1, keepdims=True), jnp.max(s_ctx, axis=1, keepdims=True))
                s_win = s_win - m
            o = (jnp.dot(jnp.exp2(s_win).astype(BF16), with_ones_lane(vb, h), preferred_element_type=F32)
                 + jnp.dot(jnp.exp2(s_ctx - m).astype(BF16), vc_pair[h], preferred_element_type=F32))
            outs.append(o / o[:, spare[h]:spare[h] + 1])
        o_ref[j * nq:(j + 1) * nq, :] = jnp.where(lo_o, outs[0], outs[1]).astype(o_ref.dtype)


def _natten_patterns(rows):
    nblk = rows // QROWS
    pats = []
    for blk in (0, 1 if nblk > 2 else 0, nblk - 1):
        r0 = blk * QROWS
        k0 = int(np.clip(r0 - WIN_R // 2, 0, rows - KROWS))
        qr = r0 + np.arange(QROWS)
        kr = k0 + np.arange(KROWS)
        wr0 = np.clip(qr - WIN_R // 2, 0, rows - WIN_R)
        valid = (kr[None, :] >= wr0[:, None]) & (kr[None, :] < wr0[:, None] + WIN_R)
        dr = np.clip(kr[None, :] - qr[:, None] + WIN_R - 1, 0, 2 * WIN_R - 2)
        pats.append((valid, dr))
    return pats


def _bias_kernel(t_ref, o_ref, *, patterns):
    pat = pl.program_id(0)
    for ps, (valid, dr) in enumerate(patterns):
        @pl.when(pat == ps)
        def _():
            for xq in range(QROWS):
                for yk in range(KROWS):
                    if valid[xq, yk]:
                        blk = t_ref[int(dr[xq, yk])]
                    else:
                        blk = jnp.full((GRID_W, GRID_W), -jnp.inf, F32)
                    o_ref[xq * GRID_W:(xq + 1) * GRID_W, yk * GRID_W:(yk + 1) * GRID_W] = blk


def _natten_shift(rpb, qn_g, kn_g):
    qmax = jnp.max(jnp.abs(qn_g)) * (DH_B ** -0.5 * LOG2E) * DH_B ** 0.5
    kmax = jnp.max(jnp.abs(kn_g)) * DH_B ** 0.5
    bound = 1.02 * qmax * kmax
    bias_hi = jnp.maximum(jnp.max(rpb) * LOG2E, 0.0)
    bias_lo = jnp.minimum(jnp.min(rpb) * LOG2E, 0.0)
    shift = bound + bias_hi
    spread = shift + bound - bias_lo
    bounded = spread <= NAT_SAFE_RANGE
    return jnp.where(bounded, shift, 0.0).astype(F32), bounded


def _natten_bias(rpb, rows, shift):
    nh = rpb.shape[0]
    qc = np.arange(GRID_W)
    kc = np.arange(GRID_W)
    wc0 = np.clip(qc - WIN_C // 2, 0, GRID_W - WIN_C)
    in_c = (kc[None, :] >= wc0[:, None]) & (kc[None, :] < wc0[:, None] + WIN_C)
    dc = np.clip(kc[None, :] - qc[:, None] + WIN_C - 1, 0, 2 * WIN_C - 2)
    sel_c = jnp.asarray(dc[:, :, None] == np.arange(2 * WIN_C - 1), F32)
    tcol = jnp.einsum('hab,uvb->hauv', rpb.astype(F32) * LOG2E, sel_c, precision=lax.Precision.HIGHEST)
    tcol = jnp.where(in_c[None, None], tcol - shift, -jnp.inf)
    nq, nk = QROWS * GRID_W, KROWS * GRID_W
    n_dr = 2 * WIN_R - 1
    return pl.pallas_call(
        functools.partial(_bias_kernel, patterns=_natten_patterns(rows)),
        grid=(3, nh),
        in_specs=[pl.BlockSpec((None, n_dr, GRID_W, GRID_W), lambda p, h: (h, 0, 0, 0))],
        out_specs=pl.BlockSpec((None, None, nq, nk), lambda p, h: (p, h, 0, 0)),
        out_shape=jax.ShapeDtypeStruct((3, nh, nq, nk), F32),
        compiler_params=_cparams(("arbitrary", "arbitrary")),
        name="natten_bias",
    )(tcol)


def _natten(px, pc, bias, shift, *, bn, seq_len, lctx, bounded):
    nq = QROWS * GRID_W
    nk = KROWS * GRID_W
    qb = min(NAT_QB, seq_len // nq)
    nstep = seq_len // (qb * nq)
    pw = 2 * DH_B
    n_pair = NH_B // 2
    qcol, kcol, vcol = (blk * PROJ_TN // pw for blk in (BLK_QB, BLK_KB, BLK_VB))
    last = nstep - 1
    first_pat = lambda i: jnp.where(i == 0, 0, 1)
    second_pat = lambda i: jnp.where(i == last, 2, 1)
    return pl.pallas_call(
        functools.partial(_natten_kernel, seq_len=seq_len, bounded=bounded),
        grid=(bn, n_pair, nstep),
        in_specs=[pl.BlockSpec((1, pw), lambda b, p, i: (0, 0)),
                  pl.BlockSpec((qb * nq, pw), lambda b, p, i: (b * nstep + i, qcol + p)),
                  pl.BlockSpec((seq_len, pw), lambda b, p, i: (b, kcol + p)),
                  pl.BlockSpec((seq_len, pw), lambda b, p, i: (b, vcol + p)),
                  pl.BlockSpec((lctx, pw), lambda b, p, i: (b, kcol + p)),
                  pl.BlockSpec((lctx, pw), lambda b, p, i: (b, vcol + p)),
                  pl.BlockSpec((None, 2, nq, nk), lambda b, p, i: (first_pat(i), p, 0, 0)),
                  pl.BlockSpec((None, 2, nq, nk), lambda b, p, i: (1, p, 0, 0)),
                  pl.BlockSpec((None, 2, nq, nk), lambda b, p, i: (second_pat(i), p, 0, 0))],
        out_specs=pl.BlockSpec((qb * nq, pw), lambda b, p, i: (b * nstep + i, p)),
        out_shape=jax.ShapeDtypeStruct((bn * seq_len, NH_B * DH_B), BF16),
        compiler_params=_cparams(("parallel", "parallel", "arbitrary")),
        name="natten_bounded" if bounded else "natten",
    )(jnp.full((1, pw), shift, F32), px, px, px, pc, pc, bias, bias, bias)


def _merge_kernel(h_ref, oa_ref, na_ref, mg_ref, x_ref, mod_ref, gn_ref, wa_ref, wb_ref, wo_ref, o_ref):
    h = h_ref[...]
    parts = []
    for hd in range(NH_A):
        hh = h[:, hd * DV_A:(hd + 1) * DV_A]
        ms = jnp.mean(hh * hh, axis=-1, keepdims=True)
        parts.append(hh * lax.rsqrt(ms + EPS))
    ha = jnp.concatenate(parts, axis=1) * gn_ref[...]
    ha = (ha * jax.nn.sigmoid(oa_ref[...].astype(F32))).astype(BF16)
    d = wa_ref.shape[1]
    gates = jax.nn.sigmoid(mg_ref[...].astype(F32))
    t = (gates[:, :d] * jnp.dot(ha, wa_ref[...], preferred_element_type=F32)
         + gates[:, d:] * jnp.dot(na_ref[...], wb_ref[...], preferred_element_type=F32))
    mix = jnp.dot(t.astype(BF16), wo_ref[...], preferred_element_type=F32)
    g1 = mod_ref[2:3, :]
    o_ref[...] = x_ref[...] + g1 * mix


def _merge(h, proj, na, x, mod3, gn, wa, wb, wo, tm=512):
    bn, S, d = x.shape
    tm = min(tm, S)
    nblk = S // tm
    d_a = h.shape[-1]
    d_b = na.shape[-1]
    oa_blk = BLK_OA * PROJ_TN // d_a
    mg_blk = BLK_MG * PROJ_TN // (2 * d)
    const = lambda b, i: (0, 0)
    return pl.pallas_call(
        _merge_kernel,
        grid=(bn, nblk),
        in_specs=[pl.BlockSpec((tm, d_a), lambda b, i: (b * nblk + i, 0)),
                  pl.BlockSpec((tm, d_a), lambda b, i: (b * nblk + i, oa_blk)),
                  pl.BlockSpec((tm, d_b), lambda b, i: (b * nblk + i, 0)),
                  pl.BlockSpec((tm, 2 * d), lambda b, i: (b * nblk + i, mg_blk)),
                  pl.BlockSpec((None, tm, d), lambda b, i: (b, i, 0)),
                  pl.BlockSpec((None, N_MOD, d), lambda b, i: (b, 0, 0)),
                  pl.BlockSpec((1, d_a), const),
                  pl.BlockSpec((d_a, d), const, pipeline_mode=pl.Buffered(1)),
                  pl.BlockSpec((d_b, d), const, pipeline_mode=pl.Buffered(1)),
                  pl.BlockSpec((d, d), const, pipeline_mode=pl.Buffered(1))],
        out_specs=pl.BlockSpec((None, tm, d), lambda b, i: (b, i, 0)),
        out_shape=jax.ShapeDtypeStruct((bn, S, d), F32),
        compiler_params=_cparams(("parallel", "parallel")),
        name="merge",
    )(h, proj, na, proj, x, mod3, gn, wa, wb, wo)


def _ffn_kernel(x_ref, mod_ref, g_ref, wg_ref, wu_ref, wd_ref, o_ref):
    x = x_ref[...]
    hx = _mod_norm(x, g_ref[...], mod_ref[4:5, :], mod_ref[3:4, :]).astype(BF16)
    a = jnp.dot(hx, wg_ref[...], preferred_element_type=F32)
    u = jnp.dot(hx, wu_ref[...], preferred_element_type=F32)
    act = (a * jax.nn.sigmoid(a) * u).astype(BF16)
    f = jnp.dot(act, wd_ref[...], preferred_element_type=F32)
    o_ref[...] = x + mod_ref[5:6, :] * f


def _ffn(x, mod3, g, wg, wu, wd, tm=512):
    bn, S, d = x.shape
    tm = min(tm, S)
    dff = wg.shape[1]
    const = lambda b, i: (0, 0)
    return pl.pallas_call(
        _ffn_kernel,
        grid=(bn, S // tm),
        in_specs=[pl.BlockSpec((None, tm, d), lambda b, i: (b, i, 0)),
                  pl.BlockSpec((None, N_MOD, d), lambda b, i: (b, 0, 0)),
                  pl.BlockSpec((1, d), const),
                  pl.BlockSpec((d, dff), const, pipeline_mode=pl.Buffered(1)),
                  pl.BlockSpec((d, dff), const, pipeline_mode=pl.Buffered(1)),
                  pl.BlockSpec((dff, d), const, pipeline_mode=pl.Buffered(1))],
        out_specs=pl.BlockSpec((None, tm, d), lambda b, i: (b, i, 0)),
        out_shape=jax.ShapeDtypeStruct((bn, S, d), F32),
        compiler_params=_cparams(("parallel", "parallel")),
        name="ffn",
    )(x, mod3, g.reshape(1, d), wg, wu, wd)


def _rope_tables(S):
    t = np.arange(S)
    row, col = t // GRID_W, t % GRID_W
    half = DK_A // 4
    inv = ROPE_THETA ** (-np.arange(half, dtype=np.float64) / half)
    ang_r = row[:, None] * inv[None, :]
    ang_c = col[:, None] * inv[None, :]
    cos = np.concatenate([np.cos(ang_r)] * 2 + [np.cos(ang_c)] * 2, axis=1)
    sin = np.concatenate([-np.sin(ang_r), np.sin(ang_r), -np.sin(ang_c), np.sin(ang_c)], axis=1)
    reps = LANES // DK_A
    return (jnp.asarray(np.tile(cos, (1, reps)), F32), jnp.asarray(np.tile(sin, (1, reps)), F32))


def kernel(x, c, ctx, c_ctx, w_mod, b_mod, norm1_g, w_in, b_gates, mlstm_norm_g, qn_g, kn_g, rpb,
           w_branch_a, w_branch_b, w_out, norm2_g, w_ffn_gate, w_ffn_up, w_ffn_down):
    bn, S, d = x.shape
    lctx = ctx.shape[1]
    rows = S // GRID_W
    depth = w_mod.shape[0]
    d_a, d_b = NH_A * DV_A, NH_B * DH_B
    dqk = NH_A * DK_A
    sizes = (dqk, dqk, d_a, d_a, 4 * NH_A, d_b, d_b, d_b, 2 * d)
    offs = np.cumsum((0,) + sizes)
    cos_t, sin_t = _rope_tables(S)
    gsum = jnp.asarray(np.kron(np.eye(MXU_TILE // DH_B), np.ones((DH_B, DH_B))), BF16)

    for l in range(depth):
        assert l == depth - 1, "context-stream update for non-final layers is not implemented"
        n_rows = -(-(bn + 1) // 8) * 8
        cc = jnp.concatenate([c, c_ctx[None, :], jnp.zeros((n_rows - bn - 1, d), F32)], axis=0)
        mod3 = _modulation(cc, w_mod, b_mod[l], l).reshape(n_rows, N_MOD, d)

        w = w_in[l].astype(BF16)
        g0, g1 = int(offs[4]), int(offs[5])
        w_main = jnp.concatenate([w[:, :g0], w[:, g1:]], axis=1)
        w_g = jnp.pad(w[:, g0:g1], ((0, 0), (0, LANES - 4 * NH_A)))
        b_g = jnp.pad(b_gates[l].reshape(1, 4 * NH_A), ((0, 0), (0, LANES - 4 * NH_A)))
        qk_gain = jnp.stack([jnp.tile(qn_g[l], PROJ_TN // DH_B) * (DH_B ** -0.5 * LOG2E),
                             jnp.tile(kn_g[l], PROJ_TN // DH_B)])

        px, ktx, gcx, grx = _mixer_in(x, mod3, norm1_g[l], w_main, w_g, b_g, qk_gain, cos_t, sin_t, gsum,
                                      rope=True, blocks=tuple(range(N_PROJ_BLK)))
        pc, ktc, _, grc = _mixer_in(ctx, mod3, norm1_g[l], w_main, w_g, b_g, qk_gain, cos_t, sin_t, gsum,
                                    rope=False, blocks=CTX_BLOCKS, ctx_row=bn)

        zero_state = jnp.zeros((bn, 2, N_PAIR, 2 * DK_A, C_EXT), F32)
        _, _, st_ctx = _mlstm_states(pc, ktc, grc, zero_state, bn=bn, seq_len=lctx)
        cf, cb, _ = _mlstm_states(px, ktx, grx, st_ctx, bn=bn, seq_len=S)
        h_a = _mlstm_outputs(px, ktx, gcx, grx, cf, cb, bn=bn, seq_len=S)

        shift, bounded = _natten_shift(rpb[l], qn_g[l], kn_g[l])
        bias = _natten_bias(rpb[l], rows, shift)
        na = lax.cond(bounded,
                      functools.partial(_natten, bn=bn, seq_len=S, lctx=lctx, bounded=True),
                      functools.partial(_natten, bn=bn, seq_len=S, lctx=lctx, bounded=False),
                      px, pc, bias, shift)

        x_mid = _merge(h_a, px, na, x, mod3, mlstm_norm_g[l].reshape(1, d_a),
                       w_branch_a[l].astype(BF16), w_branch_b[l].astype(BF16), w_out[l].astype(BF16))
        x = _ffn(x_mid, mod3, norm2_g[l], w_ffn_gate[l].astype(BF16), w_ffn_up[l].astype(BF16),
                 w_ffn_down[l].astype(BF16))
    return x
```

```python
import functools

import numpy as np
import jax
import jax.numpy as jnp
from jax import lax
from jax.experimental import pallas as pl
from jax.experimental.pallas import tpu as pltpu

F32 = jnp.float32
BF16 = jnp.bfloat16

GRID_W = 64
NH_A, DK_A, DV_A = 8, 64, 128
NH_B, DH_B = 16, 64
CHUNK = 128
GATE_CAP = 15.0
WIN_R, WIN_C = 8, 16
ROPE_THETA = 10000.0
EPS = 1e-6
N_MOD = 6
LANES = 128
MXU_TILE = 256
VMEM_LIMIT = 56 * 1024 * 1024

QROWS = 4
KROWS = QROWS + WIN_R - 1

N_PAIR = NH_A // 2
MLSTM_CPS = 8
C_EXT = 2 * DV_A

G_A, G_WL, G_B, G_TOT = 0, 32, 72, 104


def _cparams(sem):
    return pltpu.CompilerParams(dimension_semantics=sem, vmem_limit_bytes=VMEM_LIMIT)


def _mod_kernel(c_ref, w_ref, b_ref, o_ref):
    c = c_ref[...]
    h = c * jax.nn.sigmoid(c)
    o_ref[...] = jnp.dot(h, w_ref[...], preferred_element_type=F32,
                         precision=lax.Precision.HIGHEST) + b_ref[...]


def _modulation(cc, w_mod, b_mod):
    rows, d = cc.shape
    n = w_mod.shape[1]
    tn = 1024
    return pl.pallas_call(
        _mod_kernel,
        grid=(n // tn,),
        in_specs=[pl.BlockSpec((rows, d), lambda j: (0, 0)),
                  pl.BlockSpec((d, tn), lambda j: (0, j)),
                  pl.BlockSpec((1, tn), lambda j: (0, j))],
        out_specs=pl.BlockSpec((rows, tn), lambda j: (0, j)),
        out_shape=jax.ShapeDtypeStruct((rows, n), F32),
        compiler_params=_cparams(("arbitrary",)),
        name="modulation",
    )(cc, w_mod, b_mod.reshape(1, n))


def _mod_norm(x, gain, scale, shift):
    ms = jnp.mean(x * x, axis=-1, keepdims=True)
    y = x * lax.rsqrt(ms + EPS) * gain
    return y * (1.0 + scale) + shift


def _gate_pack(g, gc_ref, gr_ref):
    tm = g.shape[0]
    g = GATE_CAP * jnp.tanh(g * (1.0 / GATE_CAP))
    lane_t = lax.broadcasted_iota(jnp.int32, g.shape, 1)
    is_forget = (lane_t & NH_A) != 0
    x = jnp.where(is_forget, jax.nn.log_sigmoid(g), g)
    x = jnp.where(lane_t < 4 * NH_A, x, 0.0)

    t_idx = lax.broadcasted_iota(jnp.int32, (CHUNK, CHUNK), 0)
    lane = lax.broadcasted_iota(jnp.int32, (CHUNK, CHUNK), 1)
    fwd_half = lane < 2 * NH_A
    low = lane < 4 * NH_A
    for ci in range(tm // CHUNK):
        xc = x[ci * CHUNK:(ci + 1) * CHUNK]
        cf = xc
        step = 1
        while step < CHUNK:
            cf = cf + jnp.where(t_idx >= step, pltpu.roll(cf, step, 0), 0.0)
            step *= 2
        tot = jnp.broadcast_to(cf[CHUNK - 1:CHUNK, :], xc.shape)
        cb = tot - cf + xc
        r_b = jnp.where(fwd_half, pltpu.roll(cf, LANES - NH_A, 1), pltpu.roll(cb, LANES - NH_A, 1))
        a = xc - r_b
        wl = xc + (pltpu.roll(tot, LANES - NH_A, 1) - r_b)
        bsel = jnp.where(fwd_half, cf, cb)
        z = lambda v: jnp.where(low, v, 0.0)
        packed = z(a) + pltpu.roll(z(wl), 32, 1) + pltpu.roll(z(bsel), 64, 1) + pltpu.roll(z(tot), 96, 1)
        gc_ref[ci * CHUNK:(ci + 1) * CHUNK, :] = packed
        gr_ref[:, ci * CHUNK:(ci + 1) * CHUNK] = packed.T


PROJ_TN = 512
BLK_QA, BLK_KA, BLK_VA, BLK_OA, BLK_QB, BLK_KB, BLK_VB, BLK_MG = 0, 1, 2, 4, 6, 8, 10, 12
N_PROJ_BLK = 16
EPILOGUE_LAG = 1
CTX_BLOCKS = (BLK_KA, BLK_VA, BLK_VA + 1, BLK_KB, BLK_KB + 1, BLK_VB, BLK_VB + 1)
LOG2E = 1.4426950408889634


def _swap16(x):
    n = x.shape[-1]
    lane = lax.broadcasted_iota(jnp.int32, x.shape, x.ndim - 1)
    first = (lane & 31) < 16
    return jnp.where(first, pltpu.roll(x, n - 16, x.ndim - 1), pltpu.roll(x, 16, x.ndim - 1))


def _mixer_in_kernel(x_ref, mod_ref, g_ref, w_ref, wg_ref, bg_ref, qk_ref, cos_ref, sin_ref, gsum_ref,
                     o_ref, kt_ref, gc_ref, gr_ref, *, rope, blocks):
    tm = x_ref.shape[0]
    hx = _mod_norm(x_ref[...], g_ref[...], mod_ref[1:2, :], mod_ref[0:1, :]).astype(BF16)
    _gate_pack(jnp.dot(hx, wg_ref[...], preferred_element_type=F32) + bg_ref[...], gc_ref, gr_ref)
    if rope:
        reps = PROJ_TN // LANES
        cos = jnp.concatenate([cos_ref[...]] * reps, axis=1)
        sin = jnp.concatenate([sin_ref[...]] * reps, axis=1)
    def finish(blk, acc):
        cols = slice(blk * PROJ_TN, (blk + 1) * PROJ_TN)
        if blk < BLK_VA:
            if rope:
                acc = acc * cos + _swap16(acc) * sin
            if blk == BLK_QA:
                acc = acc * DK_A ** -0.5
        elif BLK_QB <= blk < BLK_VB:
            sq = (acc * acc).astype(BF16)
            gw = gsum_ref.shape[0]
            ss = jnp.concatenate([jnp.dot(sq[:, c0:c0 + gw], gsum_ref[...], preferred_element_type=F32)
                                  for c0 in range(0, PROJ_TN, gw)], axis=1)
            gain = qk_ref[0:1, :] if blk < BLK_KB else qk_ref[1:2, :]
            acc = acc * lax.rsqrt(ss * (1.0 / DH_B) + EPS) * gain
        if blk == BLK_KA:
            kt_ref[...] = acc.T.astype(kt_ref.dtype)
        o_ref[:, cols] = acc.astype(o_ref.dtype)

    pending = []
    for blk in range(N_PROJ_BLK):
        cols = slice(blk * PROJ_TN, (blk + 1) * PROJ_TN)
        if blk not in blocks:
            o_ref[:, cols] = jnp.zeros((tm, PROJ_TN), o_ref.dtype)
            continue
        pending.append((blk, jnp.dot(hx, w_ref[:, cols], preferred_element_type=F32)))
        if len(pending) > EPILOGUE_LAG:
            finish(*pending.pop(0))
    for item in pending:
        finish(*item)


def _mixer_in(x, mod3, g, w, w_g, b_g, qk_gain, cos, sin, gsum, *, rope, blocks, ctx_row=None, tm=512):
    bn, L, d = x.shape
    tm = min(tm, L)
    nblk = L // tm
    n = w.shape[1]
    mod_map = (lambda b, i: (b, 0, 0)) if ctx_row is None else (lambda b, i: (ctx_row, 0, 0))
    const = lambda b, i: (0, 0)
    tok = lambda b, i: (b * nblk + i, 0)
    resident = pl.Buffered(1)
    return pl.pallas_call(
        functools.partial(_mixer_in_kernel, rope=rope, blocks=blocks),
        grid=(bn, nblk),
        in_specs=[pl.BlockSpec((None, tm, d), lambda b, i: (b, i, 0)),
                  pl.BlockSpec((None, N_MOD, d), mod_map),
                  pl.BlockSpec((1, d), const),
                  pl.BlockSpec((d, n), const, pipeline_mode=resident),
                  pl.BlockSpec((d, LANES), const, pipeline_mode=resident),
                  pl.BlockSpec((1, LANES), const),
                  pl.BlockSpec((2, PROJ_TN), const),
                  pl.BlockSpec((tm, LANES), lambda b, i: (i, 0)),
                  pl.BlockSpec((tm, LANES), lambda b, i: (i, 0)),
                  pl.BlockSpec(gsum.shape, const, pipeline_mode=resident)],
        out_specs=[pl.BlockSpec((tm, n), tok),
                   pl.BlockSpec((PROJ_TN, tm), lambda b, i: (0, b * nblk + i)),
                   pl.BlockSpec((tm, LANES), tok),
                   pl.BlockSpec((LANES, tm), lambda b, i: (0, b * nblk + i))],
        out_shape=[jax.ShapeDtypeStruct((bn * L, n), BF16),
                   jax.ShapeDtypeStruct((PROJ_TN, bn * L), BF16),
                   jax.ShapeDtypeStruct((bn * L, LANES), F32),
                   jax.ShapeDtypeStruct((LANES, bn * L), F32)],
        compiler_params=_cparams(("parallel", "parallel")),
        name="mixer_in",
    )(x, mod3, g.reshape(1, d), w, w_g, b_g, qk_gain, cos, sin, gsum)


def _mlstm_state_kernel(kf_ref, vf_ref, gf_ref, kb_ref, vb_ref, gb_ref, c0_ref,
                        cf_ref, cb_ref, cfin_ref, st, *, nsteps, cps):
    s = pl.program_id(1)

    @pl.when(s == 0)
    def _():
        st[...] = c0_ref[...]

    lc = CHUNK
    ones_blk = jnp.ones((lc, DV_A), BF16)
    for c in range(cps):
        for d, (k_ref, v_ref, g_ref, out_ref) in enumerate(((kf_ref, vf_ref, gf_ref, cf_ref),
                                                            (kb_ref, vb_ref, gb_ref, cb_ref))):
            cc = c if d == 0 else cps - 1 - c
            tok = slice(cc * lc, (cc + 1) * lc)
            g = g_ref[:, tok]
            for h in range(NH_A):
                p, hh = divmod(h, 2)
                rows = slice(hh * DK_A, (hh + 1) * DK_A)
                wl = G_WL + 16 * d + h
                tl = G_TOT + 16 * d + h
                kw = (k_ref[h * DK_A:(h + 1) * DK_A, tok].astype(F32) * jnp.exp(g[wl:wl + 1, :])).astype(BF16)
                vext = jnp.concatenate([v_ref[tok, h * DV_A:(h + 1) * DV_A], ones_blk], axis=1)
                upd = jnp.dot(kw, vext, preferred_element_type=F32)
                c_old = st[d, p, rows, :]
                out_ref[cc, p, hh] = c_old.astype(out_ref.dtype)
                st[d, p, rows, :] = jnp.exp(g[tl:tl + 1, 0:1]) * c_old + upd

    @pl.when(s == nsteps - 1)
    def _():
        cfin_ref[...] = st[...]


def _mlstm_states(proj, kt, gr, c0, *, bn, seq_len):
    nc = seq_len // CHUNK
    cps = min(MLSTM_CPS, nc)
    nsteps = nc // cps
    lb = cps * CHUNK
    vcol = BLK_VA * PROJ_TN // (NH_A * DV_A)
    fwd = lambda b, s: b * nsteps + s
    bwd = lambda b, s: b * nsteps + nsteps - 1 - s
    st_shape = (2, N_PAIR, 2 * DK_A, C_EXT)
    out_blk = (None, cps, N_PAIR, 2, DK_A, C_EXT)
    return pl.pallas_call(
        functools.partial(_mlstm_state_kernel, nsteps=nsteps, cps=cps),
        grid=(bn, nsteps),
        in_specs=[pl.BlockSpec((NH_A * DK_A, lb), lambda b, s: (0, fwd(b, s))),
                  pl.BlockSpec((lb, NH_A * DV_A), lambda b, s: (fwd(b, s), vcol)),
                  pl.BlockSpec((LANES, lb), lambda b, s: (0, fwd(b, s))),
                  pl.BlockSpec((NH_A * DK_A, lb), lambda b, s: (0, bwd(b, s))),
                  pl.BlockSpec((lb, NH_A * DV_A), lambda b, s: (bwd(b, s), vcol)),
                  pl.BlockSpec((LANES, lb), lambda b, s: (0, bwd(b, s))),
                  pl.BlockSpec((None,) + st_shape, lambda b, s: (b, 0, 0, 0, 0))],
        out_specs=[pl.BlockSpec(out_blk, lambda b, s: (b, s, 0, 0, 0, 0)),
                   pl.BlockSpec(out_blk, lambda b, s: (b, nsteps - 1 - s, 0, 0, 0, 0)),
                   pl.BlockSpec((None,) + st_shape, lambda b, s: (b, 0, 0, 0, 0))],
        out_shape=[jax.ShapeDtypeStruct((bn, nc, N_PAIR, 2, DK_A, C_EXT), BF16),
                   jax.ShapeDtypeStruct((bn, nc, N_PAIR, 2, DK_A, C_EXT), BF16),
                   jax.ShapeDtypeStruct((bn,) + st_shape, F32)],
        scratch_shapes=[pltpu.VMEM(st_shape, F32)],
        compiler_params=_cparams(("parallel", "arbitrary")),
        name="mlstm_states",
    )(kt, proj, gr, kt, proj, gr, c0)


def _mlstm_out_kernel(q_ref, kt_ref, v_ref, gc_ref, gr_ref, cf_ref, cb_ref, h_ref):
    lc = CHUNK
    t_idx = lax.broadcasted_iota(jnp.int32, (lc, lc), 0)
    s_idx = lax.broadcasted_iota(jnp.int32, (lc, lc), 1)
    visible = (s_idx <= t_idx, s_idx >= t_idx)
    lo = lax.broadcasted_iota(jnp.int32, (lc, 2 * DK_A), 1) < DK_A
    ones_blk = jnp.ones((lc, DV_A), BF16)
    zero_c = jnp.zeros((DK_A, C_EXT), BF16)
    for c in range(q_ref.shape[0] // lc):
        tok = slice(c * lc, (c + 1) * lc)
        gc = gc_ref[tok, :]
        gr = gr_ref[:, tok]
        for p in range(N_PAIR):
            qp = q_ref[tok, p * 2 * DK_A:(p + 1) * 2 * DK_A]
            ktp = kt_ref[p * 2 * DK_A:(p + 1) * 2 * DK_A, tok]
            zq = jnp.zeros_like(qp)
            q_stack = jnp.concatenate([jnp.where(lo, qp, zq), jnp.where(lo, zq, qp)], axis=0)
            s_both = jnp.dot(q_stack, ktp, preferred_element_type=F32)
            b_rep = [[jnp.broadcast_to(gc[:, G_B + 16 * d + 2 * p + hh:G_B + 16 * d + 2 * p + hh + 1], (lc, lc))
                      for hh in range(2)] for d in range(2)]
            inter = []
            for d, c_ref in enumerate((cf_ref, cb_ref)):
                e = jnp.where(lo, jnp.exp(b_rep[d][0]), jnp.exp(b_rep[d][1]))
                qs = (qp.astype(F32) * e).astype(BF16)
                c_pair = jnp.concatenate([jnp.concatenate([c_ref[c, p, 0], zero_c], axis=1),
                                          jnp.concatenate([zero_c, c_ref[c, p, 1]], axis=1)], axis=0)
                inter.append(jnp.dot(qs, c_pair, preferred_element_type=F32))
            for hh in range(2):
                h = 2 * p + hh
                s_h = s_both[hh * lc:(hh + 1) * lc]
                vext = jnp.concatenate([v_ref[tok, h * DV_A:(h + 1) * DV_A], ones_blk], axis=1)
                acc = None
                for d in range(2):
                    a_row = gr[G_A + 16 * d + h:G_A + 16 * d + h + 1, :]
                    log_d = jnp.where(visible[d], b_rep[d][hh] + a_row, -jnp.inf)
                    pm = (s_h * jnp.exp(log_d)).astype(BF16)
                    nd = (jnp.dot(pm, vext, preferred_element_type=F32)
                          + inter[d][:, hh * C_EXT:(hh + 1) * C_EXT])
                    hd = nd[:, :DV_A] / jnp.maximum(jnp.abs(nd[:, DV_A:]), 1.0)
                    acc = hd if acc is None else acc + hd
                h_ref[tok, h * DV_A:(h + 1) * DV_A] = acc.astype(h_ref.dtype)


def _mlstm_outputs(proj, kt, gc, gr, cf, cb, *, bn, seq_len):
    cps = min(MLSTM_CPS, seq_len // CHUNK)
    lc = cps * CHUNK
    nc = seq_len // lc
    qcol = BLK_QA * PROJ_TN // (NH_A * DK_A)
    vcol = BLK_VA * PROJ_TN // (NH_A * DV_A)
    tok = lambda b, c: b * nc + c
    st_blk = (None, cps, N_PAIR, 2, DK_A, C_EXT)
    st_map = lambda b, c: (b, c, 0, 0, 0, 0)
    return pl.pallas_call(
        _mlstm_out_kernel,
        grid=(bn, nc),
        in_specs=[pl.BlockSpec((lc, NH_A * DK_A), lambda b, c: (tok(b, c), qcol)),
                  pl.BlockSpec((NH_A * DK_A, lc), lambda b, c: (0, tok(b, c))),
                  pl.BlockSpec((lc, NH_A * DV_A), lambda b, c: (tok(b, c), vcol)),
                  pl.BlockSpec((lc, LANES), lambda b, c: (tok(b, c), 0)),
                  pl.BlockSpec((LANES, lc), lambda b, c: (0, tok(b, c))),
                  pl.BlockSpec(st_blk, st_map),
                  pl.BlockSpec(st_blk, st_map)],
        out_specs=pl.BlockSpec((lc, NH_A * DV_A), lambda b, c: (tok(b, c), 0)),
        out_shape=jax.ShapeDtypeStruct((bn * seq_len, NH_A * DV_A), F32),
        compiler_params=_cparams(("parallel", "parallel")),
        name="mlstm_outputs",
    )(proj, kt, proj, gc, gr, cf, cb)


NAT_QB = 16
NAT_SAFE_RANGE = 100.0


def _natten_kernel(shift_ref, q_ref, k_ref, v_ref, kc_ref, vc_ref, ba_ref, bm_ref, bb_ref, o_ref, *,
                   seq_len, bounded):
    i2 = pl.program_id(2)
    nq = QROWS * GRID_W
    nk = KROWS * GRID_W
    pw = 2 * DH_B
    shift = shift_ref[0:1, 0:1]
    spare = (DH_B, 0)
    nt = (((1,), (1,)), ((), ()))

    def with_ones_lane(v, h):
        lane = lax.broadcasted_iota(jnp.int32, v.shape, 1)
        own = (lane < DH_B) if h == 0 else (lane >= DH_B)
        return jnp.where(own, v, (lane == spare[h]).astype(v.dtype))

    kc = kc_ref[...]
    vc_pair = [with_ones_lane(vc_ref[...], h) for h in range(2)]
    lo_o = lax.broadcasted_iota(jnp.int32, (nq, pw), 1) < DH_B
    zq = jnp.zeros((nq, pw), BF16)
    qb = q_ref.shape[0] // nq
    bias_refs = (ba_ref,) + (bm_ref,) * (qb - 2) + (bb_ref,)

    def scores(j, h):
        blk = qb * i2 + j
        start = jnp.clip(blk * nq - (WIN_R // 2) * GRID_W, 0, seq_len - nk)
        start = pl.multiple_of(start, GRID_W)
        q = q_ref[j * nq:(j + 1) * nq, :]
        qh = jnp.where(lo_o if h == 0 else ~lo_o, q, zq)
        s_win = lax.dot_general(qh, k_ref[pl.ds(start, nk), :], nt,
                                preferred_element_type=F32) + bias_refs[j][h]
        s_ctx = lax.dot_general(qh, kc, nt, preferred_element_type=F32)
        return start, s_win, s_ctx

    def attend(j, h, start, s_win, s_ctx):
        if bounded:
            m = shift
        else:
            m = jnp.maximum(jnp.max(s_win, axis=1, keepdims=True), jnp.max(s_ctx, axis=1, keepdims=True))
            s_win = s_win - m
        vb = v_ref[pl.ds(start, nk), :]
        o = (jnp.dot(jnp.exp2(s_win).astype(BF16), with_ones_lane(vb, h), preferred_element_type=F32)
             + jnp.dot(jnp.exp2(s_ctx - m).astype(BF16), vc_pair[h], preferred_element_type=F32))
        return o / o[:, spare[h]:spare[h] + 1]

    problems = [(j, h) for j in range(qb) for h in range(2)]
    outs = {}
    ahead = scores(*problems[0])
    for n, (j, h) in enumerate(problems):
        cur = ahead
        if n + 1 < len(problems):
            ahead = scores(*problems[n + 1])
        outs[h] = attend(j, h, *cur)
        if h == 1:
            o_ref[j * nq:(j + 1) * nq, :] = jnp.where(lo_o, outs[0], outs[1]).astype(o_ref.dtype)


def _natten_patterns(rows):
    nblk = rows // QROWS
    pats = []
    for blk in (0, 1 if nblk > 2 else 0, nblk - 1):
        r0 = blk * QROWS
        k0 = int(np.clip(r0 - WIN_R // 2, 0, rows - KROWS))
        qr = r0 + np.arange(QROWS)
        kr = k0 + np.arange(KROWS)
        wr0 = np.clip(qr - WIN_R // 2, 0, rows - WIN_R)
        valid = (kr[None, :] >= wr0[:, None]) & (kr[None, :] < wr0[:, None] + WIN_R)
        dr = np.clip(kr[None, :] - qr[:, None] + WIN_R - 1, 0, 2 * WIN_R - 2)
        pats.append((valid, dr))
    return pats


def _bias_kernel(t_ref, o_ref, *, patterns):
    pat = pl.program_id(0)
    for ps, (valid, dr) in enumerate(patterns):
        @pl.when(pat == ps)
        def _():
            for xq in range(QROWS):
                for yk in range(KROWS):
                    if valid[xq, yk]:
                        blk = t_ref[int(dr[xq, yk])]
                    else:
                        blk = jnp.full((GRID_W, GRID_W), -jnp.inf, F32)
                    o_ref[xq * GRID_W:(xq + 1) * GRID_W, yk * GRID_W:(yk + 1) * GRID_W] = blk


def _natten_shift(rpb, qn_g, kn_g):
    qmax = jnp.max(jnp.abs(qn_g)) * (DH_B ** -0.5 * LOG2E) * DH_B ** 0.5
    kmax = jnp.max(jnp.abs(kn_g)) * DH_B ** 0.5
    bound = 1.02 * qmax * kmax
    bias_hi = jnp.maximum(jnp.max(rpb) * LOG2E, 0.0)
    bias_lo = jnp.minimum(jnp.min(rpb) * LOG2E, 0.0)
    shift = bound + bias_hi
    spread = shift + bound - bias_lo
    bounded = spread <= NAT_SAFE_RANGE
    return jnp.where(bounded, shift, 0.0).astype(F32), bounded


def _natten_bias(rpb, rows, shift):
    nh = rpb.shape[0]
    qc = np.arange(GRID_W)
    kc = np.arange(GRID_W)
    wc0 = np.clip(qc - WIN_C // 2, 0, GRID_W - WIN_C)
    in_c = (kc[None, :] >= wc0[:, None]) & (kc[None, :] < wc0[:, None] + WIN_C)
    dc = np.clip(kc[None, :] - qc[:, None] + WIN_C - 1, 0, 2 * WIN_C - 2)
    sel_c = jnp.asarray(dc[:, :, None] == np.arange(2 * WIN_C - 1), F32)
    tcol = jnp.einsum('hab,uvb->hauv', rpb.astype(F32) * LOG2E, sel_c, precision=lax.Precision.HIGHEST)
    tcol = jnp.where(in_c[None, None], tcol - shift, -jnp.inf)
    nq, nk = QROWS * GRID_W, KROWS * GRID_W
    n_dr = 2 * WIN_R - 1
    return pl.pallas_call(
        functools.partial(_bias_kernel, patterns=_natten_patterns(rows)),
        grid=(3, nh),
        in_specs=[pl.BlockSpec((None, n_dr, GRID_W, GRID_W), lambda p, h: (h, 0, 0, 0))],
        out_specs=pl.BlockSpec((None, None, nq, nk), lambda p, h: (p, h, 0, 0)),
        out_shape=jax.ShapeDtypeStruct((3, nh, nq, nk), F32),
        compiler_params=_cparams(("arbitrary", "arbitrary")),
        name="natten_bias",
    )(tcol)


def _natten(px, pc, bias, shift, *, bn, seq_len, lctx, bounded):
    nq = QROWS * GRID_W
    nk = KROWS * GRID_W
    qb = min(NAT_QB, seq_len // nq)
    nstep = seq_len // (qb * nq)
    pw = 2 * DH_B
    n_pair = NH_B // 2
    qcol, kcol, vcol = (blk * PROJ_TN // pw for blk in (BLK_QB, BLK_KB, BLK_VB))
    last = nstep - 1
    first_pat = lambda i: jnp.where(i == 0, 0, 1)
    second_pat = lambda i: jnp.where(i == last, 2, 1)
    return pl.pallas_call(
        functools.partial(_natten_kernel, seq_len=seq_len, bounded=bounded),
        grid=(bn, n_pair, nstep),
        in_specs=[pl.BlockSpec((1, pw), lambda b, p, i: (0, 0)),
                  pl.BlockSpec((qb * nq, pw), lambda b, p, i: (b * nstep + i, qcol + p)),
                  pl.BlockSpec((seq_len, pw), lambda b, p, i: (b, kcol + p)),
                  pl.BlockSpec((seq_len, pw), lambda b, p, i: (b, vcol + p)),
                  pl.BlockSpec((lctx, pw), lambda b, p, i: (b, kcol + p)),
                  pl.BlockSpec((lctx, pw), lambda b, p, i: (b, vcol + p)),
                  pl.BlockSpec((None, 2, nq, nk), lambda b, p, i: (first_pat(i), p, 0, 0)),
                  pl.BlockSpec((None, 2, nq, nk), lambda b, p, i: (1, p, 0, 0)),
                  pl.BlockSpec((None, 2, nq, nk), lambda b, p, i: (second_pat(i), p, 0, 0))],
        out_specs=pl.BlockSpec((qb * nq, pw), lambda b, p, i: (b * nstep + i, p)),
        out_shape=jax.ShapeDtypeStruct((bn * seq_len, NH_B * DH_B), BF16),
        compiler_params=_cparams(("parallel", "parallel", "arbitrary")),
        name="natten_bounded" if bounded else "natten",
    )(jnp.full((1, pw), shift, F32), px, px, px, pc, pc, bias, bias, bias)


def _merge_kernel(h_ref, oa_ref, na_ref, mg_ref, x_ref, mod_ref, gn_ref, wa_ref, wb_ref, wo_ref, o_ref):
    h = h_ref[...]
    parts = []
    for hd in range(NH_A):
        hh = h[:, hd * DV_A:(hd + 1) * DV_A]
        ms = jnp.mean(hh * hh, axis=-1, keepdims=True)
        parts.append(hh * lax.rsqrt(ms + EPS))
    ha = jnp.concatenate(parts, axis=1) * gn_ref[...]
    ha = (ha * jax.nn.sigmoid(oa_ref[...].astype(F32))).astype(BF16)
    d = wa_ref.shape[1]
    gates = jax.nn.sigmoid(mg_ref[...].astype(F32))
    t = (gates[:, :d] * jnp.dot(ha, wa_ref[...], preferred_element_type=F32)
         + gates[:, d:] * jnp.dot(na_ref[...], wb_ref[...], preferred_element_type=F32))
    mix = jnp.dot(t.astype(BF16), wo_ref[...], preferred_element_type=F32)
    g1 = mod_ref[2:3, :]
    o_ref[...] = x_ref[...] + g1 * mix


def _merge(h, proj, na, x, mod3, gn, wa, wb, wo, tm=512):
    bn, S, d = x.shape
    tm = min(tm, S)
    nblk = S // tm
    d_a = h.shape[-1]
    d_b = na.shape[-1]
    oa_blk = BLK_OA * PROJ_TN // d_a
    mg_blk = BLK_MG * PROJ_TN // (2 * d)
    const = lambda b, i: (0, 0)
    return pl.pallas_call(
        _merge_kernel,
        grid=(bn, nblk),
        in_specs=[pl.BlockSpec((tm, d_a), lambda b, i: (b * nblk + i, 0)),
                  pl.BlockSpec((tm, d_a), lambda b, i: (b * nblk + i, oa_blk)),
                  pl.BlockSpec((tm, d_b), lambda b, i: (b * nblk + i, 0)),
                  pl.BlockSpec((tm, 2 * d), lambda b, i: (b * nblk + i, mg_blk)),
                  pl.BlockSpec((None, tm, d), lambda b, i: (b, i, 0)),
                  pl.BlockSpec((None, N_MOD, d), lambda b, i: (b, 0, 0)),
                  pl.BlockSpec((1, d_a), const),
                  pl.BlockSpec((d_a, d), const, pipeline_mode=pl.Buffered(1)),
                  pl.BlockSpec((d_b, d), const, pipeline_mode=pl.Buffered(1)),
                  pl.BlockSpec((d, d), const, pipeline_mode=pl.Buffered(1))],
        out_specs=pl.BlockSpec((None, tm, d), lambda b, i: (b, i, 0)),
        out_shape=jax.ShapeDtypeStruct((bn, S, d), F32),
        compiler_params=_cparams(("parallel", "parallel")),
        name="merge",
    )(h, proj, na, proj, x, mod3, gn, wa, wb, wo)


def _ffn_kernel(x_ref, mod_ref, g_ref, wg_ref, wu_ref, wd_ref, o_ref):
    x = x_ref[...]
    hx = _mod_norm(x, g_ref[...], mod_ref[4:5, :], mod_ref[3:4, :]).astype(BF16)
    a = jnp.dot(hx, wg_ref[...], preferred_element_type=F32)
    u = jnp.dot(hx, wu_ref[...], preferred_element_type=F32)
    act = (a * jax.nn.sigmoid(a) * u).astype(BF16)
    f = jnp.dot(act, wd_ref[...], preferred_element_type=F32)
    o_ref[...] = x + mod_ref[5:6, :] * f


def _ffn(x, mod3, g, wg, wu, wd, tm=512):
    bn, S, d = x.shape
    tm = min(tm, S)
    dff = wg.shape[1]
    const = lambda b, i: (0, 0)
    return pl.pallas_call(
        _ffn_kernel,
        grid=(bn, S // tm),
        in_specs=[pl.BlockSpec((None, tm, d), lambda b, i: (b, i, 0)),
                  pl.BlockSpec((None, N_MOD, d), lambda b, i: (b, 0, 0)),
                  pl.BlockSpec((1, d), const),
                  pl.BlockSpec((d, dff), const, pipeline_mode=pl.Buffered(1)),
                  pl.BlockSpec((d, dff), const, pipeline_mode=pl.Buffered(1)),
                  pl.BlockSpec((dff, d), const, pipeline_mode=pl.Buffered(1))],
        out_specs=pl.BlockSpec((None, tm, d), lambda b, i: (b, i, 0)),
        out_shape=jax.ShapeDtypeStruct((bn, S, d), F32),
        compiler_params=_cparams(("parallel", "parallel")),
        name="ffn",
    )(x, mod3, g.reshape(1, d), wg, wu, wd)


def _rope_tables(S):
    t = np.arange(S)
    row, col = t // GRID_W, t % GRID_W
    half = DK_A // 4
    inv = ROPE_THETA ** (-np.arange(half, dtype=np.float64) / half)
    ang_r = row[:, None] * inv[None, :]
    ang_c = col[:, None] * inv[None, :]
    cos = np.concatenate([np.cos(ang_r)] * 2 + [np.cos(ang_c)] * 2, axis=1)
    sin = np.concatenate([-np.sin(ang_r), np.sin(ang_r), -np.sin(ang_c), np.sin(ang_c)], axis=1)
    reps = LANES // DK_A
    return (jnp.asarray(np.tile(cos, (1, reps)), F32), jnp.asarray(np.tile(sin, (1, reps)), F32))


def kernel(x, c, ctx, c_ctx, w_mod, b_mod, norm1_g, w_in, b_gates, mlstm_norm_g, qn_g, kn_g, rpb,
           w_branch_a, w_branch_b, w_out, norm2_g, w_ffn_gate, w_ffn_up, w_ffn_down):
    bn, S, d = x.shape
    lctx = ctx.shape[1]
    rows = S // GRID_W
    depth = w_mod.shape[0]
    d_a, d_b = NH_A * DV_A, NH_B * DH_B
    dqk = NH_A * DK_A
    sizes = (dqk, dqk, d_a, d_a, 4 * NH_A, d_b, d_b, d_b, 2 * d)
    offs = np.cumsum((0,) + sizes)
    cos_t, sin_t = _rope_tables(S)
    gsum = jnp.asarray(np.kron(np.eye(MXU_TILE // DH_B), np.ones((DH_B, DH_B))), BF16)

    for l in range(depth):
        assert l == depth - 1, "context-stream update for non-final layers is not implemented"
        n_rows = -(-(bn + 1) // 8) * 8
        cc = jnp.concatenate([c, c_ctx[None, :], jnp.zeros((n_rows - bn - 1, d), F32)], axis=0)
        mod3 = _modulation(cc, w_mod[l], b_mod[l]).reshape(n_rows, N_MOD, d)

        w = w_in[l]
        seg = lambda i: w[:, int(offs[i]):int(offs[i + 1])]
        w_main = jnp.concatenate([seg(0), seg(1), seg(2), seg(3), seg(5), seg(6), seg(7), seg(8)],
                                 axis=1).astype(BF16)
        w_g = jnp.pad(seg(4), ((0, 0), (0, LANES - 4 * NH_A))).astype(BF16)
        b_g = jnp.pad(b_gates[l].reshape(1, 4 * NH_A), ((0, 0), (0, LANES - 4 * NH_A)))
        qk_gain = jnp.stack([jnp.tile(qn_g[l], PROJ_TN // DH_B) * (DH_B ** -0.5 * LOG2E),
                             jnp.tile(kn_g[l], PROJ_TN // DH_B)])

        px, ktx, gcx, grx = _mixer_in(x, mod3, norm1_g[l], w_main, w_g, b_g, qk_gain, cos_t, sin_t, gsum,
                                      rope=True, blocks=tuple(range(N_PROJ_BLK)))
        pc, ktc, _, grc = _mixer_in(ctx, mod3, norm1_g[l], w_main, w_g, b_g, qk_gain, cos_t, sin_t, gsum,
                                    rope=False, blocks=CTX_BLOCKS, ctx_row=bn)

        zero_state = jnp.zeros((bn, 2, N_PAIR, 2 * DK_A, C_EXT), F32)
        _, _, st_ctx = _mlstm_states(pc, ktc, grc, zero_state, bn=bn, seq_len=lctx)
        cf, cb, _ = _mlstm_states(px, ktx, grx, st_ctx, bn=bn, seq_len=S)
        h_a = _mlstm_outputs(px, ktx, gcx, grx, cf, cb, bn=bn, seq_len=S)

        shift, bounded = _natten_shift(rpb[l], qn_g[l], kn_g[l])
        bias = _natten_bias(rpb[l], rows, shift)
        na = lax.cond(bounded,
                      functools.partial(_natten, bn=bn, seq_len=S, lctx=lctx, bounded=True),
                      functools.partial(_natten, bn=bn, seq_len=S, lctx=lctx, bounded=False),
                      px, pc, bias, shift)

        x_mid = _merge(h_a, px, na, x, mod3, mlstm_norm_g[l].reshape(1, d_a),
                       w_branch_a[l].astype(BF16), w_branch_b[l].astype(BF16), w_out[l].astype(BF16))
        x = _ffn(x_mid, mod3, norm2_g[l], w_ffn_gate[l].astype(BF16), w_ffn_up[l].astype(BF16),
                 w_ffn_down[l].astype(BF16))
    return x
```

```python
import functools

import numpy as np
import jax
import jax.numpy as jnp
from jax import lax
from jax.experimental import pallas as pl
from jax.experimental.pallas import tpu as pltpu

F32 = jnp.float32
BF16 = jnp.bfloat16

GRID_W = 64
NH_A, DK_A, DV_A = 8, 64, 128
NH_B, DH_B = 16, 64
CHUNK = 128
GATE_CAP = 15.0
WIN_R, WIN_C = 8, 16
ROPE_THETA = 10000.0
EPS = 1e-6
N_MOD = 6
LANES = 128
MXU_TILE = 256
VMEM_LIMIT = 56 * 1024 * 1024

QROWS = 4
KROWS = QROWS + WIN_R - 1

N_PAIR = NH_A // 2
MLSTM_CPS = 8
C_EXT = 2 * DV_A

G_A, G_WL, G_B, G_TOT = 0, 32, 72, 104


def _cparams(sem):
    return pltpu.CompilerParams(dimension_semantics=sem, vmem_limit_bytes=VMEM_LIMIT)


def _mod_kernel(c_ref, w_ref, b_ref, o_ref):
    c = c_ref[...]
    h = c * jax.nn.sigmoid(c)
    o_ref[...] = jnp.dot(h, w_ref[...], preferred_element_type=F32,
                         precision=lax.Precision.HIGHEST) + b_ref[...]


def _modulation(cc, w_mod, b_mod):
    rows, d = cc.shape
    n = w_mod.shape[1]
    tn = 1024
    return pl.pallas_call(
        _mod_kernel,
        grid=(n // tn,),
        in_specs=[pl.BlockSpec((rows, d), lambda j: (0, 0)),
                  pl.BlockSpec((d, tn), lambda j: (0, j)),
                  pl.BlockSpec((1, tn), lambda j: (0, j))],
        out_specs=pl.BlockSpec((rows, tn), lambda j: (0, j)),
        out_shape=jax.ShapeDtypeStruct((rows, n), F32),
        compiler_params=_cparams(("arbitrary",)),
        name="modulation",
    )(cc, w_mod, b_mod.reshape(1, n))


def _mod_norm(x, gain, scale, shift):
    ms = jnp.mean(x * x, axis=-1, keepdims=True)
    y = x * lax.rsqrt(ms + EPS) * gain
    return y * (1.0 + scale) + shift


def _gate_pack(g, gc_ref, gr_ref):
    tm = g.shape[0]
    g = GATE_CAP * jnp.tanh(g * (1.0 / GATE_CAP))
    lane_t = lax.broadcasted_iota(jnp.int32, g.shape, 1)
    is_forget = (lane_t & NH_A) != 0
    x = jnp.where(is_forget, jax.nn.log_sigmoid(g), g)
    x = jnp.where(lane_t < 4 * NH_A, x, 0.0)

    t_idx = lax.broadcasted_iota(jnp.int32, (CHUNK, CHUNK), 0)
    lane = lax.broadcasted_iota(jnp.int32, (CHUNK, CHUNK), 1)
    fwd_half = lane < 2 * NH_A
    low = lane < 4 * NH_A
    for ci in range(tm // CHUNK):
        xc = x[ci * CHUNK:(ci + 1) * CHUNK]
        cf = xc
        step = 1
        while step < CHUNK:
            cf = cf + jnp.where(t_idx >= step, pltpu.roll(cf, step, 0), 0.0)
            step *= 2
        tot = jnp.broadcast_to(cf[CHUNK - 1:CHUNK, :], xc.shape)
        cb = tot - cf + xc
        r_b = jnp.where(fwd_half, pltpu.roll(cf, LANES - NH_A, 1), pltpu.roll(cb, LANES - NH_A, 1))
        a = xc - r_b
        wl = xc + (pltpu.roll(tot, LANES - NH_A, 1) - r_b)
        bsel = jnp.where(fwd_half, cf, cb)
        z = lambda v: jnp.where(low, v, 0.0)
        packed = z(a) + pltpu.roll(z(wl), 32, 1) + pltpu.roll(z(bsel), 64, 1) + pltpu.roll(z(tot), 96, 1)
        gc_ref[ci * CHUNK:(ci + 1) * CHUNK, :] = packed
        gr_ref[:, ci * CHUNK:(ci + 1) * CHUNK] = packed.T


PROJ_TN = 512
BLK_QA, BLK_KA, BLK_VA, BLK_OA, BLK_QB, BLK_KB, BLK_VB, BLK_MG = 0, 1, 2, 4, 6, 8, 10, 12
N_PROJ_BLK = 16
EPILOGUE_LAG = 1
CTX_BLOCKS = (BLK_KA, BLK_VA, BLK_VA + 1, BLK_KB, BLK_KB + 1, BLK_VB, BLK_VB + 1)
LOG2E = 1.4426950408889634


def _swap16(x):
    n = x.shape[-1]
    lane = lax.broadcasted_iota(jnp.int32, x.shape, x.ndim - 1)
    first = (lane & 31) < 16
    return jnp.where(first, pltpu.roll(x, n - 16, x.ndim - 1), pltpu.roll(x, 16, x.ndim - 1))


def _mixer_in_kernel(x_ref, mod_ref, g_ref, w_ref, wg_ref, bg_ref, qk_ref, cos_ref, sin_ref, gsum_ref,
                     o_ref, kt_ref, gc_ref, gr_ref, *, rope, blocks):
    tm = x_ref.shape[0]
    hx = _mod_norm(x_ref[...], g_ref[...], mod_ref[1:2, :], mod_ref[0:1, :]).astype(BF16)
    _gate_pack(jnp.dot(hx, wg_ref[...], preferred_element_type=F32) + bg_ref[...], gc_ref, gr_ref)
    if rope:
        reps = PROJ_TN // LANES
        cos = jnp.concatenate([cos_ref[...]] * reps, axis=1)
        sin = jnp.concatenate([sin_ref[...]] * reps, axis=1)
    def finish(blk, acc):
        cols = slice(blk * PROJ_TN, (blk + 1) * PROJ_TN)
        if blk < BLK_VA:
            if rope:
                acc = acc * cos + _swap16(acc) * sin
            if blk == BLK_QA:
                acc = acc * DK_A ** -0.5
        elif BLK_QB <= blk < BLK_VB:
            sq = (acc * acc).astype(BF16)
            gw = gsum_ref.shape[0]
            ss = jnp.concatenate([jnp.dot(sq[:, c0:c0 + gw], gsum_ref[...], preferred_element_type=F32)
                                  for c0 in range(0, PROJ_TN, gw)], axis=1)
            gain = qk_ref[0:1, :] if blk < BLK_KB else qk_ref[1:2, :]
            acc = acc * lax.rsqrt(ss * (1.0 / DH_B) + EPS) * gain
        if blk == BLK_KA:
            kt_ref[...] = acc.T.astype(kt_ref.dtype)
        o_ref[:, cols] = acc.astype(o_ref.dtype)

    pending = []
    for blk in range(N_PROJ_BLK):
        cols = slice(blk * PROJ_TN, (blk + 1) * PROJ_TN)
        if blk not in blocks:
            o_ref[:, cols] = jnp.zeros((tm, PROJ_TN), o_ref.dtype)
            continue
        pending.append((blk, jnp.dot(hx, w_ref[:, cols], preferred_element_type=F32)))
        if len(pending) > EPILOGUE_LAG:
            finish(*pending.pop(0))
    for item in pending:
        finish(*item)


def _mixer_in(x, mod3, g, w, w_g, b_g, qk_gain, cos, sin, gsum, *, rope, blocks, ctx_row=None, tm=512):
    bn, L, d = x.shape
    tm = min(tm, L)
    nblk = L // tm
    n = w.shape[1]
    mod_map = (lambda b, i: (b, 0, 0)) if ctx_row is None else (lambda b, i: (ctx_row, 0, 0))
    const = lambda b, i: (0, 0)
    tok = lambda b, i: (b * nblk + i, 0)
    resident = pl.Buffered(1)
    return pl.pallas_call(
        functools.partial(_mixer_in_kernel, rope=rope, blocks=blocks),
        grid=(bn, nblk),
        in_specs=[pl.BlockSpec((None, tm, d), lambda b, i: (b, i, 0)),
                  pl.BlockSpec((None, N_MOD, d), mod_map),
                  pl.BlockSpec((1, d), const),
                  pl.BlockSpec((d, n), const, pipeline_mode=resident),
                  pl.BlockSpec((d, LANES), const, pipeline_mode=resident),
                  pl.BlockSpec((1, LANES), const),
                  pl.BlockSpec((2, PROJ_TN), const),
                  pl.BlockSpec((tm, LANES), lambda b, i: (i, 0)),
                  pl.BlockSpec((tm, LANES), lambda b, i: (i, 0)),
                  pl.BlockSpec(gsum.shape, const, pipeline_mode=resident)],
        out_specs=[pl.BlockSpec((tm, n), tok),
                   pl.BlockSpec((PROJ_TN, tm), lambda b, i: (0, b * nblk + i)),
                   pl.BlockSpec((tm, LANES), tok),
                   pl.BlockSpec((LANES, tm), lambda b, i: (0, b * nblk + i))],
        out_shape=[jax.ShapeDtypeStruct((bn * L, n), BF16),
                   jax.ShapeDtypeStruct((PROJ_TN, bn * L), BF16),
                   jax.ShapeDtypeStruct((bn * L, LANES), F32),
                   jax.ShapeDtypeStruct((LANES, bn * L), F32)],
        compiler_params=_cparams(("parallel", "parallel")),
        name="mixer_in",
    )(x, mod3, g.reshape(1, d), w, w_g, b_g, qk_gain, cos, sin, gsum)


def _mlstm_state_kernel(kf_ref, vf_ref, gf_ref, kb_ref, vb_ref, gb_ref, c0_ref,
                        cf_ref, cb_ref, cfin_ref, st, *, nsteps, cps):
    s = pl.program_id(1)

    @pl.when(s == 0)
    def _():
        st[...] = c0_ref[...]

    lc = CHUNK
    ones_blk = jnp.ones((lc, DV_A), BF16)
    for c in range(cps):
        for d, (k_ref, v_ref, g_ref, out_ref) in enumerate(((kf_ref, vf_ref, gf_ref, cf_ref),
                                                            (kb_ref, vb_ref, gb_ref, cb_ref))):
            cc = c if d == 0 else cps - 1 - c
            tok = slice(cc * lc, (cc + 1) * lc)
            g = g_ref[:, tok]
            for h in range(NH_A):
                p, hh = divmod(h, 2)
                rows = slice(hh * DK_A, (hh + 1) * DK_A)
                wl = G_WL + 16 * d + h
                tl = G_TOT + 16 * d + h
                kw = (k_ref[h * DK_A:(h + 1) * DK_A, tok].astype(F32) * jnp.exp(g[wl:wl + 1, :])).astype(BF16)
                vext = jnp.concatenate([v_ref[tok, h * DV_A:(h + 1) * DV_A], ones_blk], axis=1)
                upd = jnp.dot(kw, vext, preferred_element_type=F32)
                c_old = st[d, p, rows, :]
                out_ref[cc, p, hh] = c_old.astype(out_ref.dtype)
                st[d, p, rows, :] = jnp.exp(g[tl:tl + 1, 0:1]) * c_old + upd

    @pl.when(s == nsteps - 1)
    def _():
        cfin_ref[...] = st[...]


def _mlstm_states(proj, kt, gr, c0, *, bn, seq_len):
    nc = seq_len // CHUNK
    cps = min(MLSTM_CPS, nc)
    nsteps = nc // cps
    lb = cps * CHUNK
    vcol = BLK_VA * PROJ_TN // (NH_A * DV_A)
    fwd = lambda b, s: b * nsteps + s
    bwd = lambda b, s: b * nsteps + nsteps - 1 - s
    st_shape = (2, N_PAIR, 2 * DK_A, C_EXT)
    out_blk = (None, cps, N_PAIR, 2, DK_A, C_EXT)
    return pl.pallas_call(
        functools.partial(_mlstm_state_kernel, nsteps=nsteps, cps=cps),
        grid=(bn, nsteps),
        in_specs=[pl.BlockSpec((NH_A * DK_A, lb), lambda b, s: (0, fwd(b, s))),
                  pl.BlockSpec((lb, NH_A * DV_A), lambda b, s: (fwd(b, s), vcol)),
                  pl.BlockSpec((LANES, lb), lambda b, s: (0, fwd(b, s))),
                  pl.BlockSpec((NH_A * DK_A, lb), lambda b, s: (0, bwd(b, s))),
                  pl.BlockSpec((lb, NH_A * DV_A), lambda b, s: (bwd(b, s), vcol)),
                  pl.BlockSpec((LANES, lb), lambda b, s: (0, bwd(b, s))),
                  pl.BlockSpec((None,) + st_shape, lambda b, s: (b, 0, 0, 0, 0))],
        out_specs=[pl.BlockSpec(out_blk, lambda b, s: (b, s, 0, 0, 0, 0)),
                   pl.BlockSpec(out_blk, lambda b, s: (b, nsteps - 1 - s, 0, 0, 0, 0)),
                   pl.BlockSpec((None,) + st_shape, lambda b, s: (b, 0, 0, 0, 0))],
        out_shape=[jax.ShapeDtypeStruct((bn, nc, N_PAIR, 2, DK_A, C_EXT), BF16),
                   jax.ShapeDtypeStruct((bn, nc, N_PAIR, 2, DK_A, C_EXT), BF16),
                   jax.ShapeDtypeStruct((bn,) + st_shape, F32)],
        scratch_shapes=[pltpu.VMEM(st_shape, F32)],
        compiler_params=_cparams(("parallel", "arbitrary")),
        name="mlstm_states",
    )(kt, proj, gr, kt, proj, gr, c0)


def _mlstm_out_kernel(q_ref, kt_ref, v_ref, gc_ref, gr_ref, cf_ref, cb_ref, h_ref):
    lc = CHUNK
    t_idx = lax.broadcasted_iota(jnp.int32, (lc, lc), 0)
    s_idx = lax.broadcasted_iota(jnp.int32, (lc, lc), 1)
    visible = (s_idx <= t_idx, s_idx >= t_idx)
    lo = lax.broadcasted_iota(jnp.int32, (lc, 2 * DK_A), 1) < DK_A
    ones_blk = jnp.ones((lc, DV_A), BF16)
    zero_c = jnp.zeros((DK_A, C_EXT), BF16)
    for c in range(q_ref.shape[0] // lc):
        tok = slice(c * lc, (c + 1) * lc)
        gc = gc_ref[tok, :]
        gr = gr_ref[:, tok]
        for p in range(N_PAIR):
            qp = q_ref[tok, p * 2 * DK_A:(p + 1) * 2 * DK_A]
            ktp = kt_ref[p * 2 * DK_A:(p + 1) * 2 * DK_A, tok]
            zq = jnp.zeros_like(qp)
            q_stack = jnp.concatenate([jnp.where(lo, qp, zq), jnp.where(lo, zq, qp)], axis=0)
            s_both = jnp.dot(q_stack, ktp, preferred_element_type=F32)
            b_rep = [[jnp.broadcast_to(gc[:, G_B + 16 * d + 2 * p + hh:G_B + 16 * d + 2 * p + hh + 1], (lc, lc))
                      for hh in range(2)] for d in range(2)]
            inter = []
            for d, c_ref in enumerate((cf_ref, cb_ref)):
                e = jnp.where(lo, jnp.exp(b_rep[d][0]), jnp.exp(b_rep[d][1]))
                qs = (qp.astype(F32) * e).astype(BF16)
                c_pair = jnp.concatenate([jnp.concatenate([c_ref[c, p, 0], zero_c], axis=1),
                                          jnp.concatenate([zero_c, c_ref[c, p, 1]], axis=1)], axis=0)
                inter.append(jnp.dot(qs, c_pair, preferred_element_type=F32))
            for hh in range(2):
                h = 2 * p + hh
                s_h = s_both[hh * lc:(hh + 1) * lc]
                vext = jnp.concatenate([v_ref[tok, h * DV_A:(h + 1) * DV_A], ones_blk], axis=1)
                acc = None
                for d in range(2):
                    a_row = gr[G_A + 16 * d + h:G_A + 16 * d + h + 1, :]
                    log_d = jnp.where(visible[d], b_rep[d][hh] + a_row, -jnp.inf)
                    pm = (s_h * jnp.exp(log_d)).astype(BF16)
                    nd = (jnp.dot(pm, vext, preferred_element_type=F32)
                          + inter[d][:, hh * C_EXT:(hh + 1) * C_EXT])
                    hd = nd[:, :DV_A] / jnp.maximum(jnp.abs(nd[:, DV_A:]), 1.0)
                    acc = hd if acc is None else acc + hd
                h_ref[tok, h * DV_A:(h + 1) * DV_A] = acc.astype(h_ref.dtype)


def _mlstm_outputs(proj, kt, gc, gr, cf, cb, *, bn, seq_len):
    cps = min(MLSTM_CPS, seq_len // CHUNK)
    lc = cps * CHUNK
    nc = seq_len // lc
    qcol = BLK_QA * PROJ_TN // (NH_A * DK_A)
    vcol = BLK_VA * PROJ_TN // (NH_A * DV_A)
    tok = lambda b, c: b * nc + c
    st_blk = (None, cps, N_PAIR, 2, DK_A, C_EXT)
    st_map = lambda b, c: (b, c, 0, 0, 0, 0)
    return pl.pallas_call(
        _mlstm_out_kernel,
        grid=(bn, nc),
        in_specs=[pl.BlockSpec((lc, NH_A * DK_A), lambda b, c: (tok(b, c), qcol)),
                  pl.BlockSpec((NH_A * DK_A, lc), lambda b, c: (0, tok(b, c))),
                  pl.BlockSpec((lc, NH_A * DV_A), lambda b, c: (tok(b, c), vcol)),
                  pl.BlockSpec((lc, LANES), lambda b, c: (tok(b, c), 0)),
                  pl.BlockSpec((LANES, lc), lambda b, c: (0, tok(b, c))),
                  pl.BlockSpec(st_blk, st_map),
                  pl.BlockSpec(st_blk, st_map)],
        out_specs=pl.BlockSpec((lc, NH_A * DV_A), lambda b, c: (tok(b, c), 0)),
        out_shape=jax.ShapeDtypeStruct((bn * seq_len, NH_A * DV_A), F32),
        compiler_params=_cparams(("parallel", "parallel")),
        name="mlstm_outputs",
    )(proj, kt, proj, gc, gr, cf, cb)


NAT_QB = 16
NAT_SAFE_RANGE = 100.0


def _natten_kernel(shift_ref, q_ref, k_ref, v_ref, kc_ref, vc_ref, ba_ref, bm_ref, bb_ref, o_ref, *,
                   seq_len, bounded):
    i2 = pl.program_id(2)
    nq = QROWS * GRID_W
    nk = KROWS * GRID_W
    pw = 2 * DH_B
    shift = shift_ref[0:1, 0:1]
    spare = (DH_B, 0)
    nt = (((1,), (1,)), ((), ()))

    def with_ones_lane(v, h):
        lane = lax.broadcasted_iota(jnp.int32, v.shape, 1)
        own = (lane < DH_B) if h == 0 else (lane >= DH_B)
        return jnp.where(own, v, (lane == spare[h]).astype(v.dtype))

    kc = kc_ref[...]
    vc_pair = [with_ones_lane(vc_ref[...], h) for h in range(2)]
    lo_o = lax.broadcasted_iota(jnp.int32, (nq, pw), 1) < DH_B
    zq = jnp.zeros((nq, pw), BF16)
    qb = q_ref.shape[0] // nq
    bias_refs = (ba_ref,) + (bm_ref,) * (qb - 2) + (bb_ref,)

    def scores(j, h):
        blk = qb * i2 + j
        start = jnp.clip(blk * nq - (WIN_R // 2) * GRID_W, 0, seq_len - nk)
        start = pl.multiple_of(start, GRID_W)
        q = q_ref[j * nq:(j + 1) * nq, :]
        qh = jnp.where(lo_o if h == 0 else ~lo_o, q, zq)
        s_win = lax.dot_general(qh, k_ref[pl.ds(start, nk), :], nt,
                                preferred_element_type=F32) + bias_refs[j][h]
        s_ctx = lax.dot_general(qh, kc, nt, preferred_element_type=F32)
        return start, s_win, s_ctx

    def attend(j, h, start, s_win, s_ctx):
        if bounded:
            m = shift
        else:
            m = jnp.maximum(jnp.max(s_win, axis=1, keepdims=True), jnp.max(s_ctx, axis=1, keepdims=True))
            s_win = s_win - m
        vb = v_ref[pl.ds(start, nk), :]
        o = (jnp.dot(jnp.exp2(s_win).astype(BF16), with_ones_lane(vb, h), preferred_element_type=F32)
             + jnp.dot(jnp.exp2(s_ctx - m).astype(BF16), vc_pair[h], preferred_element_type=F32))
        return o / o[:, spare[h]:spare[h] + 1]

    problems = [(j, h) for j in range(qb) for h in range(2)]
    outs = {}
    ahead = scores(*problems[0])
    for n, (j, h) in enumerate(problems):
        cur = ahead
        if n + 1 < len(problems):
            ahead = scores(*problems[n + 1])
        outs[h] = attend(j, h, *cur)
        if h == 1:
            o_ref[j * nq:(j + 1) * nq, :] = jnp.where(lo_o, outs[0], outs[1]).astype(o_ref.dtype)


def _natten_patterns(rows):
    nblk = rows // QROWS
    pats = []
    for blk in (0, 1 if nblk > 2 else 0, nblk - 1):
        r0 = blk * QROWS
        k0 = int(np.clip(r0 - WIN_R // 2, 0, rows - KROWS))
        qr = r0 + np.arange(QROWS)
        kr = k0 + np.arange(KROWS)
        wr0 = np.clip(qr - WIN_R // 2, 0, rows - WIN_R)
        valid = (kr[None, :] >= wr0[:, None]) & (kr[None, :] < wr0[:, None] + WIN_R)
        dr = np.clip(kr[None, :] - qr[:, None] + WIN_R - 1, 0, 2 * WIN_R - 2)
        pats.append((valid, dr))
    return pats


def _bias_kernel(t_ref, o_ref, *, patterns):
    pat = pl.program_id(0)
    for ps, (valid, dr) in enumerate(patterns):
        @pl.when(pat == ps)
        def _():
            for xq in range(QROWS):
                for yk in range(KROWS):
                    if valid[xq, yk]:
                        blk = t_ref[int(dr[xq, yk])]
                    else:
                        blk = jnp.full((GRID_W, GRID_W), -jnp.inf, F32)
                    o_ref[xq * GRID_W:(xq + 1) * GRID_W, yk * GRID_W:(yk + 1) * GRID_W] = blk


def _natten_shift(rpb, qn_g, kn_g):
    qmax = jnp.max(jnp.abs(qn_g)) * (DH_B ** -0.5 * LOG2E) * DH_B ** 0.5
    kmax = jnp.max(jnp.abs(kn_g)) * DH_B ** 0.5
    bound = 1.02 * qmax * kmax
    bias_hi = jnp.maximum(jnp.max(rpb) * LOG2E, 0.0)
    bias_lo = jnp.minimum(jnp.min(rpb) * LOG2E, 0.0)
    shift = bound + bias_hi
    spread = shift + bound - bias_lo
    bounded = spread <= NAT_SAFE_RANGE
    return jnp.where(bounded, shift, 0.0).astype(F32), bounded


def _natten_bias(rpb, rows, shift):
    nh = rpb.shape[0]
    qc = np.arange(GRID_W)
    kc = np.arange(GRID_W)
    wc0 = np.clip(qc - WIN_C // 2, 0, GRID_W - WIN_C)
    in_c = (kc[None, :] >= wc0[:, None]) & (kc[None, :] < wc0[:, None] + WIN_C)
    dc = np.clip(kc[None, :] - qc[:, None] + WIN_C - 1, 0, 2 * WIN_C - 2)
    sel_c = jnp.asarray(dc[:, :, None] == np.arange(2 * WIN_C - 1), F32)
    tcol = jnp.einsum('hab,uvb->hauv', rpb.astype(F32) * LOG2E, sel_c, precision=lax.Precision.HIGHEST)
    tcol = jnp.where(in_c[None, None], tcol - shift, -jnp.inf)
    nq, nk = QROWS * GRID_W, KROWS * GRID_W
    n_dr = 2 * WIN_R - 1
    return pl.pallas_call(
        functools.partial(_bias_kernel, patterns=_natten_patterns(rows)),
        grid=(3, nh),
        in_specs=[pl.BlockSpec((None, n_dr, GRID_W, GRID_W), lambda p, h: (h, 0, 0, 0))],
        out_specs=pl.BlockSpec((None, None, nq, nk), lambda p, h: (p, h, 0, 0)),
        out_shape=jax.ShapeDtypeStruct((3, nh, nq, nk), F32),
        compiler_params=_cparams(("arbitrary", "arbitrary")),
        name="natten_bias",
    )(tcol)


def _natten(px, pc, bias, shift, *, bn, seq_len, lctx, bounded):
    nq = QROWS * GRID_W
    nk = KROWS * GRID_W
    qb = min(NAT_QB, seq_len // nq)
    nstep = seq_len // (qb * nq)
    pw = 2 * DH_B
    n_pair = NH_B // 2
    qcol, kcol, vcol = (blk * PROJ_TN // pw for blk in (BLK_QB, BLK_KB, BLK_VB))
    last = nstep - 1
    first_pat = lambda i: jnp.where(i == 0, 0, 1)
    second_pat = lambda i: jnp.where(i == last, 2, 1)
    return pl.pallas_call(
        functools.partial(_natten_kernel, seq_len=seq_len, bounded=bounded),
        grid=(bn, n_pair, nstep),
        in_specs=[pl.BlockSpec((1, pw), lambda b, p, i: (0, 0)),
                  pl.BlockSpec((qb * nq, pw), lambda b, p, i: (b * nstep + i, qcol + p)),
                  pl.BlockSpec((seq_len, pw), lambda b, p, i: (b, kcol + p)),
                  pl.BlockSpec((seq_len, pw), lambda b, p, i: (b, vcol + p)),
                  pl.BlockSpec((lctx, pw), lambda b, p, i: (b, kcol + p)),
                  pl.BlockSpec((lctx, pw), lambda b, p, i: (b, vcol + p)),
                  pl.BlockSpec((None, 2, nq, nk), lambda b, p, i: (first_pat(i), p, 0, 0)),
                  pl.BlockSpec((None, 2, nq, nk), lambda b, p, i: (1, p, 0, 0)),
                  pl.BlockSpec((None, 2, nq, nk), lambda b, p, i: (second_pat(i), p, 0, 0))],
        out_specs=pl.BlockSpec((qb * nq, pw), lambda b, p, i: (b * nstep + i, p)),
        out_shape=jax.ShapeDtypeStruct((bn * seq_len, NH_B * DH_B), BF16),
        compiler_params=_cparams(("parallel", "parallel", "arbitrary")),
        name="natten_bounded" if bounded else "natten",
    )(jnp.full((1, pw), shift, F32), px, px, px, pc, pc, bias, bias, bias)


def _channel_kernel(h_ref, oa_ref, na_ref, mg_ref, x_ref, mod_ref, gn_ref, g2_ref, wa_ref, wb_ref, wo_ref,
                    wg_ref, wu_ref, wd_ref, o_ref):
    h = h_ref[...]
    parts = []
    for hd in range(NH_A):
        hh = h[:, hd * DV_A:(hd + 1) * DV_A]
        ms = jnp.mean(hh * hh, axis=-1, keepdims=True)
        parts.append(hh * lax.rsqrt(ms + EPS))
    ha = jnp.concatenate(parts, axis=1) * gn_ref[...]
    ha = (ha * jax.nn.sigmoid(oa_ref[...].astype(F32))).astype(BF16)
    d = wa_ref.shape[1]
    gates = jax.nn.sigmoid(mg_ref[...].astype(F32))
    t = (gates[:, :d] * jnp.dot(ha, wa_ref[...], preferred_element_type=F32)
         + gates[:, d:] * jnp.dot(na_ref[...], wb_ref[...], preferred_element_type=F32))
    mix = jnp.dot(t.astype(BF16), wo_ref[...], preferred_element_type=F32)
    x_mid = x_ref[...] + mod_ref[2:3, :] * mix
    hx = _mod_norm(x_mid, g2_ref[...], mod_ref[4:5, :], mod_ref[3:4, :]).astype(BF16)
    a = jnp.dot(hx, wg_ref[...], preferred_element_type=F32)
    u = jnp.dot(hx, wu_ref[...], preferred_element_type=F32)
    act = (a * jax.nn.sigmoid(a) * u).astype(BF16)
    f = jnp.dot(act, wd_ref[...], preferred_element_type=F32)
    o_ref[...] = x_mid + mod_ref[5:6, :] * f


def _channel(h, proj, na, x, mod3, gn, g2, wa, wb, wo, wg, wu, wd, tm=512):
    bn, S, d = x.shape
    tm = min(tm, S)
    nblk = S // tm
    d_a = h.shape[-1]
    d_b = na.shape[-1]
    dff = wg.shape[1]
    oa_blk = BLK_OA * PROJ_TN // d_a
    mg_blk = BLK_MG * PROJ_TN // (2 * d)
    const = lambda b, i: (0, 0)
    tok = lambda b, i: (b * nblk + i, 0)
    res = pl.Buffered(1)
    return pl.pallas_call(
        _channel_kernel,
        grid=(bn, nblk),
        in_specs=[pl.BlockSpec((tm, d_a), tok),
                  pl.BlockSpec((tm, d_a), lambda b, i: (b * nblk + i, oa_blk)),
                  pl.BlockSpec((tm, d_b), tok),
                  pl.BlockSpec((tm, 2 * d), lambda b, i: (b * nblk + i, mg_blk)),
                  pl.BlockSpec((None, tm, d), lambda b, i: (b, i, 0)),
                  pl.BlockSpec((None, N_MOD, d), lambda b, i: (b, 0, 0)),
                  pl.BlockSpec((1, d_a), const),
                  pl.BlockSpec((1, d), const),
                  pl.BlockSpec((d_a, d), const, pipeline_mode=res),
                  pl.BlockSpec((d_b, d), const, pipeline_mode=res),
                  pl.BlockSpec((d, d), const, pipeline_mode=res),
                  pl.BlockSpec((d, dff), const, pipeline_mode=res),
                  pl.BlockSpec((d, dff), const, pipeline_mode=res),
                  pl.BlockSpec((dff, d), const, pipeline_mode=res)],
        out_specs=pl.BlockSpec((None, tm, d), lambda b, i: (b, i, 0)),
        out_shape=jax.ShapeDtypeStruct((bn, S, d), F32),
        compiler_params=_cparams(("parallel", "parallel")),
        name="channel",
    )(h, proj, na, proj, x, mod3, gn, g2.reshape(1, d), wa, wb, wo, wg, wu, wd)


def _rope_tables(S):
    t = np.arange(S)
    row, col = t // GRID_W, t % GRID_W
    half = DK_A // 4
    inv = ROPE_THETA ** (-np.arange(half, dtype=np.float64) / half)
    ang_r = row[:, None] * inv[None, :]
    ang_c = col[:, None] * inv[None, :]
    cos = np.concatenate([np.cos(ang_r)] * 2 + [np.cos(ang_c)] * 2, axis=1)
    sin = np.concatenate([-np.sin(ang_r), np.sin(ang_r), -np.sin(ang_c), np.sin(ang_c)], axis=1)
    reps = LANES // DK_A
    return (jnp.asarray(np.tile(cos, (1, reps)), F32), jnp.asarray(np.tile(sin, (1, reps)), F32))


def kernel(x, c, ctx, c_ctx, w_mod, b_mod, norm1_g, w_in, b_gates, mlstm_norm_g, qn_g, kn_g, rpb,
           w_branch_a, w_branch_b, w_out, norm2_g, w_ffn_gate, w_ffn_up, w_ffn_down):
    bn, S, d = x.shape
    lctx = ctx.shape[1]
    rows = S // GRID_W
    depth = w_mod.shape[0]
    d_a, d_b = NH_A * DV_A, NH_B * DH_B
    dqk = NH_A * DK_A
    sizes = (dqk, dqk, d_a, d_a, 4 * NH_A, d_b, d_b, d_b, 2 * d)
    offs = np.cumsum((0,) + sizes)
    cos_t, sin_t = _rope_tables(S)
    gsum = jnp.asarray(np.kron(np.eye(MXU_TILE // DH_B), np.ones((DH_B, DH_B))), BF16)

    for l in range(depth):
        assert l == depth - 1, "context-stream update for non-final layers is not implemented"
        n_rows = -(-(bn + 1) // 8) * 8
        cc = jnp.concatenate([c, c_ctx[None, :], jnp.zeros((n_rows - bn - 1, d), F32)], axis=0)
        mod3 = _modulation(cc, w_mod[l], b_mod[l]).reshape(n_rows, N_MOD, d)

        w = w_in[l]
        seg = lambda i: w[:, int(offs[i]):int(offs[i + 1])]
        w_main = jnp.concatenate([seg(0), seg(1), seg(2), seg(3), seg(5), seg(6), seg(7), seg(8)],
                                 axis=1).astype(BF16)
        w_g = jnp.pad(seg(4), ((0, 0), (0, LANES - 4 * NH_A))).astype(BF16)
        b_g = jnp.pad(b_gates[l].reshape(1, 4 * NH_A), ((0, 0), (0, LANES - 4 * NH_A)))
        qk_gain = jnp.stack([jnp.tile(qn_g[l], PROJ_TN // DH_B) * (DH_B ** -0.5 * LOG2E),
                             jnp.tile(kn_g[l], PROJ_TN // DH_B)])

        px, ktx, gcx, grx = _mixer_in(x, mod3, norm1_g[l], w_main, w_g, b_g, qk_gain, cos_t, sin_t, gsum,
                                      rope=True, blocks=tuple(range(N_PROJ_BLK)))
        pc, ktc, _, grc = _mixer_in(ctx, mod3, norm1_g[l], w_main, w_g, b_g, qk_gain, cos_t, sin_t, gsum,
                                    rope=False, blocks=CTX_BLOCKS, ctx_row=bn)

        zero_state = jnp.zeros((bn, 2, N_PAIR, 2 * DK_A, C_EXT), F32)
        _, _, st_ctx = _mlstm_states(pc, ktc, grc, zero_state, bn=bn, seq_len=lctx)
        cf, cb, _ = _mlstm_states(px, ktx, grx, st_ctx, bn=bn, seq_len=S)
        h_a = _mlstm_outputs(px, ktx, gcx, grx, cf, cb, bn=bn, seq_len=S)

        shift, bounded = _natten_shift(rpb[l], qn_g[l], kn_g[l])
        bias = _natten_bias(rpb[l], rows, shift)
        na = lax.cond(bounded,
                      functools.partial(_natten, bn=bn, seq_len=S, lctx=lctx, bounded=True),
                      functools.partial(_natten, bn=bn, seq_len=S, lctx=lctx, bounded=False),
                      px, pc, bias, shift)

        x = _channel(h_a, px, na, x, mod3, mlstm_norm_g[l].reshape(1, d_a), norm2_g[l],
                     w_branch_a[l].astype(BF16), w_branch_b[l].astype(BF16), w_out[l].astype(BF16),
                     w_ffn_gate[l].astype(BF16), w_ffn_up[l].astype(BF16), w_ffn_down[l].astype(BF16))
    return x
```
